```python
import jax
import jax.numpy as jnp
from jax import lax
import numpy as np

D_MODEL = 1024
BATCH = 4
SEQ = 8192
DEPTH = 4

CHUNK = 64
N_MIXERS = 3
PLE_DIM = 256
EPS = 1e-6

HG_HEADS = D_MODEL // 128
HG_DK = D_MODEL // HG_HEADS
HG_DV = D_MODEL // HG_HEADS

FOX_HEADS = D_MODEL // 64
FOX_HD = D_MODEL // FOX_HEADS
Q_BLOCK = 128

RG_WIDTH = D_MODEL
RG_BLOCKS = 4
RG_BW = RG_WIDTH // RG_BLOCKS
CONV_W = 4
RG_C = 8.0

N_EXPERTS = 32
TOP_K = 4
D_EXPERT = D_MODEL
SWIGLU_LIMIT = 7.0
SWIGLU_ALPHA = 1.702
EXPERT_BLOCK = 128

N_HG = (DEPTH + 2) // 3
N_FOX = (DEPTH + 1) // 3
N_RG = DEPTH // 3

kernel_name = 'hybrid_hgrn2_fox_rglru_moe'


def rmsnorm(x, g):
    xf = x.astype(jnp.float32)
    y = xf * lax.rsqrt(jnp.mean(xf * xf, axis=-1, keepdims=True) + EPS)
    return (y * g.astype(jnp.float32)).astype(x.dtype)


def hgrn2_mixer(xn, w_in, w_out, g_norm, lb):
    b, s, _ = xn.shape
    nc = s // CHUNK
    q, z, v, g = jnp.split(xn @ w_in, 4, axis=-1)
    q = jax.nn.silu(q.astype(jnp.float32)) * HG_DK ** -0.5
    z = z.astype(jnp.float32)
    lbf = lb.astype(jnp.float32)
    log_f = jnp.logaddexp(jnp.log(lbf), jnp.log1p(-lbf) + jax.nn.log_sigmoid(z))
    k = (1.0 - lbf) * jax.nn.sigmoid(-z)

    def to_chunks(t, dh):
        return t.reshape(b, nc, CHUNK, HG_HEADS, dh).transpose(1, 0, 3, 2, 4)

    qc, kc, lfc = to_chunks(q, HG_DK), to_chunks(k, HG_DK), to_chunks(log_f, HG_DK)
    vc = to_chunks(v.astype(jnp.float32), HG_DV)
    causal = jnp.tril(jnp.ones((CHUNK, CHUNK), dtype=bool))

    def step(state, inp):
        qi, ki, vi, lfi = inp
        cum = jnp.cumsum(lfi, axis=2)
        diff = cum[:, :, :, None, :] - cum[:, :, None, :, :]
        decay = jnp.exp(jnp.where(causal[:, :, None], diff, -jnp.inf))
        scores = jnp.einsum('bhtd,bhtsd,bhsd->bhts', qi, decay, ki)
        o = (jnp.einsum('bhts,bhsv->bhtv', scores, vi)
             + jnp.einsum('bhtd,bhdv->bhtv', qi * jnp.exp(cum), state))
        last = cum[:, :, -1:, :]
        state = (jnp.exp(last[:, :, 0, :, None]) * state
                 + jnp.einsum('bhsd,bhsv->bhdv', ki * jnp.exp(last - cum), vi))
        return state, o

    state0 = jnp.zeros((b, HG_HEADS, HG_DK, HG_DV), jnp.float32)
    _, o = lax.scan(step, state0, (qc, kc, vc, lfc))
    o = o.transpose(1, 0, 3, 2, 4).reshape(b, s, HG_HEADS, HG_DV)
    o = rmsnorm(o, g_norm) * jax.nn.silu(g.astype(jnp.float32)).reshape(b, s, HG_HEADS, HG_DV)
    return o.reshape(b, s, D_MODEL).astype(xn.dtype) @ w_out


def fox_mixer(xn, w_in, f_bias, q_norm, k_norm, w_out):
    b, s, _ = xn.shape
    proj = xn @ w_in
    q, k, v, g = jnp.split(proj[..., :4 * D_MODEL], 4, axis=-1)
    f_logit = (proj[..., 4 * D_MODEL:] + f_bias).astype(jnp.float32)
    q = rmsnorm(q.reshape(b, s, FOX_HEADS, FOX_HD), q_norm).transpose(0, 2, 1, 3)
    k = rmsnorm(k.reshape(b, s, FOX_HEADS, FOX_HD), k_norm).transpose(0, 2, 1, 3)
    v = v.reshape(b, s, FOX_HEADS, FOX_HD).transpose(0, 2, 1, 3)
    cum = jnp.cumsum(jax.nn.log_sigmoid(f_logit), axis=1).transpose(0, 2, 1)
    scale = FOX_HD ** -0.5
    outs = []
    for blk in range(s // Q_BLOCK):
        q0, q1 = blk * Q_BLOCK, (blk + 1) * Q_BLOCK
        logits = jnp.einsum('bhqd,bhkd->bhqk', q[:, :, q0:q1], k[:, :, :q1]).astype(jnp.float32) * scale
        logits = logits + cum[:, :, q0:q1, None] - cum[:, :, None, :q1]
        mask = jnp.arange(q1)[None, :] <= jnp.arange(q0, q1)[:, None]
        probs = jax.nn.softmax(jnp.where(mask, logits, -jnp.inf), axis=-1)
        outs.append(jnp.einsum('bhqk,bhkd->bhqd', probs.astype(v.dtype), v[:, :, :q1]))
    o = jnp.concatenate(outs, axis=2).transpose(0, 2, 1, 3).reshape(b, s, D_MODEL)
    return (o * jax.nn.sigmoid(g)) @ w_out


def _linear_combine(left, right):
    a_l, b_l = left
    a_r, b_r = right
    return a_l * a_r, a_r * b_l + b_r


def rglru_mixer(xn, w_in, conv_w, conv_b, w_a, b_a, w_x, b_x, lam, w_out):
    b, s, _ = xn.shape
    gate, u = jnp.split(xn @ w_in, 2, axis=-1)
    gate = jax.nn.gelu(gate, approximate=True)
    up = jnp.pad(u, ((0, 0), (CONV_W - 1, 0), (0, 0)))
    u = conv_b + up[:, 0:s] * conv_w[0]
    for tap in range(1, CONV_W):
        u = u + up[:, tap:tap + s] * conv_w[tap]
    ub = u.reshape(b, s, RG_BLOCKS, RG_BW)
    r = jax.nn.sigmoid(jnp.einsum('bsnd,nde->bsne', ub, w_a).reshape(b, s, RG_WIDTH) + b_a)
    i = jax.nn.sigmoid(jnp.einsum('bsnd,nde->bsne', ub, w_x).reshape(b, s, RG_WIDTH) + b_x)
    log_a = -RG_C * r.astype(jnp.float32) * jax.nn.softplus(-lam.astype(jnp.float32))
    a = jnp.exp(log_a)
    inp = jnp.sqrt(-jnp.expm1(2.0 * log_a)) * (i * u).astype(jnp.float32)
    _, h = lax.associative_scan(_linear_combine, (a, inp), axis=1)
    return (h.astype(xn.dtype) * gate) @ w_out


def moe(xn, w_r, b_r, w_gu, b_gu, w_dn, b_dn):
    b, s, d = xn.shape
    n = b * s
    xt = xn.reshape(n, d)
    logits = (xt @ w_r + b_r).astype(jnp.float32)
    top_val, top_idx = lax.top_k(logits, TOP_K)
    gates = jax.nn.softmax(top_val, axis=-1).astype(xn.dtype).reshape(-1)
    flat_e = top_idx.reshape(-1)
    order = jnp.argsort(flat_e)
    e_sorted = flat_e[order]
    counts = jnp.bincount(flat_e, length=N_EXPERTS)
    padded = (counts + EXPERT_BLOCK - 1) // EXPERT_BLOCK * EXPERT_BLOCK
    padded_end = jnp.cumsum(padded)
    start = jnp.cumsum(counts) - counts
    dest = padded_end[e_sorted] - padded[e_sorted] + jnp.arange(n * TOP_K) - start[e_sorted]
    n_blocks = -(-(n * TOP_K) // EXPERT_BLOCK) + N_EXPERTS
    cap = n_blocks * EXPERT_BLOCK
    buf_tok = jnp.full((cap,), n, jnp.int32).at[dest].set((order // TOP_K).astype(jnp.int32))
    buf_gate = jnp.zeros((cap,), xn.dtype).at[dest].set(gates[order])
    block_expert = jnp.minimum(
        jnp.searchsorted(padded_end, jnp.arange(n_blocks) * EXPERT_BLOCK, side='right'), N_EXPERTS - 1)
    xp = jnp.concatenate([xt, jnp.zeros((1, d), xt.dtype)], axis=0)

    def expert_block(args):
        tok, gate, e = args
        hgu = xp[tok] @ w_gu[e] + b_gu[e]
        glu, lin = jnp.split(hgu, 2, axis=-1)
        glu = jnp.minimum(glu, SWIGLU_LIMIT)
        lin = jnp.clip(lin, -SWIGLU_LIMIT, SWIGLU_LIMIT)
        act = glu * jax.nn.sigmoid(SWIGLU_ALPHA * glu) * (lin + 1.0)
        return (act @ w_dn[e] + b_dn[e]) * gate[:, None]

    ys = lax.map(expert_block, (buf_tok.reshape(n_blocks, EXPERT_BLOCK),
                                buf_gate.reshape(n_blocks, EXPERT_BLOCK), block_expert))
    out = jnp.zeros((n + 1, d), xn.dtype).at[buf_tok].add(ys.reshape(cap, d))
    return out[:n].reshape(b, s, d)


def setup_inputs(seed: int = 0) -> dict:
    key = jax.random.key(seed)
    ks = iter(jax.random.split(key, 40))
    d = D_MODEL
    out_scale = d ** -0.5 * (2 * DEPTH) ** -0.5

    def nrm(shape, scale):
        return jax.random.normal(next(ks), shape, jnp.float32) * scale

    def gain(shape):
        return 1.0 + nrm(shape, 0.05)

    x = nrm((BATCH, SEQ, d), 1.0)
    p = nrm((DEPTH, BATCH, SEQ, PLE_DIM), 1.0)
    norm_mix = gain((DEPTH, d))
    norm_ffn = gain((DEPTH, d))
    hg_w_in = nrm((N_HG, d, 4 * d), d ** -0.5)
    hg_w_out = nrm((N_HG, d, d), out_scale)
    hg_gnorm = gain((N_HG, HG_DV))
    hg_lb_param = nrm((DEPTH, d), 0.5)
    fox_w_in = nrm((N_FOX, d, 4 * d + FOX_HEADS), d ** -0.5)
    fox_f_bias = 2.0 + nrm((N_FOX, FOX_HEADS), 0.5)
    fox_qnorm = gain((N_FOX, FOX_HD))
    fox_knorm = gain((N_FOX, FOX_HD))
    fox_w_out = nrm((N_FOX, d, d), out_scale)
    rg_w_in = nrm((N_RG, d, 2 * RG_WIDTH), d ** -0.5)
    rg_conv_w = nrm((N_RG, CONV_W, RG_WIDTH), CONV_W ** -0.5)
    rg_conv_b = nrm((N_RG, RG_WIDTH), 0.02)
    rg_wa = nrm((N_RG, RG_BLOCKS, RG_BW, RG_BW), RG_BW ** -0.5)
    rg_ba = nrm((N_RG, RG_WIDTH), 0.1)
    rg_wx = nrm((N_RG, RG_BLOCKS, RG_BW, RG_BW), RG_BW ** -0.5)
    rg_bx = nrm((N_RG, RG_WIDTH), 0.1)
    a_c = jax.random.uniform(next(ks), (N_RG, RG_WIDTH), jnp.float32, 0.9, 0.999)
    sig = a_c ** (1.0 / RG_C)
    rg_lambda = jnp.log(sig) - jnp.log1p(-sig)
    rg_w_out = nrm((N_RG, RG_WIDTH, d), out_scale)
    router_w = nrm((DEPTH, d, N_EXPERTS), d ** -0.5)
    router_b = nrm((DEPTH, N_EXPERTS), 0.01)
    moe_w_gu = nrm((DEPTH, N_EXPERTS, d, 2 * D_EXPERT), d ** -0.5)
    moe_b_gu = nrm((DEPTH, N_EXPERTS, 2 * D_EXPERT), 0.02)
    moe_w_dn = nrm((DEPTH, N_EXPERTS, D_EXPERT, d), D_EXPERT ** -0.5 * (2 * DEPTH) ** -0.5)
    moe_b_dn = nrm((DEPTH, N_EXPERTS, d), 0.02)
    ple_w = nrm((DEPTH, PLE_DIM, d), PLE_DIM ** -0.5)
    ple_norm = gain((DEPTH, d))
    ple_gate_norm = gain((DEPTH, d))
    ple_gate_w = nrm((DEPTH, d, d), d ** -0.5)
    return {'x': x, 'p': p, 'norm_mix': norm_mix, 'norm_ffn': norm_ffn,
            'hg_w_in': hg_w_in, 'hg_w_out': hg_w_out, 'hg_gnorm': hg_gnorm, 'hg_lb_param': hg_lb_param,
            'fox_w_in': fox_w_in, 'fox_f_bias': fox_f_bias, 'fox_qnorm': fox_qnorm, 'fox_knorm': fox_knorm,
            'fox_w_out': fox_w_out,
            'rg_w_in': rg_w_in, 'rg_conv_w': rg_conv_w, 'rg_conv_b': rg_conv_b, 'rg_wa': rg_wa, 'rg_ba': rg_ba,
            'rg_wx': rg_wx, 'rg_bx': rg_bx, 'rg_lambda': rg_lambda, 'rg_w_out': rg_w_out,
            'router_w': router_w, 'router_b': router_b, 'moe_w_gu': moe_w_gu, 'moe_b_gu': moe_b_gu,
            'moe_w_dn': moe_w_dn, 'moe_b_dn': moe_b_dn,
            'ple_w': ple_w, 'ple_norm': ple_norm, 'ple_gate_norm': ple_gate_norm, 'ple_gate_w': ple_gate_w}


def reference(x, p, norm_mix, norm_ffn, hg_w_in, hg_w_out, hg_gnorm, hg_lb_param,
              fox_w_in, fox_f_bias, fox_qnorm, fox_knorm, fox_w_out,
              rg_w_in, rg_conv_w, rg_conv_b, rg_wa, rg_ba, rg_wx, rg_bx, rg_lambda, rg_w_out,
              router_w, router_b, moe_w_gu, moe_b_gu, moe_w_dn, moe_b_dn,
              ple_w, ple_norm, ple_gate_norm, ple_gate_w):
    lb_all = jnp.cumsum(jax.nn.softmax(hg_lb_param.astype(jnp.float32), axis=0), axis=0)
    lb_all = lb_all - lb_all[0]
    h = x
    for i in range(DEPTH):
        xn = rmsnorm(h, norm_mix[i])
        j = i // N_MIXERS
        kind = i % N_MIXERS
        if kind == 0:
            mix = hgrn2_mixer(xn, hg_w_in[j], hg_w_out[j], hg_gnorm[j], lb_all[i])
        elif kind == 1:
            mix = fox_mixer(xn, fox_w_in[j], fox_f_bias[j], fox_qnorm[j], fox_knorm[j], fox_w_out[j])
        else:
            mix = rglru_mixer(xn, rg_w_in[j], rg_conv_w[j], rg_conv_b[j], rg_wa[j], rg_ba[j],
                              rg_wx[j], rg_bx[j], rg_lambda[j], rg_w_out[j])
        h = h + mix
        h = h + moe(rmsnorm(h, norm_ffn[i]), router_w[i], router_b[i],
                    moe_w_gu[i], moe_b_gu[i], moe_w_dn[i], moe_b_dn[i])
        ple = (rmsnorm(p[i] @ ple_w[i], ple_norm[i])
               * jax.nn.sigmoid(rmsnorm(h, ple_gate_norm[i]) @ ple_gate_w[i]))
        h = h + ple
    return h
```

```python
import functools

import jax
import jax.numpy as jnp
import numpy as np
from jax import lax
from jax.experimental import pallas as pl
from jax.experimental.pallas import tpu as pltpu

F32 = jnp.float32
BF16 = jnp.bfloat16
I32 = jnp.int32

D_MODEL = 1024
EPS = 1e-6
PLE_DIM = 256

HG_HEADS = 8
HG_DK = 128
HG_CHUNK = 128
HG_LEVELS = 7

FOX_HEADS = 16
FOX_HD = 64

RG_BLOCKS = 4
RG_BW = 256
CONV_W = 4
RG_C = 8.0

N_EXPERTS = 32
TOP_K = 4
SWIGLU_LIMIT = 7.0
SWIGLU_ALPHA = 1.702
MOE_BLK = 256

VMEM_LIMIT = 56 * 1024 * 1024

HI = lax.Precision.HIGHEST


def _cparams(sem):
    return pltpu.CompilerParams(dimension_semantics=sem, vmem_limit_bytes=VMEM_LIMIT)


def _bdot(a, b):
    return jnp.dot(a.astype(BF16), b.astype(BF16), preferred_element_type=F32)


def _bdot_nt(a, b):
    return lax.dot_general(a.astype(BF16), b.astype(BF16), (((1,), (1,)), ((), ())),
                           preferred_element_type=F32)


def _bdot_tn(a, b):
    return lax.dot_general(a.astype(BF16), b.astype(BF16), (((0,), (0,)), ((), ())),
                           preferred_element_type=F32)


def _rms(x, g):
    return x * lax.rsqrt(jnp.mean(x * x, axis=-1, keepdims=True) + EPS) * g


def _split2(x):
    hi = x.astype(BF16)
    lo = (x - hi.astype(F32)).astype(BF16)
    return hi, lo


def _split3(x):
    hi = x.astype(BF16)
    r = x - hi.astype(F32)
    mid = r.astype(BF16)
    lo = (r - mid.astype(F32)).astype(BF16)
    return hi, mid, lo


def _log_sigmoid(z):
    return jnp.minimum(z, 0.0) - jnp.log1p(jnp.exp(-jnp.abs(z)))


def _full(shape):
    return pl.BlockSpec(shape, lambda *_: (0,) * len(shape))


def _norm_proj_kernel(h_ref, g_ref, w_ref, o_ref, *, cn):
    xn = _rms(h_ref[...], g_ref[...]).astype(BF16)
    m = w_ref.shape[1]
    for c in range(m // cn):
        o_ref[:, c * cn:(c + 1) * cn] = jnp.dot(
            xn, w_ref[:, c * cn:(c + 1) * cn], preferred_element_type=F32).astype(o_ref.dtype)


def _norm_proj(h, g, w, tm=256, out_dtype=F32):
    n, d = h.shape
    m = w.shape[1]
    return pl.pallas_call(
        functools.partial(_norm_proj_kernel, cn=512),
        grid=(n // tm,),
        in_specs=[pl.BlockSpec((tm, d), lambda i: (i, 0)), _full((1, d)), _full((d, m))],
        out_specs=pl.BlockSpec((tm, m), lambda i: (i, 0)),
        out_shape=jax.ShapeDtypeStruct((n, m), out_dtype),
        compiler_params=_cparams(("parallel",)),
        name="norm_proj",
    )(h, g.reshape(1, d), w)


def _hgrn2_consts():
    c = HG_CHUNK
    t = np.arange(c)
    tril = (t[:, None] >= t[None, :]).astype(np.float32)
    sel = np.zeros((HG_LEVELS, c, c), np.float32)
    for l in range(HG_LEVELS):
        hs = 1 << l
        m = (t // (2 * hs)) * (2 * hs) + hs - 1
        sel[l, t, m] = 1.0
    return jnp.asarray(tril, BF16), jnp.asarray(sel.reshape(HG_LEVELS * c, c), BF16)


def _hgrn2_kernel(q_ref, z_ref, v_ref, g_ref, par_ref, tril_ref, sel_ref, o_ref, st_ref, *, nchunk):
    c = HG_CHUNK

    @pl.when(pl.program_id(2) == 0)
    def _():
        st_ref[...] = jnp.zeros_like(st_ref)

    log_lb = par_ref[0:1, :]
    log1m_lb = par_ref[1:2, :]
    one_m_lb = par_ref[2:3, :]
    gnorm = par_ref[3:4, :]
    row = lax.broadcasted_iota(I32, (c, c), 0)
    col = lax.broadcasted_iota(I32, (c, c), 1)

    def chunk(ci, carry):
        r = pl.ds(pl.multiple_of(ci * c, c), c)
        qr = q_ref[r, :]
        z = z_ref[r, :]
        vb = v_ref[r, :].astype(BF16)
        gt = g_ref[r, :]
        q = qr * jax.nn.sigmoid(qr) * (HG_DK ** -0.5)
        b = log1m_lb + _log_sigmoid(z)
        lf = jnp.maximum(log_lb, b) + jnp.log1p(jnp.exp(-jnp.abs(log_lb - b)))
        k = one_m_lb * jax.nn.sigmoid(-z)
        tril = tril_ref[...]
        hi, mid, lo = _split3(lf)
        cum = (jnp.dot(tril, hi, preferred_element_type=F32)
               + jnp.dot(tril, mid, preferred_element_type=F32)
               + jnp.dot(tril, lo, preferred_element_type=F32))
        chi, clo = _split2(cum)
        refs = (jnp.dot(sel_ref[...], chi, preferred_element_type=F32)
                + jnp.dot(sel_ref[...], clo, preferred_element_type=F32))
        a = jnp.where(row == col, _bdot_nt(q, k), 0.0)
        for l in range(HG_LEVELS):
            hs = 1 << l
            e = jnp.exp(-jnp.abs(cum - refs[l * c:(l + 1) * c, :]))
            right = (row & hs) != 0
            x = jnp.where(right, q, k) * e
            ql = jnp.where(right, x, 0.0)
            kl = jnp.where(right, 0.0, x)
            a = a + jnp.where((row >> (l + 1)) == (col >> (l + 1)), _bdot_nt(ql, kl), 0.0)
        st = st_ref[...]
        o = _bdot(a, vb) + _bdot_nt(q * jnp.exp(cum), st)
        last = cum[c - 1:c, :]
        kd = k * jnp.exp(last - cum)
        st_ref[...] = st * jnp.exp(last) + _bdot_tn(vb, kd)
        y = _rms(o, gnorm) * (gt * jax.nn.sigmoid(gt))
        o_ref[r, :] = y.astype(o_ref.dtype)
        return carry

    lax.fori_loop(0, nchunk, chunk, 0)


def _hgrn2_recurrence(proj, par, batch, seq, tt=512):
    n = batch * seq
    nt = seq // tt
    tril, sel = _hgrn2_consts()

    def part(p):
        return pl.BlockSpec((tt, HG_DK), lambda b, h, i, p=p: (b * nt + i, p * HG_HEADS + h))

    return pl.pallas_call(
        functools.partial(_hgrn2_kernel, nchunk=tt // HG_CHUNK),
        grid=(batch, HG_HEADS, nt),
        in_specs=[part(0), part(1), part(2), part(3),
                  pl.BlockSpec((8, HG_DK), lambda b, h, i: (0, h)),
                  _full(tril.shape), _full(sel.shape)],
        out_specs=pl.BlockSpec((tt, HG_DK), lambda b, h, i: (b * nt + i, h)),
        out_shape=jax.ShapeDtypeStruct((n, D_MODEL), BF16),
        scratch_shapes=[pltpu.VMEM((HG_DK, HG_DK), F32)],
        compiler_params=_cparams(("parallel", "parallel", "arbitrary")),
        name="hgrn2_recurrence",
    )(proj, proj, proj, proj, par, tril, sel)


def _hgrn2_mixer(h, g_mix, w_in, w_out_unused, g_norm, lb, batch, seq):
    del w_out_unused
    proj = _norm_proj(h, g_mix, w_in.astype(BF16))
    par = jnp.zeros((8, D_MODEL), F32)
    par = par.at[0].set(jnp.log(lb)).at[1].set(jnp.log1p(-lb)).at[2].set(1.0 - lb)
    par = par.at[3].set(jnp.tile(g_norm.astype(F32), HG_HEADS))
    return _hgrn2_recurrence(proj, par, batch, seq)


def _fox_proj_kernel(h_ref, g_ref, w_ref, wf_ref, fb_ref, qg_ref, kg_ref, gs_ref, gst_ref, tril_ref,
                     o_ref, cum_ref, carry_ref, *, tiles_per_seq):
    d = D_MODEL

    @pl.when(pl.program_id(0) % tiles_per_seq == 0)
    def _():
        carry_ref[...] = jnp.zeros_like(carry_ref)

    xn = _rms(h_ref[...], g_ref[...])
    xb = xn.astype(BF16)

    def headnorm(t, gain):
        shi, slo = _split2(t * t)
        ssq = (jnp.dot(shi, gs_ref[...], preferred_element_type=F32)
               + jnp.dot(slo, gs_ref[...], preferred_element_type=F32))
        inv = lax.rsqrt(ssq * (1.0 / FOX_HD) + EPS)
        ihi, ilo = _split2(inv)
        invf = (jnp.dot(ihi, gst_ref[...], preferred_element_type=F32)
                + jnp.dot(ilo, gst_ref[...], preferred_element_type=F32))
        return t * invf * gain

    q = jnp.dot(xb, w_ref[:, 0:d], preferred_element_type=F32)
    o_ref[:, 0:d] = (headnorm(q, qg_ref[...]) * (FOX_HD ** -0.5)).astype(o_ref.dtype)
    k = jnp.dot(xb, w_ref[:, d:2 * d], preferred_element_type=F32)
    o_ref[:, d:2 * d] = headnorm(k, kg_ref[...]).astype(o_ref.dtype)
    for c in range(2, 4):
        o_ref[:, c * d:(c + 1) * d] = jnp.dot(
            xb, w_ref[:, c * d:(c + 1) * d], preferred_element_type=F32).astype(o_ref.dtype)

    fl = jnp.dot(xn, wf_ref[...], preferred_element_type=F32, precision=HI) + fb_ref[...]
    hi, mid, lo = _split3(_log_sigmoid(fl))
    tril = tril_ref[...]
    cum = (jnp.dot(tril, hi, preferred_element_type=F32)
           + jnp.dot(tril, mid, preferred_element_type=F32)
           + jnp.dot(tril, lo, preferred_element_type=F32)) + carry_ref[0:1, :]
    cum_ref[...] = cum
    tm = cum.shape[0]
    carry_ref[...] = jnp.broadcast_to(cum[tm - 1:tm, :], carry_ref.shape)


def _fox_proj(h, g_mix, w_in, f_bias, q_norm, k_norm, seq, tm=256):
    n, d = h.shape
    w_main = w_in[:, :4 * d].astype(BF16)
    w_f = jnp.zeros((d, 128), F32).at[:, :FOX_HEADS].set(w_in[:, 4 * d:].astype(F32))
    fb = jnp.zeros((1, 128), F32).at[0, :FOX_HEADS].set(f_bias.astype(F32))
    head_of = np.arange(d) // FOX_HD
    gs_np = (head_of[:, None] == np.arange(128)[None, :]).astype(np.float32)
    gs = jnp.asarray(gs_np, BF16)
    gst = jnp.asarray(gs_np.T, BF16)
    tril = jnp.asarray(np.tril(np.ones((tm, tm), np.float32)), BF16)
    qg = jnp.tile(q_norm.astype(F32), FOX_HEADS).reshape(1, d)
    kg = jnp.tile(k_norm.astype(F32), FOX_HEADS).reshape(1, d)
    return pl.pallas_call(
        functools.partial(_fox_proj_kernel, tiles_per_seq=seq // tm),
        grid=(n // tm,),
        in_specs=[pl.BlockSpec((tm, d), lambda i: (i, 0)), _full((1, d)), _full((d, 4 * d)),
                  _full((d, 128)), _full((1, 128)), _full((1, d)), _full((1, d)),
                  _full((d, 128)), _full((128, d)), _full((tm, tm))],
        out_specs=[pl.BlockSpec((tm, 4 * d), lambda i: (i, 0)),
                   pl.BlockSpec((tm, 128), lambda i: (i, 0))],
        out_shape=[jax.ShapeDtypeStruct((n, 4 * d), BF16), jax.ShapeDtypeStruct((n, 128), F32)],
        scratch_shapes=[pltpu.VMEM((8, 128), F32)],
        compiler_params=_cparams(("arbitrary",)),
        name="fox_proj",
    )(h, g_mix.reshape(1, d), w_main, w_f, fb, qg, kg, gs, gst, tril)


def _fox_attn_kernel(qi_ref, kj_ref, q_ref, k_ref, v_ref, g_ref, cq_ref, ck_ref, o_ref,
                     qh_ref, m_ref, l_ref, acc_ref, *, tq):
    step = pl.program_id(2)
    qi = qi_ref[step]
    kj = kj_ref[step]
    lane = lax.broadcasted_iota(I32, (tq, 128), 1)

    @pl.when(kj == 0)
    def _():
        qv = q_ref[...]
        qh_ref[0] = jnp.where(lane < FOX_HD, qv, jnp.zeros_like(qv))
        qh_ref[1] = jnp.where(lane < FOX_HD, jnp.zeros_like(qv), qv)
        m_ref[...] = jnp.full_like(m_ref, -jnp.inf)
        l_ref[...] = jnp.zeros_like(l_ref)
        acc_ref[...] = jnp.zeros_like(acc_ref)

    def sweep(masked):
        kb = k_ref[...]
        vb = v_ref[...]
        for hh in range(2):
            s = lax.dot_general(qh_ref[hh], kb, (((1,), (1,)), ((), ())), preferred_element_type=F32)
            s = s + (cq_ref[hh:hh + 1, 0:1] - ck_ref[hh:hh + 1, :])
            if masked:
                r = lax.broadcasted_iota(I32, s.shape, 0)
                cc = lax.broadcasted_iota(I32, s.shape, 1)
                s = jnp.where(cc <= r, s, -jnp.inf)
            m_old = m_ref[hh]
            m_new = jnp.maximum(m_old, jnp.max(s, axis=-1, keepdims=True))
            alpha = jnp.exp(m_old - m_new)
            p = jnp.exp(s - m_new[:, 0:1])
            l_ref[hh] = alpha * l_ref[hh] + jnp.sum(p, axis=-1, keepdims=True)
            acc_ref[hh] = alpha * acc_ref[hh] + jnp.dot(p.astype(BF16), vb, preferred_element_type=F32)
            m_ref[hh] = m_new

    @pl.when(kj < qi)
    def _():
        sweep(False)

    @pl.when(kj == qi)
    def _():
        sweep(True)
        o = jnp.where(lane < FOX_HD, acc_ref[0] / l_ref[0], acc_ref[1] / l_ref[1])
        gt = g_ref[...].astype(F32)
        o_ref[...] = (o * jax.nn.sigmoid(gt)).astype(o_ref.dtype)


def _fox_attention(qkvg, cum, batch, seq, tq=512):
    n = batch * seq
    nq = seq // tq
    hp = FOX_HEADS // 2
    pairs = [(i, j) for i in range(nq) for j in range(i + 1)]
    qi = jnp.asarray([p[0] for p in pairs], I32)
    kj = jnp.asarray([p[1] for p in pairs], I32)
    cum_t = cum[:, :FOX_HEADS].reshape(batch, seq, hp, 2).transpose(0, 2, 3, 1).reshape(batch * hp, 2, seq)
    grid_spec = pltpu.PrefetchScalarGridSpec(
        num_scalar_prefetch=2,
        grid=(batch, hp, len(pairs)),
        in_specs=[
            pl.BlockSpec((tq, 128), lambda b, h, s, qi, kj: (b * nq + qi[s], h)),
            pl.BlockSpec((tq, 128), lambda b, h, s, qi, kj: (b * nq + kj[s], hp + h)),
            pl.BlockSpec((tq, 128), lambda b, h, s, qi, kj: (b * nq + kj[s], 2 * hp + h)),
            pl.BlockSpec((tq, 128), lambda b, h, s, qi, kj: (b * nq + qi[s], 3 * hp + h)),
            pl.BlockSpec((None, 2, tq), lambda b, h, s, qi, kj: (b * hp + h, 0, qi[s])),
            pl.BlockSpec((None, 2, tq), lambda b, h, s, qi, kj: (b * hp + h, 0, kj[s])),
        ],
        out_specs=pl.BlockSpec((tq, 128), lambda b, h, s, qi, kj: (b * nq + qi[s], h)),
        scratch_shapes=[pltpu.VMEM((2, tq, 128), BF16), pltpu.VMEM((2, tq, 128), F32),
                        pltpu.VMEM((2, tq, 128), F32), pltpu.VMEM((2, tq, 128), F32)],
    )
    return pl.pallas_call(
        functools.partial(_fox_attn_kernel, tq=tq),
        grid_spec=grid_spec,
        out_shape=jax.ShapeDtypeStruct((n, D_MODEL), BF16),
        compiler_params=_cparams(("parallel", "parallel", "arbitrary")),
        name="fox_attention",
    )(qi, kj, qkvg, qkvg, qkvg, qkvg, cum_t, cum_t)


def _fox_mixer(h, g_mix, w_in, f_bias, q_norm, k_norm, batch, seq):
    qkvg, cum = _fox_proj(h, g_mix, w_in, f_bias, q_norm, k_norm, seq)
    return _fox_attention(qkvg, cum, batch, seq)


def _rglru_kernel(gate_ref, u_ref, par_ref, wa_ref, wx_ref, o_ref, prev_ref, hc_ref, *, tt):
    @pl.when(pl.program_id(1) == 0)
    def _():
        prev_ref[...] = jnp.zeros_like(prev_ref)
        hc_ref[...] = jnp.zeros_like(hc_ref)

    u = u_ref[...]
    ext = jnp.concatenate([prev_ref[...], u], axis=0)
    conv = par_ref[4:5, :] + u * par_ref[3:4, :]
    for shift in range(1, CONV_W):
        conv = conv + pltpu.roll(ext, shift, 0)[8:, :] * par_ref[3 - shift:4 - shift, :]
    prev_ref[...] = u[tt - 8:, :]

    cb = conv.astype(BF16)
    ra, ia = [], []
    for nb in range(RG_BLOCKS):
        blk = cb[:, nb * RG_BW:(nb + 1) * RG_BW]
        ra.append(jnp.dot(blk, wa_ref[nb], preferred_element_type=F32))
        ia.append(jnp.dot(blk, wx_ref[nb], preferred_element_type=F32))
    r = jax.nn.sigmoid(jnp.concatenate(ra, axis=1) + par_ref[5:6, :])
    ig = jax.nn.sigmoid(jnp.concatenate(ia, axis=1) + par_ref[6:7, :])
    lam = par_ref[7:8, :]
    softplus = jnp.maximum(-lam, 0.0) + jnp.log1p(jnp.exp(-jnp.abs(lam)))
    log_a = -RG_C * r * softplus
    a = jnp.exp(log_a)
    b = jnp.sqrt(1.0 - a * a) * (ig * conv)

    row = lax.broadcasted_iota(I32, a.shape, 0)
    dist = 1
    while dist < tt:
        ok = row >= dist
        a_sh = pltpu.roll(a, dist, 0)
        b_sh = pltpu.roll(b, dist, 0)
        b = jnp.where(ok, a * b_sh + b, b)
        a = jnp.where(ok, a * a_sh, a)
        dist *= 2
    hs = b + a * hc_ref[0:1, :]
    hc_ref[...] = jnp.broadcast_to(hs[tt - 1:tt, :], hc_ref.shape)

    gt = gate_ref[...]
    gelu = 0.5 * gt * (1.0 + jnp.tanh(0.7978845608028654 * (gt + 0.044715 * gt * gt * gt)))
    o_ref[...] = (hs * gelu).astype(o_ref.dtype)


def _rglru_mixer(h, g_mix, w_in, conv_w, conv_b, w_a, b_a, w_x, b_x, lam, batch, seq, tt=256):
    n = batch * seq
    nt = seq // tt
    w = D_MODEL
    proj = _norm_proj(h, g_mix, w_in.astype(BF16))
    par = jnp.concatenate([conv_w.astype(F32), conv_b.reshape(1, w), b_a.reshape(1, w),
                           b_x.reshape(1, w), lam.reshape(1, w)], axis=0).astype(F32)
    return pl.pallas_call(
        functools.partial(_rglru_kernel, tt=tt),
        grid=(batch, nt),
        in_specs=[pl.BlockSpec((tt, w), lambda b, i: (b * nt + i, 0)),
                  pl.BlockSpec((tt, w), lambda b, i: (b * nt + i, 1)),
                  _full((8, w)), _full((RG_BLOCKS, RG_BW, RG_BW)), _full((RG_BLOCKS, RG_BW, RG_BW))],
        out_specs=pl.BlockSpec((tt, w), lambda b, i: (b * nt + i, 0)),
        out_shape=jax.ShapeDtypeStruct((n, w), BF16),
        scratch_shapes=[pltpu.VMEM((8, w), F32), pltpu.VMEM((8, w), F32)],
        compiler_params=_cparams(("parallel", "arbitrary")),
        name="rglru",
    )(proj, proj, par, w_a.astype(BF16), w_x.astype(BF16))


def _outproj_router_kernel(a_ref, h_ref, wo_ref, g_ref, wr_ref, br_ref, tri_ref,
                           hm_ref, xn_ref, idx_ref, gate_ref, rank_ref, cnt_ref):
    @pl.when(pl.program_id(0) == 0)
    def _():
        cnt_ref[...] = jnp.zeros_like(cnt_ref)

    hm = h_ref[...] + jnp.dot(a_ref[...], wo_ref[...], preferred_element_type=F32)
    hm_ref[...] = hm
    xn = _rms(hm, g_ref[...])
    xn_ref[...] = xn
    logit = lax.dot_general(wr_ref[...], xn, (((1,), (1,)), ((), ())),
                            preferred_element_type=F32, precision=HI) + br_ref[:, 0:1]
    ne, tm = logit.shape
    eidx = lax.broadcasted_iota(I32, (ne, tm), 0)
    work = logit
    vals, idxs, hots = [], [], []
    for _ in range(TOP_K):
        mx = jnp.max(work, axis=0, keepdims=True)
        pick = jnp.min(jnp.where(work == mx, eidx, ne), axis=0, keepdims=True)
        hot = eidx == pick
        work = jnp.where(hot, -jnp.inf, work)
        vals.append(mx)
        idxs.append(pick)
        hots.append(hot)
    ex = [jnp.exp(v - vals[0]) for v in vals]
    den = ex[0] + ex[1] + ex[2] + ex[3]
    gate_ref[...] = jnp.concatenate([e / den for e in ex], axis=0)
    idx_ref[...] = jnp.concatenate(idxs, axis=0)
    tok_hot = (hots[0] | hots[1] | hots[2] | hots[3]).astype(BF16)
    before = jnp.dot(tok_hot, tri_ref[...], preferred_element_type=F32) + cnt_ref[:, 0:1]
    rank_ref[...] = jnp.concatenate(
        [jnp.sum(jnp.where(h, before, 0.0), axis=0, keepdims=True) for h in hots], axis=0).astype(I32)
    cnt_ref[...] = cnt_ref[...] + jnp.sum(tok_hot.astype(F32), axis=1, keepdims=True)


def _outproj_router(act, h, w_out, g_ffn, w_r, b_r, tm=256):
    n, d = h.shape
    tri = jnp.asarray(np.triu(np.ones((tm, tm), np.float32), 1), BF16)
    row = lambda i: (i, 0)
    colb = lambda i: (0, i)
    return pl.pallas_call(
        _outproj_router_kernel,
        grid=(n // tm,),
        in_specs=[pl.BlockSpec((tm, d), row), pl.BlockSpec((tm, d), row), _full((d, d)), _full((1, d)),
                  _full((N_EXPERTS, d)), _full((N_EXPERTS, 128)), _full((tm, tm))],
        out_specs=[pl.BlockSpec((tm, d), row), pl.BlockSpec((tm, d), row),
                   pl.BlockSpec((TOP_K, tm), colb), pl.BlockSpec((TOP_K, tm), colb),
                   pl.BlockSpec((TOP_K, tm), colb), _full((N_EXPERTS, 128))],
        out_shape=[jax.ShapeDtypeStruct((n, d), F32), jax.ShapeDtypeStruct((n, d), F32),
                   jax.ShapeDtypeStruct((TOP_K, n), I32), jax.ShapeDtypeStruct((TOP_K, n), F32),
                   jax.ShapeDtypeStruct((TOP_K, n), I32), jax.ShapeDtypeStruct((N_EXPERTS, 128), F32)],
        compiler_params=_cparams(("arbitrary",)),
        name="outproj_router",
    )(act, h, w_out.astype(BF16), g_ffn.reshape(1, d), w_r.T.astype(F32),
      jnp.broadcast_to(b_r.astype(F32)[:, None], (N_EXPERTS, 128)), tri)


def _stage_indices(idx_hbm, idx_smem, isem, step, nsteps):
    slot = step % 2

    def copy(s, sl):
        return pltpu.make_async_copy(idx_hbm.at[s], idx_smem.at[sl], isem.at[sl])

    @pl.when(step == 0)
    def _():
        copy(0, 0).start()

    copy(step, slot).wait()

    @pl.when(step + 1 < nsteps)
    def _():
        copy(step + 1, 1 - slot).start()

    return slot


def _dispatch_kernel(dest_hbm, xn_hbm, xs_in_hbm, xs_hbm, idx_smem, isem, rsem, *, td):
    del xs_in_hbm
    step = pl.program_id(0)
    slot = _stage_indices(dest_hbm, idx_smem, isem, step, pl.num_programs(0))
    base = step * td

    def issue(r, carry):
        src = xn_hbm.at[pl.ds(base + r, 1), :]
        for k in range(TOP_K):
            d = idx_smem[slot, k * td + r]
            pltpu.make_async_copy(src, xs_hbm.at[pl.ds(d, 1), :], rsem).start()
        return carry

    lax.fori_loop(0, td, issue, 0)
    pltpu.make_async_copy(xn_hbm.at[pl.ds(0, TOP_K * td), :], xs_hbm.at[pl.ds(0, TOP_K * td), :], rsem).wait()


def _dispatch(xn, dest, cap, td=256):
    n, d = xn.shape
    dest_t = dest.reshape(TOP_K, n // td, td).transpose(1, 0, 2).reshape(n // td, TOP_K * td)
    any_spec = pl.BlockSpec(memory_space=pl.ANY)
    return pl.pallas_call(
        functools.partial(_dispatch_kernel, td=td),
        grid=(n // td,),
        in_specs=[any_spec, any_spec, any_spec],
        out_specs=any_spec,
        out_shape=jax.ShapeDtypeStruct((cap, d), F32),
        scratch_shapes=[pltpu.SMEM((2, TOP_K * td), I32), pltpu.SemaphoreType.DMA((2,)),
                        pltpu.SemaphoreType.DMA],
        input_output_aliases={2: 0},
        compiler_params=_cparams(("arbitrary",)),
        name="moe_dispatch",
    )(dest_t, xn, jnp.zeros((cap, d), F32))


def _expert_kernel(be_ref, nu_ref, x_ref, wgu_ref, bgu_ref, wdn_ref, bdn_ref, y_ref):
    del be_ref
    d = D_MODEL

    @pl.when(pl.program_id(0) < nu_ref[0])
    def _():
        xb = x_ref[...].astype(BF16)
        glu = jnp.dot(xb, wgu_ref[:, 0:d], preferred_element_type=F32) + bgu_ref[:, 0:d]
        lin = jnp.dot(xb, wgu_ref[:, d:2 * d], preferred_element_type=F32) + bgu_ref[:, d:2 * d]
        glu = jnp.minimum(glu, SWIGLU_LIMIT)
        lin = jnp.clip(lin, -SWIGLU_LIMIT, SWIGLU_LIMIT)
        act = glu * jax.nn.sigmoid(SWIGLU_ALPHA * glu) * (lin + 1.0)
        y_ref[...] = jnp.dot(act.astype(BF16), wdn_ref[...], preferred_element_type=F32) + bdn_ref[...]

    @pl.when(pl.program_id(0) >= nu_ref[0])
    def _():
        y_ref[...] = jnp.zeros_like(y_ref)


def _experts(xs, block_expert, n_used, w_gu, b_gu, w_dn, b_dn):
    cap, d = xs.shape
    nb = cap // MOE_BLK

    def xmap(b, be, nu):
        return (jnp.minimum(b, nu[0] - 1), 0)

    def wmap(b, be, nu):
        return (be[b], 0, 0)

    grid_spec = pltpu.PrefetchScalarGridSpec(
        num_scalar_prefetch=2,
        grid=(nb,),
        in_specs=[pl.BlockSpec((MOE_BLK, d), xmap),
                  pl.BlockSpec((None, d, 2 * d), wmap), pl.BlockSpec((None, 1, 2 * d), wmap),
                  pl.BlockSpec((None, d, d), wmap), pl.BlockSpec((None, 1, d), wmap)],
        out_specs=pl.BlockSpec((MOE_BLK, d), lambda b, be, nu: (b, 0)),
    )
    return pl.pallas_call(
        _expert_kernel,
        grid_spec=grid_spec,
        out_shape=jax.ShapeDtypeStruct((cap, d), F32),
        compiler_params=_cparams(("arbitrary",)),
        name="moe_experts",
    )(block_expert, n_used, xs, w_gu, b_gu.reshape(N_EXPERTS, 1, 2 * d), w_dn, b_dn.reshape(N_EXPERTS, 1, d))


def _combine_ple_kernel(dest_hbm, y_hbm, hm_ref, gate_ref, p_ref, pw_ref, pn_ref, gn_ref, gw_ref,
                        o_ref, idx_smem, ybuf, isem, rsem, *, tc):
    step = pl.program_id(0)
    slot = _stage_indices(dest_hbm, idx_smem, isem, step, pl.num_programs(0))

    def issue(r, carry):
        for k in range(TOP_K):
            s = idx_smem[slot, k * tc + r]
            pltpu.make_async_copy(y_hbm.at[pl.ds(s, 1), :], ybuf.at[k, pl.ds(r, 1), :], rsem).start()
        return carry

    lax.fori_loop(0, tc, issue, 0)
    ple = _rms(jnp.dot(p_ref[...].astype(BF16), pw_ref[...], preferred_element_type=F32), pn_ref[...])
    for k in range(TOP_K):
        pltpu.make_async_copy(y_hbm.at[pl.ds(0, tc), :], ybuf.at[k], rsem).wait()
    h2 = hm_ref[...]
    for k in range(TOP_K):
        h2 = h2 + ybuf[k] * gate_ref[:, k:k + 1]
    gate = jax.nn.sigmoid(jnp.dot(_rms(h2, gn_ref[...]).astype(BF16), gw_ref[...], preferred_element_type=F32))
    o_ref[...] = h2 + ple * gate


def _combine_ple(y, dest, h_mid, gates, p, ple_w, ple_norm, gate_norm, gate_w, tc=256):
    n, d = h_mid.shape
    dest_t = dest.reshape(TOP_K, n // tc, tc).transpose(1, 0, 2).reshape(n // tc, TOP_K * tc)
    any_spec = pl.BlockSpec(memory_space=pl.ANY)
    row = lambda i: (i, 0)
    return pl.pallas_call(
        functools.partial(_combine_ple_kernel, tc=tc),
        grid=(n // tc,),
        in_specs=[any_spec, any_spec, pl.BlockSpec((tc, d), row), pl.BlockSpec((tc, TOP_K), row),
                  pl.BlockSpec((tc, PLE_DIM), row), _full((PLE_DIM, d)), _full((1, d)), _full((1, d)),
                  _full((d, d))],
        out_specs=pl.BlockSpec((tc, d), row),
        out_shape=jax.ShapeDtypeStruct((n, d), F32),
        scratch_shapes=[pltpu.SMEM((2, TOP_K * tc), I32), pltpu.VMEM((TOP_K, tc, d), F32),
                        pltpu.SemaphoreType.DMA((2,)), pltpu.SemaphoreType.DMA],
        compiler_params=_cparams(("arbitrary",)),
        name="moe_combine_ple",
    )(dest_t, y, h_mid, gates.T, p, ple_w.astype(BF16), ple_norm.reshape(1, d),
      gate_norm.reshape(1, d), gate_w.astype(BF16))


def _moe_ple(act, h, w_out, g_ffn, w_r, b_r, w_gu, b_gu, w_dn, b_dn, p, ple_w, ple_norm, gate_norm, gate_w):
    n, d = h.shape
    h_mid, xn, idx, gates, rank, cnt = _outproj_router(act, h, w_out, g_ffn, w_r, b_r)
    counts = cnt[:, 0].astype(I32)
    padded = (counts + MOE_BLK - 1) // MOE_BLK * MOE_BLK
    pend = jnp.cumsum(padded)
    pstart = pend - padded
    hot = idx[:, :, None] == jnp.arange(N_EXPERTS, dtype=I32)[None, None, :]
    dest = jnp.sum(jnp.where(hot, pstart[None, None, :], 0), axis=-1) + rank
    nb = -(-(n * TOP_K) // MOE_BLK) + N_EXPERTS
    cap = nb * MOE_BLK
    block_expert = jnp.minimum(
        jnp.searchsorted(pend, jnp.arange(nb, dtype=I32) * MOE_BLK, side='right'), N_EXPERTS - 1).astype(I32)
    n_used = (pend[-1:] // MOE_BLK).astype(I32)
    xs = _dispatch(xn, dest, cap)
    y = _experts(xs, block_expert, n_used, w_gu.astype(BF16), b_gu.astype(F32), w_dn.astype(BF16),
                 b_dn.astype(F32))
    return _combine_ple(y, dest, h_mid, gates, p, ple_w, ple_norm, gate_norm, gate_w)


def kernel(x, p, norm_mix, norm_ffn, hg_w_in, hg_w_out, hg_gnorm, hg_lb_param, fox_w_in, fox_f_bias, fox_qnorm, fox_knorm, fox_w_out, rg_w_in, rg_conv_w, rg_conv_b, rg_wa, rg_ba, rg_wx, rg_bx, rg_lambda, rg_w_out, router_w, router_b, moe_w_gu, moe_b_gu, moe_w_dn, moe_b_dn, ple_w, ple_norm, ple_gate_norm, ple_gate_w):
    batch, seq, d = x.shape
    depth = p.shape[0]
    n = batch * seq
    lb_all = jnp.cumsum(jax.nn.softmax(hg_lb_param.astype(F32), axis=0), axis=0)
    lb_all = lb_all - lb_all[0]
    h = x.reshape(n, d).astype(F32)
    for i in range(depth):
        j = i // 3
        kind = i % 3
        if kind == 0:
            act = _hgrn2_mixer(h, norm_mix[i], hg_w_in[j], None, hg_gnorm[j], lb_all[i], batch, seq)
            w_out = hg_w_out[j]
        elif kind == 1:
            act = _fox_mixer(h, norm_mix[i], fox_w_in[j], fox_f_bias[j], fox_qnorm[j], fox_knorm[j], batch, seq)
            w_out = fox_w_out[j]
        else:
            act = _rglru_mixer(h, norm_mix[i], rg_w_in[j], rg_conv_w[j], rg_conv_b[j], rg_wa[j], rg_ba[j],
                               rg_wx[j], rg_bx[j], rg_lambda[j], batch, seq)
            w_out = rg_w_out[j]
        h = _moe_ple(act, h, w_out, norm_ffn[i], router_w[i], router_b[i], moe_w_gu[i], moe_b_gu[i],
                     moe_w_dn[i], moe_b_dn[i], p[i].reshape(n, PLE_DIM), ple_w[i], ple_norm[i],
                     ple_gate_norm[i], ple_gate_w[i])
    return h.reshape(batch, seq, d)
```

```python
import functools

import jax
import jax.numpy as jnp
import numpy as np
from jax import lax
from jax.experimental import pallas as pl
from jax.experimental.pallas import tpu as pltpu

F32 = jnp.float32
BF16 = jnp.bfloat16
I32 = jnp.int32

D_MODEL = 1024
EPS = 1e-6
PLE_DIM = 256

HG_HEADS = 8
HG_DK = 128
HG_CHUNK = 128
HG_LEVELS = 7

FOX_HEADS = 16
FOX_HD = 64

RG_BLOCKS = 4
RG_BW = 256
CONV_W = 4
RG_C = 8.0

N_EXPERTS = 32
TOP_K = 4
SWIGLU_LIMIT = 7.0
SWIGLU_ALPHA = 1.702
MOE_BLK = 256

VMEM_LIMIT = 56 * 1024 * 1024

HI = lax.Precision.HIGHEST


def _cparams(sem):
    return pltpu.CompilerParams(dimension_semantics=sem, vmem_limit_bytes=VMEM_LIMIT)


def _bdot(a, b):
    return jnp.dot(a.astype(BF16), b.astype(BF16), preferred_element_type=F32)


def _bdot_nt(a, b):
    return lax.dot_general(a.astype(BF16), b.astype(BF16), (((1,), (1,)), ((), ())),
                           preferred_element_type=F32)


def _bdot_tn(a, b):
    return lax.dot_general(a.astype(BF16), b.astype(BF16), (((0,), (0,)), ((), ())),
                           preferred_element_type=F32)


def _rms(x, g):
    return x * lax.rsqrt(jnp.mean(x * x, axis=-1, keepdims=True) + EPS) * g


def _split2(x):
    hi = x.astype(BF16)
    lo = (x - hi.astype(F32)).astype(BF16)
    return hi, lo


def _split3(x):
    hi = x.astype(BF16)
    r = x - hi.astype(F32)
    mid = r.astype(BF16)
    lo = (r - mid.astype(F32)).astype(BF16)
    return hi, mid, lo


def _log_sigmoid(z):
    return jnp.minimum(z, 0.0) - jnp.log1p(jnp.exp(-jnp.abs(z)))


def _full(shape):
    return pl.BlockSpec(shape, lambda *_: (0,) * len(shape))


def _norm_proj_kernel(h_ref, g_ref, w_ref, o_ref, *, cn):
    xn = _rms(h_ref[...], g_ref[...]).astype(BF16)
    m = w_ref.shape[1]
    for c in range(m // cn):
        o_ref[:, c * cn:(c + 1) * cn] = jnp.dot(
            xn, w_ref[:, c * cn:(c + 1) * cn], preferred_element_type=F32).astype(o_ref.dtype)


def _norm_proj(h, g, w, tm=256, out_dtype=F32):
    n, d = h.shape
    m = w.shape[1]
    return pl.pallas_call(
        functools.partial(_norm_proj_kernel, cn=512),
        grid=(n // tm,),
        in_specs=[pl.BlockSpec((tm, d), lambda i: (i, 0)), _full((1, d)), _full((d, m))],
        out_specs=pl.BlockSpec((tm, m), lambda i: (i, 0)),
        out_shape=jax.ShapeDtypeStruct((n, m), out_dtype),
        compiler_params=_cparams(("parallel",)),
        name="norm_proj",
    )(h, g.reshape(1, d), w)


def _hgrn2_consts():
    c = HG_CHUNK
    t = np.arange(c)
    tril = (t[:, None] >= t[None, :]).astype(np.float32)
    sel = np.zeros((HG_LEVELS, c, c), np.float32)
    for l in range(HG_LEVELS):
        hs = 1 << l
        m = (t // (2 * hs)) * (2 * hs) + hs - 1
        sel[l, t, m] = 1.0
    return jnp.asarray(tril, BF16), jnp.asarray(sel.reshape(HG_LEVELS * c, c), BF16)


def _hgrn2_kernel(q_ref, z_ref, v_ref, g_ref, par_ref, tril_ref, sel_ref, o_ref, st_ref, *, nchunk):
    c = HG_CHUNK

    @pl.when(pl.program_id(2) == 0)
    def _():
        st_ref[...] = jnp.zeros_like(st_ref)

    log_lb = par_ref[0:1, :]
    log1m_lb = par_ref[1:2, :]
    one_m_lb = par_ref[2:3, :]
    gnorm = par_ref[3:4, :]
    row = lax.broadcasted_iota(I32, (c, c), 0)
    col = lax.broadcasted_iota(I32, (c, c), 1)

    def chunk(ci, carry):
        r = pl.ds(pl.multiple_of(ci * c, c), c)
        qr = q_ref[r, :]
        z = z_ref[r, :]
        vb = v_ref[r, :].astype(BF16)
        gt = g_ref[r, :]
        q = qr * jax.nn.sigmoid(qr) * (HG_DK ** -0.5)
        b = log1m_lb + _log_sigmoid(z)
        lf = jnp.maximum(log_lb, b) + jnp.log1p(jnp.exp(-jnp.abs(log_lb - b)))
        k = one_m_lb * jax.nn.sigmoid(-z)
        tril = tril_ref[...]
        hi, mid, lo = _split3(lf)
        cum = (jnp.dot(tril, hi, preferred_element_type=F32)
               + jnp.dot(tril, mid, preferred_element_type=F32)
               + jnp.dot(tril, lo, preferred_element_type=F32))
        chi, clo = _split2(cum)
        refs = (jnp.dot(sel_ref[...], chi, preferred_element_type=F32)
                + jnp.dot(sel_ref[...], clo, preferred_element_type=F32))
        a = jnp.where(row == col, _bdot_nt(q, k), 0.0)
        for l in range(HG_LEVELS):
            hs = 1 << l
            e = jnp.exp(-jnp.abs(cum - refs[l * c:(l + 1) * c, :]))
            right = (row & hs) != 0
            x = jnp.where(right, q, k) * e
            ql = jnp.where(right, x, 0.0)
            kl = jnp.where(right, 0.0, x)
            a = a + jnp.where((row >> (l + 1)) == (col >> (l + 1)), _bdot_nt(ql, kl), 0.0)
        st = st_ref[...]
        o = _bdot(a, vb) + _bdot_nt(q * jnp.exp(cum), st)
        last = cum[c - 1:c, :]
        kd = k * jnp.exp(last - cum)
        st_ref[...] = st * jnp.exp(last) + _bdot_tn(vb, kd)
        y = _rms(o, gnorm) * (gt * jax.nn.sigmoid(gt))
        o_ref[r, :] = y.astype(o_ref.dtype)
        return carry

    lax.fori_loop(0, nchunk, chunk, 0)


def _hgrn2_recurrence(proj, par, batch, seq, tt=512):
    n = batch * seq
    nt = seq // tt
    tril, sel = _hgrn2_consts()

    def part(p):
        return pl.BlockSpec((tt, HG_DK), lambda b, h, i, p=p: (b * nt + i, p * HG_HEADS + h))

    return pl.pallas_call(
        functools.partial(_hgrn2_kernel, nchunk=tt // HG_CHUNK),
        grid=(batch, HG_HEADS, nt),
        in_specs=[part(0), part(1), part(2), part(3),
                  pl.BlockSpec((8, HG_DK), lambda b, h, i: (0, h)),
                  _full(tril.shape), _full(sel.shape)],
        out_specs=pl.BlockSpec((tt, HG_DK), lambda b, h, i: (b * nt + i, h)),
        out_shape=jax.ShapeDtypeStruct((n, D_MODEL), BF16),
        scratch_shapes=[pltpu.VMEM((HG_DK, HG_DK), F32)],
        compiler_params=_cparams(("parallel", "parallel", "arbitrary")),
        name="hgrn2_recurrence",
    )(proj, proj, proj, proj, par, tril, sel)


def _hgrn2_mixer(h, g_mix, w_in, w_out_unused, g_norm, lb, batch, seq):
    del w_out_unused
    proj = _norm_proj(h, g_mix, w_in.astype(BF16))
    par = jnp.zeros((8, D_MODEL), F32)
    par = par.at[0].set(jnp.log(lb)).at[1].set(jnp.log1p(-lb)).at[2].set(1.0 - lb)
    par = par.at[3].set(jnp.tile(g_norm.astype(F32), HG_HEADS))
    return _hgrn2_recurrence(proj, par, batch, seq)


def _fox_proj_kernel(h_ref, g_ref, w_ref, wf_ref, fb_ref, qg_ref, kg_ref, gs_ref, gst_ref, tril_ref,
                     o_ref, cum_ref, carry_ref, *, tiles_per_seq):
    d = D_MODEL

    @pl.when(pl.program_id(0) % tiles_per_seq == 0)
    def _():
        carry_ref[...] = jnp.zeros_like(carry_ref)

    xn = _rms(h_ref[...], g_ref[...])
    xb = xn.astype(BF16)

    def headnorm(t, gain):
        shi, slo = _split2(t * t)
        ssq = (jnp.dot(shi, gs_ref[...], preferred_element_type=F32)
               + jnp.dot(slo, gs_ref[...], preferred_element_type=F32))
        inv = lax.rsqrt(ssq * (1.0 / FOX_HD) + EPS)
        ihi, ilo = _split2(inv)
        invf = (jnp.dot(ihi, gst_ref[...], preferred_element_type=F32)
                + jnp.dot(ilo, gst_ref[...], preferred_element_type=F32))
        return t * invf * gain

    q = jnp.dot(xb, w_ref[:, 0:d], preferred_element_type=F32)
    o_ref[:, 0:d] = (headnorm(q, qg_ref[...]) * (FOX_HD ** -0.5)).astype(o_ref.dtype)
    k = jnp.dot(xb, w_ref[:, d:2 * d], preferred_element_type=F32)
    o_ref[:, d:2 * d] = headnorm(k, kg_ref[...]).astype(o_ref.dtype)
    for c in range(2, 4):
        o_ref[:, c * d:(c + 1) * d] = jnp.dot(
            xb, w_ref[:, c * d:(c + 1) * d], preferred_element_type=F32).astype(o_ref.dtype)

    fl = jnp.dot(xn, wf_ref[...], preferred_element_type=F32, precision=HI) + fb_ref[...]
    hi, mid, lo = _split3(_log_sigmoid(fl))
    tril = tril_ref[...]
    cum = (jnp.dot(tril, hi, preferred_element_type=F32)
           + jnp.dot(tril, mid, preferred_element_type=F32)
           + jnp.dot(tril, lo, preferred_element_type=F32)) + carry_ref[0:1, :]
    cum_ref[...] = cum
    tm = cum.shape[0]
    carry_ref[...] = jnp.broadcast_to(cum[tm - 1:tm, :], carry_ref.shape)


def _fox_proj(h, g_mix, w_in, f_bias, q_norm, k_norm, seq, tm=256):
    n, d = h.shape
    w_main = w_in[:, :4 * d].astype(BF16)
    w_f = jnp.zeros((d, 128), F32).at[:, :FOX_HEADS].set(w_in[:, 4 * d:].astype(F32))
    fb = jnp.zeros((1, 128), F32).at[0, :FOX_HEADS].set(f_bias.astype(F32))
    head_of = np.arange(d) // FOX_HD
    gs_np = (head_of[:, None] == np.arange(128)[None, :]).astype(np.float32)
    gs = jnp.asarray(gs_np, BF16)
    gst = jnp.asarray(gs_np.T, BF16)
    tril = jnp.asarray(np.tril(np.ones((tm, tm), np.float32)), BF16)
    qg = jnp.tile(q_norm.astype(F32), FOX_HEADS).reshape(1, d)
    kg = jnp.tile(k_norm.astype(F32), FOX_HEADS).reshape(1, d)
    return pl.pallas_call(
        functools.partial(_fox_proj_kernel, tiles_per_seq=seq // tm),
        grid=(n // tm,),
        in_specs=[pl.BlockSpec((tm, d), lambda i: (i, 0)), _full((1, d)), _full((d, 4 * d)),
                  _full((d, 128)), _full((1, 128)), _full((1, d)), _full((1, d)),
                  _full((d, 128)), _full((128, d)), _full((tm, tm))],
        out_specs=[pl.BlockSpec((tm, 4 * d), lambda i: (i, 0)),
                   pl.BlockSpec((tm, 128), lambda i: (i, 0))],
        out_shape=[jax.ShapeDtypeStruct((n, 4 * d), BF16), jax.ShapeDtypeStruct((n, 128), F32)],
        scratch_shapes=[pltpu.VMEM((8, 128), F32)],
        compiler_params=_cparams(("arbitrary",)),
        name="fox_proj",
    )(h, g_mix.reshape(1, d), w_main, w_f, fb, qg, kg, gs, gst, tril)


def _fox_attn_kernel(qi_ref, kj_ref, q_ref, k_ref, v_ref, g_ref, cq_ref, ck_ref, o_ref,
                     qh_ref, m_ref, l_ref, acc_ref, *, tq):
    step = pl.program_id(2)
    qi = qi_ref[step]
    kj = kj_ref[step]
    lane = lax.broadcasted_iota(I32, (tq, 128), 1)

    @pl.when(kj == 0)
    def _():
        qv = q_ref[...]
        qh_ref[0] = jnp.where(lane < FOX_HD, qv, jnp.zeros_like(qv))
        qh_ref[1] = jnp.where(lane < FOX_HD, jnp.zeros_like(qv), qv)
        m_ref[...] = jnp.full_like(m_ref, -jnp.inf)
        l_ref[...] = jnp.zeros_like(l_ref)
        acc_ref[...] = jnp.zeros_like(acc_ref)

    def sweep(masked):
        kb = k_ref[...]
        vb = v_ref[...]
        for hh in range(2):
            s = lax.dot_general(qh_ref[hh], kb, (((1,), (1,)), ((), ())), preferred_element_type=F32)
            s = s + (cq_ref[hh:hh + 1, 0:1] - ck_ref[hh:hh + 1, :])
            if masked:
                r = lax.broadcasted_iota(I32, s.shape, 0)
                cc = lax.broadcasted_iota(I32, s.shape, 1)
                s = jnp.where(cc <= r, s, -jnp.inf)
            m_old = m_ref[hh]
            m_new = jnp.maximum(m_old, jnp.max(s, axis=-1, keepdims=True))
            alpha = jnp.exp(m_old - m_new)
            p = jnp.exp(s - m_new[:, 0:1])
            l_ref[hh] = alpha * l_ref[hh] + jnp.sum(p, axis=-1, keepdims=True)
            acc_ref[hh] = alpha * acc_ref[hh] + jnp.dot(p.astype(BF16), vb, preferred_element_type=F32)
            m_ref[hh] = m_new

    @pl.when(kj < qi)
    def _():
        sweep(False)

    @pl.when(kj == qi)
    def _():
        sweep(True)
        o = jnp.where(lane < FOX_HD, acc_ref[0] / l_ref[0], acc_ref[1] / l_ref[1])
        gt = g_ref[...].astype(F32)
        o_ref[...] = (o * jax.nn.sigmoid(gt)).astype(o_ref.dtype)


def _fox_attention(qkvg, cum, batch, seq, tq=512):
    n = batch * seq
    nq = seq // tq
    hp = FOX_HEADS // 2
    pairs = [(i, j) for i in range(nq) for j in range(i + 1)]
    qi = jnp.asarray([p[0] for p in pairs], I32)
    kj = jnp.asarray([p[1] for p in pairs], I32)
    cum_t = cum[:, :FOX_HEADS].reshape(batch, seq, hp, 2).transpose(0, 2, 3, 1).reshape(batch * hp, 2, seq)
    grid_spec = pltpu.PrefetchScalarGridSpec(
        num_scalar_prefetch=2,
        grid=(batch, hp, len(pairs)),
        in_specs=[
            pl.BlockSpec((tq, 128), lambda b, h, s, qi, kj: (b * nq + qi[s], h)),
            pl.BlockSpec((tq, 128), lambda b, h, s, qi, kj: (b * nq + kj[s], hp + h)),
            pl.BlockSpec((tq, 128), lambda b, h, s, qi, kj: (b * nq + kj[s], 2 * hp + h)),
            pl.BlockSpec((tq, 128), lambda b, h, s, qi, kj: (b * nq + qi[s], 3 * hp + h)),
            pl.BlockSpec((None, 2, tq), lambda b, h, s, qi, kj: (b * hp + h, 0, qi[s])),
            pl.BlockSpec((None, 2, tq), lambda b, h, s, qi, kj: (b * hp + h, 0, kj[s])),
        ],
        out_specs=pl.BlockSpec((tq, 128), lambda b, h, s, qi, kj: (b * nq + qi[s], h)),
        scratch_shapes=[pltpu.VMEM((2, tq, 128), BF16), pltpu.VMEM((2, tq, 128), F32),
                        pltpu.VMEM((2, tq, 128), F32), pltpu.VMEM((2, tq, 128), F32)],
    )
    return pl.pallas_call(
        functools.partial(_fox_attn_kernel, tq=tq),
        grid_spec=grid_spec,
        out_shape=jax.ShapeDtypeStruct((n, D_MODEL), BF16),
        compiler_params=_cparams(("parallel", "parallel", "arbitrary")),
        name="fox_attention",
    )(qi, kj, qkvg, qkvg, qkvg, qkvg, cum_t, cum_t)


def _fox_mixer(h, g_mix, w_in, f_bias, q_norm, k_norm, batch, seq):
    qkvg, cum = _fox_proj(h, g_mix, w_in, f_bias, q_norm, k_norm, seq)
    return _fox_attention(qkvg, cum, batch, seq)


def _rglru_kernel(gate_ref, u_ref, par_ref, wa_ref, wx_ref, o_ref, prev_ref, hc_ref, *, tt):
    @pl.when(pl.program_id(1) == 0)
    def _():
        prev_ref[...] = jnp.zeros_like(prev_ref)
        hc_ref[...] = jnp.zeros_like(hc_ref)

    u = u_ref[...]
    ext = jnp.concatenate([prev_ref[...], u], axis=0)
    conv = par_ref[4:5, :] + u * par_ref[3:4, :]
    for shift in range(1, CONV_W):
        conv = conv + pltpu.roll(ext, shift, 0)[8:, :] * par_ref[3 - shift:4 - shift, :]
    prev_ref[...] = u[tt - 8:, :]

    cb = conv.astype(BF16)
    ra, ia = [], []
    for nb in range(RG_BLOCKS):
        blk = cb[:, nb * RG_BW:(nb + 1) * RG_BW]
        ra.append(jnp.dot(blk, wa_ref[nb], preferred_element_type=F32))
        ia.append(jnp.dot(blk, wx_ref[nb], preferred_element_type=F32))
    r = jax.nn.sigmoid(jnp.concatenate(ra, axis=1) + par_ref[5:6, :])
    ig = jax.nn.sigmoid(jnp.concatenate(ia, axis=1) + par_ref[6:7, :])
    lam = par_ref[7:8, :]
    softplus = jnp.maximum(-lam, 0.0) + jnp.log1p(jnp.exp(-jnp.abs(lam)))
    log_a = -RG_C * r * softplus
    a = jnp.exp(log_a)
    b = jnp.sqrt(1.0 - a * a) * (ig * conv)

    row = lax.broadcasted_iota(I32, a.shape, 0)
    dist = 1
    while dist < tt:
        ok = row >= dist
        a_sh = pltpu.roll(a, dist, 0)
        b_sh = pltpu.roll(b, dist, 0)
        b = jnp.where(ok, a * b_sh + b, b)
        a = jnp.where(ok, a * a_sh, a)
        dist *= 2
    hs = b + a * hc_ref[0:1, :]
    hc_ref[...] = jnp.broadcast_to(hs[tt - 1:tt, :], hc_ref.shape)

    gt = gate_ref[...]
    gelu = 0.5 * gt * (1.0 + jnp.tanh(0.7978845608028654 * (gt + 0.044715 * gt * gt * gt)))
    o_ref[...] = (hs * gelu).astype(o_ref.dtype)


def _rglru_mixer(h, g_mix, w_in, conv_w, conv_b, w_a, b_a, w_x, b_x, lam, batch, seq, tt=256):
    n = batch * seq
    nt = seq // tt
    w = D_MODEL
    proj = _norm_proj(h, g_mix, w_in.astype(BF16))
    par = jnp.concatenate([conv_w.astype(F32), conv_b.reshape(1, w), b_a.reshape(1, w),
                           b_x.reshape(1, w), lam.reshape(1, w)], axis=0).astype(F32)
    return pl.pallas_call(
        functools.partial(_rglru_kernel, tt=tt),
        grid=(batch, nt),
        in_specs=[pl.BlockSpec((tt, w), lambda b, i: (b * nt + i, 0)),
                  pl.BlockSpec((tt, w), lambda b, i: (b * nt + i, 1)),
                  _full((8, w)), _full((RG_BLOCKS, RG_BW, RG_BW)), _full((RG_BLOCKS, RG_BW, RG_BW))],
        out_specs=pl.BlockSpec((tt, w), lambda b, i: (b * nt + i, 0)),
        out_shape=jax.ShapeDtypeStruct((n, w), BF16),
        scratch_shapes=[pltpu.VMEM((8, w), F32), pltpu.VMEM((8, w), F32)],
        compiler_params=_cparams(("parallel", "arbitrary")),
        name="rglru",
    )(proj, proj, par, w_a.astype(BF16), w_x.astype(BF16))


def _outproj_router_kernel(a_ref, h_ref, wo_ref, g_ref, wr_ref, br_ref, tri_ref,
                           hm_ref, xn_ref, idx_ref, gate_ref, rank_ref, cnt_ref):
    @pl.when(pl.program_id(0) == 0)
    def _():
        cnt_ref[...] = jnp.zeros_like(cnt_ref)

    hm = h_ref[...] + jnp.dot(a_ref[...], wo_ref[...], preferred_element_type=F32)
    hm_ref[...] = hm
    xn = _rms(hm, g_ref[...])
    xn_ref[...] = xn
    logit = lax.dot_general(wr_ref[...], xn, (((1,), (1,)), ((), ())),
                            preferred_element_type=F32, precision=HI) + br_ref[:, 0:1]
    ne, tm = logit.shape
    eidx = lax.broadcasted_iota(I32, (ne, tm), 0)
    work = logit
    vals, idxs, hots = [], [], []
    for _ in range(TOP_K):
        mx = jnp.max(work, axis=0, keepdims=True)
        pick = jnp.min(jnp.where(work == mx, eidx, ne), axis=0, keepdims=True)
        hot = eidx == pick
        work = jnp.where(hot, -jnp.inf, work)
        vals.append(mx)
        idxs.append(pick)
        hots.append(hot)
    ex = [jnp.exp(v - vals[0]) for v in vals]
    den = ex[0] + ex[1] + ex[2] + ex[3]
    gate_ref[...] = jnp.concatenate([e / den for e in ex], axis=0)
    idx_ref[...] = jnp.concatenate(idxs, axis=0)
    tok_hot = (hots[0] | hots[1] | hots[2] | hots[3]).astype(BF16)
    before = jnp.dot(tok_hot, tri_ref[...], preferred_element_type=F32) + cnt_ref[:, 0:1]
    rank_ref[...] = jnp.concatenate(
        [jnp.sum(jnp.where(h, before, 0.0), axis=0, keepdims=True) for h in hots], axis=0).astype(I32)
    cnt_ref[...] = cnt_ref[...] + jnp.sum(tok_hot.astype(F32), axis=1, keepdims=True)


def _outproj_router(act, h, w_out, g_ffn, w_r, b_r, tm=256):
    n, d = h.shape
    tri = jnp.asarray(np.triu(np.ones((tm, tm), np.float32), 1), BF16)
    row = lambda i: (i, 0)
    colb = lambda i: (0, i)
    return pl.pallas_call(
        _outproj_router_kernel,
        grid=(n // tm,),
        in_specs=[pl.BlockSpec((tm, d), row), pl.BlockSpec((tm, d), row), _full((d, d)), _full((1, d)),
                  _full((N_EXPERTS, d)), _full((N_EXPERTS, 128)), _full((tm, tm))],
        out_specs=[pl.BlockSpec((tm, d), row), pl.BlockSpec((tm, d), row),
                   pl.BlockSpec((TOP_K, tm), colb), pl.BlockSpec((TOP_K, tm), colb),
                   pl.BlockSpec((TOP_K, tm), colb), _full((N_EXPERTS, 128))],
        out_shape=[jax.ShapeDtypeStruct((n, d), F32), jax.ShapeDtypeStruct((n, d), F32),
                   jax.ShapeDtypeStruct((TOP_K, n), I32), jax.ShapeDtypeStruct((TOP_K, n), F32),
                   jax.ShapeDtypeStruct((TOP_K, n), I32), jax.ShapeDtypeStruct((N_EXPERTS, 128), F32)],
        compiler_params=_cparams(("arbitrary",)),
        name="outproj_router",
    )(act, h, w_out.astype(BF16), g_ffn.reshape(1, d), w_r.T.astype(F32),
      jnp.broadcast_to(b_r.astype(F32)[:, None], (N_EXPERTS, 128)), tri)


def _stage_indices(idx_hbm, idx_smem, isem, step, nsteps):
    slot = step % 2

    def copy(s, sl):
        return pltpu.make_async_copy(idx_hbm.at[s], idx_smem.at[sl], isem.at[sl])

    @pl.when(step == 0)
    def _():
        copy(0, 0).start()

    copy(step, slot).wait()

    @pl.when(step + 1 < nsteps)
    def _():
        copy(step + 1, 1 - slot).start()

    return slot


def _dispatch_kernel(cnt_ref, pst_ref, dest_hbm, x_ref, xs_hbm, idx_smem, zrow, zblk, isem, rsem, zsem, *, td):
    step = pl.program_id(0)
    slot = _stage_indices(dest_hbm, idx_smem, isem, step, pl.num_programs(0))

    def issue(r, carry):
        src = x_ref.at[pl.ds(r, 1), :]
        for k in range(TOP_K):
            d = idx_smem[slot, k * td + r]
            pltpu.make_async_copy(src, xs_hbm.at[pl.ds(d, 1), :], rsem).start()
        return carry

    lax.fori_loop(0, td, issue, 0)

    @pl.when(step == 0)
    def _():
        zrow[...] = jnp.zeros_like(zrow)

        def per_expert(e, carry):
            cnt = cnt_ref[e]
            first = pst_ref[e] + cnt
            npad = (((cnt + (MOE_BLK - 1)) // MOE_BLK) * MOE_BLK) - cnt

            def zcopy(r):
                return pltpu.make_async_copy(zrow.at[pl.ds(0, 1), :], xs_hbm.at[pl.ds(first + r, 1), :], zsem)

            def zstart(r, c):
                zcopy(r).start()
                return c

            def zwait(r, c):
                zcopy(r).wait()
                return c

            lax.fori_loop(0, npad, zstart, 0)
            lax.fori_loop(0, npad, zwait, 0)
            return carry

        lax.fori_loop(0, N_EXPERTS, per_expert, 0)

        zblk[...] = jnp.zeros_like(zblk)
        last = N_EXPERTS - 1
        used = (pst_ref[last] + cnt_ref[last] + (MOE_BLK - 1)) // MOE_BLK

        def ztail(b, c):
            start = pl.multiple_of(b * MOE_BLK, MOE_BLK)
            cp = pltpu.make_async_copy(zblk, xs_hbm.at[pl.ds(start, MOE_BLK), :], zsem)
            cp.start()
            cp.wait()
            return c

        lax.fori_loop(used, xs_hbm.shape[0] // MOE_BLK, ztail, 0)

    pltpu.make_async_copy(xs_hbm.at[pl.ds(0, TOP_K * td), :], xs_hbm.at[pl.ds(0, TOP_K * td), :], rsem).wait()


def _dispatch(xn, dest, counts, pstart, cap, td=256):
    n, d = xn.shape
    dest_t = dest.reshape(TOP_K, n // td, td).transpose(1, 0, 2).reshape(n // td, TOP_K * td)
    any_spec = pl.BlockSpec(memory_space=pl.ANY)
    grid_spec = pltpu.PrefetchScalarGridSpec(
        num_scalar_prefetch=2,
        grid=(n // td,),
        in_specs=[any_spec, pl.BlockSpec((td, d), lambda i, c, s: (i, 0))],
        out_specs=any_spec,
        scratch_shapes=[pltpu.SMEM((2, TOP_K * td), I32), pltpu.VMEM((8, d), F32), pltpu.VMEM((MOE_BLK, d), F32),
                        pltpu.SemaphoreType.DMA((2,)), pltpu.SemaphoreType.DMA, pltpu.SemaphoreType.DMA],
    )
    return pl.pallas_call(
        functools.partial(_dispatch_kernel, td=td),
        grid_spec=grid_spec,
        out_shape=jax.ShapeDtypeStruct((cap, d), F32),
        compiler_params=_cparams(("arbitrary",)),
        name="moe_dispatch",
    )(counts, pstart, dest_t, xn)


def _expert_kernel(be_ref, nu_ref, x_ref, wgu_ref, bgu_ref, wdn_ref, bdn_ref, y_ref):
    del be_ref
    d = D_MODEL

    @pl.when(pl.program_id(0) < nu_ref[0])
    def _():
        xb = x_ref[...].astype(BF16)
        glu = jnp.dot(xb, wgu_ref[:, 0:d], preferred_element_type=F32) + bgu_ref[:, 0:d]
        lin = jnp.dot(xb, wgu_ref[:, d:2 * d], preferred_element_type=F32) + bgu_ref[:, d:2 * d]
        glu = jnp.minimum(glu, SWIGLU_LIMIT)
        lin = jnp.clip(lin, -SWIGLU_LIMIT, SWIGLU_LIMIT)
        act = glu * jax.nn.sigmoid(SWIGLU_ALPHA * glu) * (lin + 1.0)
        y_ref[...] = jnp.dot(act.astype(BF16), wdn_ref[...], preferred_element_type=F32) + bdn_ref[...]

    @pl.when(pl.program_id(0) >= nu_ref[0])
    def _():
        y_ref[...] = jnp.zeros_like(y_ref)


def _experts(xs, block_expert, n_used, w_gu, b_gu, w_dn, b_dn):
    cap, d = xs.shape
    nb = cap // MOE_BLK

    def xmap(b, be, nu):
        return (jnp.minimum(b, nu[0] - 1), 0)

    def wmap(b, be, nu):
        return (be[b], 0, 0)

    grid_spec = pltpu.PrefetchScalarGridSpec(
        num_scalar_prefetch=2,
        grid=(nb,),
        in_specs=[pl.BlockSpec((MOE_BLK, d), xmap),
                  pl.BlockSpec((None, d, 2 * d), wmap), pl.BlockSpec((None, 1, 2 * d), wmap),
                  pl.BlockSpec((None, d, d), wmap), pl.BlockSpec((None, 1, d), wmap)],
        out_specs=pl.BlockSpec((MOE_BLK, d), lambda b, be, nu: (b, 0)),
    )
    return pl.pallas_call(
        _expert_kernel,
        grid_spec=grid_spec,
        out_shape=jax.ShapeDtypeStruct((cap, d), F32),
        compiler_params=_cparams(("arbitrary",)),
        name="moe_experts",
    )(block_expert, n_used, xs, w_gu, b_gu.reshape(N_EXPERTS, 1, 2 * d), w_dn, b_dn.reshape(N_EXPERTS, 1, d))


def _combine_ple_kernel(dest_hbm, y_hbm, hm_ref, gate_ref, p_ref, pw_ref, pn_ref, gn_ref, gw_ref,
                        o_ref, idx_smem, ybuf, isem, rsem, *, tc):
    step = pl.program_id(0)
    slot = _stage_indices(dest_hbm, idx_smem, isem, step, pl.num_programs(0))

    def issue(r, carry):
        for k in range(TOP_K):
            s = idx_smem[slot, k * tc + r]
            pltpu.make_async_copy(y_hbm.at[pl.ds(s, 1), :], ybuf.at[k, pl.ds(r, 1), :], rsem).start()
        return carry

    lax.fori_loop(0, tc, issue, 0)
    ple = _rms(jnp.dot(p_ref[...].astype(BF16), pw_ref[...], preferred_element_type=F32), pn_ref[...])
    for k in range(TOP_K):
        pltpu.make_async_copy(y_hbm.at[pl.ds(0, tc), :], ybuf.at[k], rsem).wait()
    h2 = hm_ref[...]
    for k in range(TOP_K):
        h2 = h2 + ybuf[k] * gate_ref[:, k:k + 1]
    gate = jax.nn.sigmoid(jnp.dot(_rms(h2, gn_ref[...]).astype(BF16), gw_ref[...], preferred_element_type=F32))
    o_ref[...] = h2 + ple * gate


def _combine_ple(y, dest, h_mid, gates, p, ple_w, ple_norm, gate_norm, gate_w, tc=256):
    n, d = h_mid.shape
    dest_t = dest.reshape(TOP_K, n // tc, tc).transpose(1, 0, 2).reshape(n // tc, TOP_K * tc)
    any_spec = pl.BlockSpec(memory_space=pl.ANY)
    row = lambda i: (i, 0)
    return pl.pallas_call(
        functools.partial(_combine_ple_kernel, tc=tc),
        grid=(n // tc,),
        in_specs=[any_spec, any_spec, pl.BlockSpec((tc, d), row), pl.BlockSpec((tc, TOP_K), row),
                  pl.BlockSpec((tc, PLE_DIM), row), _full((PLE_DIM, d)), _full((1, d)), _full((1, d)),
                  _full((d, d))],
        out_specs=pl.BlockSpec((tc, d), row),
        out_shape=jax.ShapeDtypeStruct((n, d), F32),
        scratch_shapes=[pltpu.SMEM((2, TOP_K * tc), I32), pltpu.VMEM((TOP_K, tc, d), F32),
                        pltpu.SemaphoreType.DMA((2,)), pltpu.SemaphoreType.DMA],
        compiler_params=_cparams(("arbitrary",)),
        name="moe_combine_ple",
    )(dest_t, y, h_mid, gates.T, p, ple_w.astype(BF16), ple_norm.reshape(1, d),
      gate_norm.reshape(1, d), gate_w.astype(BF16))


def _moe_ple(act, h, w_out, g_ffn, w_r, b_r, w_gu, b_gu, w_dn, b_dn, p, ple_w, ple_norm, gate_norm, gate_w):
    n, d = h.shape
    h_mid, xn, idx, gates, rank, cnt = _outproj_router(act, h, w_out, g_ffn, w_r, b_r)
    counts = cnt[:, 0].astype(I32)
    padded = (counts + MOE_BLK - 1) // MOE_BLK * MOE_BLK
    pend = jnp.cumsum(padded)
    pstart = pend - padded
    hot = idx[:, :, None] == jnp.arange(N_EXPERTS, dtype=I32)[None, None, :]
    dest = jnp.sum(jnp.where(hot, pstart[None, None, :], 0), axis=-1) + rank
    nb = -(-(n * TOP_K) // MOE_BLK) + N_EXPERTS
    cap = nb * MOE_BLK
    block_start = jnp.arange(nb, dtype=I32) * MOE_BLK
    block_expert = jnp.minimum(jnp.sum(pend[None, :] <= block_start[:, None], axis=1), N_EXPERTS - 1).astype(I32)
    n_used = (pend[-1:] // MOE_BLK).astype(I32)
    xs = _dispatch(xn, dest, counts, pstart.astype(I32), cap)
    y = _experts(xs, block_expert, n_used, w_gu.astype(BF16), b_gu.astype(F32), w_dn.astype(BF16),
                 b_dn.astype(F32))
    return _combine_ple(y, dest, h_mid, gates, p, ple_w, ple_norm, gate_norm, gate_w)


def kernel(x, p, norm_mix, norm_ffn, hg_w_in, hg_w_out, hg_gnorm, hg_lb_param, fox_w_in, fox_f_bias, fox_qnorm, fox_knorm, fox_w_out, rg_w_in, rg_conv_w, rg_conv_b, rg_wa, rg_ba, rg_wx, rg_bx, rg_lambda, rg_w_out, router_w, router_b, moe_w_gu, moe_b_gu, moe_w_dn, moe_b_dn, ple_w, ple_norm, ple_gate_norm, ple_gate_w):
    batch, seq, d = x.shape
    depth = p.shape[0]
    n = batch * seq
    lb_all = jnp.cumsum(jax.nn.softmax(hg_lb_param.astype(F32), axis=0), axis=0)
    lb_all = lb_all - lb_all[0]
    h = x.reshape(n, d).astype(F32)
    for i in range(depth):
        j = i // 3
        kind = i % 3
        if kind == 0:
            act = _hgrn2_mixer(h, norm_mix[i], hg_w_in[j], None, hg_gnorm[j], lb_all[i], batch, seq)
            w_out = hg_w_out[j]
        elif kind == 1:
            act = _fox_mixer(h, norm_mix[i], fox_w_in[j], fox_f_bias[j], fox_qnorm[j], fox_knorm[j], batch, seq)
            w_out = fox_w_out[j]
        else:
            act = _rglru_mixer(h, norm_mix[i], rg_w_in[j], rg_conv_w[j], rg_conv_b[j], rg_wa[j], rg_ba[j],
                               rg_wx[j], rg_bx[j], rg_lambda[j], batch, seq)
            w_out = rg_w_out[j]
        h = _moe_ple(act, h, w_out, norm_ffn[i], router_w[i], router_b[i], moe_w_gu[i], moe_b_gu[i],
                     moe_w_dn[i], moe_b_dn[i], p[i].reshape(n, PLE_DIM), ple_w[i], ple_norm[i],
                     ple_gate_norm[i], ple_gate_w[i])
    return h.reshape(batch, seq, d)
```

```python
import functools

import jax
import jax.numpy as jnp
import numpy as np
from jax import lax
from jax.experimental import pallas as pl
from jax.experimental.pallas import tpu as pltpu

F32 = jnp.float32
BF16 = jnp.bfloat16
I32 = jnp.int32

D_MODEL = 1024
EPS = 1e-6
PLE_DIM = 256

HG_HEADS = 8
HG_DK = 128
HG_CHUNK = 128
HG_LEVELS = 7

FOX_HEADS = 16
FOX_HD = 64
FOX_ZERO_EXP = -105.0

RG_BLOCKS = 4
RG_BW = 256
CONV_W = 4
RG_C = 8.0

N_EXPERTS = 32
TOP_K = 4
SWIGLU_LIMIT = 7.0
SWIGLU_ALPHA = 1.702
MOE_BLK = 256

VMEM_LIMIT = 56 * 1024 * 1024

HI = lax.Precision.HIGHEST


def _cparams(sem):
    return pltpu.CompilerParams(dimension_semantics=sem, vmem_limit_bytes=VMEM_LIMIT)


def _bdot(a, b):
    return jnp.dot(a.astype(BF16), b.astype(BF16), preferred_element_type=F32)


def _bdot_nt(a, b):
    return lax.dot_general(a.astype(BF16), b.astype(BF16), (((1,), (1,)), ((), ())),
                           preferred_element_type=F32)


def _bdot_tn(a, b):
    return lax.dot_general(a.astype(BF16), b.astype(BF16), (((0,), (0,)), ((), ())),
                           preferred_element_type=F32)


def _rms(x, g):
    return x * lax.rsqrt(jnp.mean(x * x, axis=-1, keepdims=True) + EPS) * g


def _split2(x):
    hi = x.astype(BF16)
    lo = (x - hi.astype(F32)).astype(BF16)
    return hi, lo


def _split3(x):
    hi = x.astype(BF16)
    r = x - hi.astype(F32)
    mid = r.astype(BF16)
    lo = (r - mid.astype(F32)).astype(BF16)
    return hi, mid, lo


def _log_sigmoid(z):
    return jnp.minimum(z, 0.0) - jnp.log1p(jnp.exp(-jnp.abs(z)))


def _full(shape):
    return pl.BlockSpec(shape, lambda *_: (0,) * len(shape))


def _norm_proj_kernel(h_ref, g_ref, w_ref, o_ref, *, cn):
    xn = _rms(h_ref[...], g_ref[...]).astype(BF16)
    m = w_ref.shape[1]
    for c in range(m // cn):
        o_ref[:, c * cn:(c + 1) * cn] = jnp.dot(
            xn, w_ref[:, c * cn:(c + 1) * cn], preferred_element_type=F32).astype(o_ref.dtype)


def _norm_proj(h, g, w, tm=256, out_dtype=F32):
    n, d = h.shape
    m = w.shape[1]
    return pl.pallas_call(
        functools.partial(_norm_proj_kernel, cn=512),
        grid=(n // tm,),
        in_specs=[pl.BlockSpec((tm, d), lambda i: (i, 0)), _full((1, d)), _full((d, m))],
        out_specs=pl.BlockSpec((tm, m), lambda i: (i, 0)),
        out_shape=jax.ShapeDtypeStruct((n, m), out_dtype),
        compiler_params=_cparams(("parallel",)),
        name="norm_proj",
    )(h, g.reshape(1, d), w)


def _hgrn2_consts():
    c = HG_CHUNK
    t = np.arange(c)
    tril = (t[:, None] >= t[None, :]).astype(np.float32)
    sel = np.zeros((HG_LEVELS, c, c), np.float32)
    for l in range(HG_LEVELS):
        hs = 1 << l
        m = (t // (2 * hs)) * (2 * hs) + hs - 1
        sel[l, t, m] = 1.0
    return jnp.asarray(tril, BF16), jnp.asarray(sel.reshape(HG_LEVELS * c, c), BF16)


def _hgrn2_kernel(q_ref, z_ref, v_ref, g_ref, par_ref, tril_ref, sel_ref, o_ref, st_ref, *, nchunk):
    c = HG_CHUNK

    @pl.when(pl.program_id(2) == 0)
    def _():
        st_ref[...] = jnp.zeros_like(st_ref)

    log_lb = par_ref[0:1, :]
    log1m_lb = par_ref[1:2, :]
    one_m_lb = par_ref[2:3, :]
    gnorm = par_ref[3:4, :]
    row = lax.broadcasted_iota(I32, (c, c), 0)
    col = lax.broadcasted_iota(I32, (c, c), 1)

    def chunk(ci, carry):
        r = pl.ds(pl.multiple_of(ci * c, c), c)
        qr = q_ref[r, :]
        z = z_ref[r, :]
        vb = v_ref[r, :].astype(BF16)
        gt = g_ref[r, :]
        q = qr * jax.nn.sigmoid(qr) * (HG_DK ** -0.5)
        b = log1m_lb + _log_sigmoid(z)
        lf = jnp.maximum(log_lb, b) + jnp.log1p(jnp.exp(-jnp.abs(log_lb - b)))
        k = one_m_lb * jax.nn.sigmoid(-z)
        tril = tril_ref[...]
        hi, mid, lo = _split3(lf)
        cum = (jnp.dot(tril, hi, preferred_element_type=F32)
               + jnp.dot(tril, mid, preferred_element_type=F32)
               + jnp.dot(tril, lo, preferred_element_type=F32))
        chi, clo = _split2(cum)
        refs = (jnp.dot(sel_ref[...], chi, preferred_element_type=F32)
                + jnp.dot(sel_ref[...], clo, preferred_element_type=F32))
        a = jnp.where(row == col, _bdot_nt(q, k), 0.0)
        for l in range(HG_LEVELS):
            hs = 1 << l
            ref = refs[l * c:(l + 1) * c, :]
            if hs >= 8:
                parts = []
                for blk in range(c // hs):
                    sl = slice(blk * hs, (blk + 1) * hs)
                    if blk % 2:
                        parts.append(q[sl] * jnp.exp(jnp.minimum(cum[sl] - ref[sl], 0.0)))
                    else:
                        parts.append(k[sl] * jnp.exp(jnp.minimum(ref[sl] - cum[sl], 0.0)))
                x = jnp.concatenate(parts, axis=0).astype(BF16)
            else:
                x = (jnp.where((row & hs) != 0, q, k) * jnp.exp(-jnp.abs(cum - ref))).astype(BF16)
            a = jnp.where((((row ^ col) >> l) == 1) & (row > col), _bdot_nt(x, x), a)
        st = st_ref[...]
        o = _bdot(a, vb) + _bdot_nt(q * jnp.exp(cum), st)
        last = cum[c - 1:c, :]
        kd = k * jnp.exp(last - cum)
        st_ref[...] = st * jnp.exp(last) + _bdot_tn(vb, kd)
        y = _rms(o, gnorm) * (gt * jax.nn.sigmoid(gt))
        o_ref[r, :] = y.astype(o_ref.dtype)
        return carry

    lax.fori_loop(0, nchunk, chunk, 0, unroll=True)


def _hgrn2_recurrence(proj, par, batch, seq, tt=512):
    n = batch * seq
    nt = seq // tt
    tril, sel = _hgrn2_consts()

    def part(p):
        return pl.BlockSpec((tt, HG_DK), lambda b, h, i, p=p: (b * nt + i, p * HG_HEADS + h))

    return pl.pallas_call(
        functools.partial(_hgrn2_kernel, nchunk=tt // HG_CHUNK),
        grid=(batch, HG_HEADS, nt),
        in_specs=[part(0), part(1), part(2), part(3),
                  pl.BlockSpec((8, HG_DK), lambda b, h, i: (0, h)),
                  _full(tril.shape), _full(sel.shape)],
        out_specs=pl.BlockSpec((tt, HG_DK), lambda b, h, i: (b * nt + i, h)),
        out_shape=jax.ShapeDtypeStruct((n, D_MODEL), BF16),
        scratch_shapes=[pltpu.VMEM((HG_DK, HG_DK), F32)],
        compiler_params=_cparams(("parallel", "parallel", "arbitrary")),
        name="hgrn2_recurrence",
    )(proj, proj, proj, proj, par, tril, sel)


def _hgrn2_mixer(h, g_mix, w_in, w_out_unused, g_norm, lb, batch, seq):
    del w_out_unused
    proj = _norm_proj(h, g_mix, w_in.astype(BF16))
    par = jnp.zeros((8, D_MODEL), F32)
    par = par.at[0].set(jnp.log(lb)).at[1].set(jnp.log1p(-lb)).at[2].set(1.0 - lb)
    par = par.at[3].set(jnp.tile(g_norm.astype(F32), HG_HEADS))
    return _hgrn2_recurrence(proj, par, batch, seq)


def _fox_proj_kernel(h_ref, g_ref, w_ref, wf_ref, fb_ref, qg_ref, kg_ref, gs_ref, gst_ref, tril_ref,
                     o_ref, cum_ref, carry_ref, *, tiles_per_seq):
    d = D_MODEL

    @pl.when(pl.program_id(0) % tiles_per_seq == 0)
    def _():
        carry_ref[...] = jnp.zeros_like(carry_ref)

    xn = _rms(h_ref[...], g_ref[...])
    xb = xn.astype(BF16)

    def headnorm(t, gain):
        shi, slo = _split2(t * t)
        ssq = (jnp.dot(shi, gs_ref[...], preferred_element_type=F32)
               + jnp.dot(slo, gs_ref[...], preferred_element_type=F32))
        inv = lax.rsqrt(ssq * (1.0 / FOX_HD) + EPS)
        ihi, ilo = _split2(inv)
        invf = (jnp.dot(ihi, gst_ref[...], preferred_element_type=F32)
                + jnp.dot(ilo, gst_ref[...], preferred_element_type=F32))
        return t * invf * gain

    q = jnp.dot(xb, w_ref[:, 0:d], preferred_element_type=F32)
    o_ref[:, 0:d] = (headnorm(q, qg_ref[...]) * (FOX_HD ** -0.5)).astype(o_ref.dtype)
    k = jnp.dot(xb, w_ref[:, d:2 * d], preferred_element_type=F32)
    o_ref[:, d:2 * d] = headnorm(k, kg_ref[...]).astype(o_ref.dtype)
    for c in range(2, 4):
        o_ref[:, c * d:(c + 1) * d] = jnp.dot(
            xb, w_ref[:, c * d:(c + 1) * d], preferred_element_type=F32).astype(o_ref.dtype)

    fl = jnp.dot(xn, wf_ref[...], preferred_element_type=F32, precision=HI) + fb_ref[...]
    hi, mid, lo = _split3(_log_sigmoid(fl))
    tril = tril_ref[...]
    cum = (jnp.dot(tril, hi, preferred_element_type=F32)
           + jnp.dot(tril, mid, preferred_element_type=F32)
           + jnp.dot(tril, lo, preferred_element_type=F32)) + carry_ref[0:1, :]
    cum_ref[...] = cum
    tm = cum.shape[0]
    carry_ref[...] = jnp.broadcast_to(cum[tm - 1:tm, :], carry_ref.shape)


def _fox_proj(h, g_mix, w_in, f_bias, q_norm, k_norm, seq, tm=256):
    n, d = h.shape
    w_main = w_in[:, :4 * d].astype(BF16)
    w_f = jnp.zeros((d, 128), F32).at[:, :FOX_HEADS].set(w_in[:, 4 * d:].astype(F32))
    fb = jnp.zeros((1, 128), F32).at[0, :FOX_HEADS].set(f_bias.astype(F32))
    head_of = np.arange(d) // FOX_HD
    gs_np = (head_of[:, None] == np.arange(128)[None, :]).astype(np.float32)
    gs = jnp.asarray(gs_np, BF16)
    gst = jnp.asarray(gs_np.T, BF16)
    tril = jnp.asarray(np.tril(np.ones((tm, tm), np.float32)), BF16)
    qg = jnp.tile(q_norm.astype(F32), FOX_HEADS).reshape(1, d)
    kg = jnp.tile(k_norm.astype(F32), FOX_HEADS).reshape(1, d)
    return pl.pallas_call(
        functools.partial(_fox_proj_kernel, tiles_per_seq=seq // tm),
        grid=(n // tm,),
        in_specs=[pl.BlockSpec((tm, d), lambda i: (i, 0)), _full((1, d)), _full((d, 4 * d)),
                  _full((d, 128)), _full((1, 128)), _full((1, d)), _full((1, d)),
                  _full((d, 128)), _full((128, d)), _full((tm, tm))],
        out_specs=[pl.BlockSpec((tm, 4 * d), lambda i: (i, 0)),
                   pl.BlockSpec((tm, 128), lambda i: (i, 0))],
        out_shape=[jax.ShapeDtypeStruct((n, 4 * d), BF16), jax.ShapeDtypeStruct((n, 128), F32)],
        scratch_shapes=[pltpu.VMEM((8, 128), F32)],
        compiler_params=_cparams(("arbitrary",)),
        name="fox_proj",
    )(h, g_mix.reshape(1, d), w_main, w_f, fb, qg, kg, gs, gst, tril)


def _fox_attn_kernel(jlo_ref, q_ref, k_ref, v_ref, g_ref, cq_ref, ck_ref, o_ref,
                     qh_ref, m_ref, l_ref, acc_ref, *, tq, nq):
    qi = pl.program_id(2)
    lane = lax.broadcasted_iota(I32, (tq, 128), 1)
    qv = q_ref[...]
    qh_ref[0] = jnp.where(lane < FOX_HD, qv, jnp.zeros_like(qv))
    qh_ref[1] = jnp.where(lane < FOX_HD, jnp.zeros_like(qv), qv)
    m_ref[...] = jnp.full_like(m_ref, -jnp.inf)
    l_ref[...] = jnp.zeros_like(l_ref)
    acc_ref[...] = jnp.zeros_like(acc_ref)

    def sweep(kj, masked):
        rows = pl.ds(pl.multiple_of(kj * tq, tq), tq)
        kb = k_ref[rows, :]
        vb = v_ref[rows, :]
        for hh in range(2):
            s = lax.dot_general(qh_ref[hh], kb, (((1,), (1,)), ((), ())), preferred_element_type=F32)
            s = s + (cq_ref[hh:hh + 1, 0:1] - ck_ref[hh:hh + 1, rows])
            if masked:
                r = lax.broadcasted_iota(I32, s.shape, 0)
                cc = lax.broadcasted_iota(I32, s.shape, 1)
                s = jnp.where(cc <= r, s, -jnp.inf)
            m_old = m_ref[hh]
            m_new = jnp.maximum(m_old, jnp.max(s, axis=-1, keepdims=True))
            alpha = jnp.exp(m_old - m_new)
            p = jnp.exp(s - m_new[:, 0:1])
            l_ref[hh] = alpha * l_ref[hh] + jnp.sum(p, axis=-1, keepdims=True)
            acc_ref[hh] = alpha * acc_ref[hh] + jnp.dot(p.astype(BF16), vb, preferred_element_type=F32)
            m_ref[hh] = m_new

    def body(kj, carry):
        sweep(kj, False)
        return carry

    lo = jlo_ref[(pl.program_id(0) * pl.num_programs(1) + pl.program_id(1)) * nq + qi]
    lax.fori_loop(lo, qi, body, 0)
    sweep(qi, True)
    o = jnp.where(lane < FOX_HD, acc_ref[0] / l_ref[0], acc_ref[1] / l_ref[1])
    gt = g_ref[...].astype(F32)
    o_ref[...] = (o * jax.nn.sigmoid(gt)).astype(o_ref.dtype)


def _fox_first_block(cum_t, logit_bound, tq):
    bh, _, seq = cum_t.shape
    nq = seq // tq
    blk = cum_t.reshape(bh, 2, nq, tq)
    gap = blk[:, :, :, None, 0] - blk[:, :, None, :, tq - 1]
    dead = jnp.all(2.0 * logit_bound + gap < FOX_ZERO_EXP, axis=1)
    dead = dead & (jnp.arange(nq)[None, None, :] < jnp.arange(nq)[None, :, None])
    return jnp.sum(jnp.cumprod(dead.astype(I32), axis=-1), axis=-1).astype(I32).reshape(-1)


def _fox_attention(qkvg, cum, logit_bound, batch, seq, tq=512):
    n = batch * seq
    nq = seq // tq
    hp = FOX_HEADS // 2
    cum_t = cum[:, :FOX_HEADS].reshape(batch, seq, hp, 2).transpose(0, 2, 3, 1).reshape(batch * hp, 2, seq)
    jlo = _fox_first_block(cum_t, logit_bound, tq)
    grid_spec = pltpu.PrefetchScalarGridSpec(
        num_scalar_prefetch=1,
        grid=(batch, hp, nq),
        in_specs=[
            pl.BlockSpec((tq, 128), lambda b, h, i, jlo: (b * nq + i, h)),
            pl.BlockSpec((seq, 128), lambda b, h, i, jlo: (b, hp + h)),
            pl.BlockSpec((seq, 128), lambda b, h, i, jlo: (b, 2 * hp + h)),
            pl.BlockSpec((tq, 128), lambda b, h, i, jlo: (b * nq + i, 3 * hp + h)),
            pl.BlockSpec((None, 2, tq), lambda b, h, i, jlo: (b * hp + h, 0, i)),
            pl.BlockSpec((None, 2, seq), lambda b, h, i, jlo: (b * hp + h, 0, 0)),
        ],
        out_specs=pl.BlockSpec((tq, 128), lambda b, h, i, jlo: (b * nq + i, h)),
        scratch_shapes=[pltpu.VMEM((2, tq, 128), BF16), pltpu.VMEM((2, tq, 128), F32),
                        pltpu.VMEM((2, tq, 128), F32), pltpu.VMEM((2, tq, 128), F32)],
    )
    return pl.pallas_call(
        functools.partial(_fox_attn_kernel, tq=tq, nq=nq),
        grid_spec=grid_spec,
        out_shape=jax.ShapeDtypeStruct((n, D_MODEL), BF16),
        compiler_params=_cparams(("parallel", "parallel", "arbitrary")),
        name="fox_attention",
    )(jlo, qkvg, qkvg, qkvg, qkvg, cum_t, cum_t)


def _fox_mixer(h, g_mix, w_in, f_bias, q_norm, k_norm, batch, seq):
    qkvg, cum = _fox_proj(h, g_mix, w_in, f_bias, q_norm, k_norm, seq)
    logit_bound = 1.02 * FOX_HD ** 0.5 * jnp.max(jnp.abs(q_norm.astype(F32))) * jnp.max(jnp.abs(k_norm.astype(F32)))
    return _fox_attention(qkvg, cum, logit_bound, batch, seq)


def _rglru_kernel(gate_ref, u_ref, par_ref, wa_ref, wx_ref, o_ref, prev_ref, hc_ref, *, tt):
    @pl.when(pl.program_id(1) == 0)
    def _():
        prev_ref[...] = jnp.zeros_like(prev_ref)
        hc_ref[...] = jnp.zeros_like(hc_ref)

    u = u_ref[...]
    ext = jnp.concatenate([prev_ref[...], u], axis=0)
    conv = par_ref[4:5, :] + u * par_ref[3:4, :]
    for shift in range(1, CONV_W):
        conv = conv + pltpu.roll(ext, shift, 0)[8:, :] * par_ref[3 - shift:4 - shift, :]
    prev_ref[...] = u[tt - 8:, :]

    cb = conv.astype(BF16)
    ra, ia = [], []
    for nb in range(RG_BLOCKS):
        blk = cb[:, nb * RG_BW:(nb + 1) * RG_BW]
        ra.append(jnp.dot(blk, wa_ref[nb], preferred_element_type=F32))
        ia.append(jnp.dot(blk, wx_ref[nb], preferred_element_type=F32))
    r = jax.nn.sigmoid(jnp.concatenate(ra, axis=1) + par_ref[5:6, :])
    ig = jax.nn.sigmoid(jnp.concatenate(ia, axis=1) + par_ref[6:7, :])
    lam = par_ref[7:8, :]
    softplus = jnp.maximum(-lam, 0.0) + jnp.log1p(jnp.exp(-jnp.abs(lam)))
    log_a = -RG_C * r * softplus
    a = jnp.exp(log_a)
    b = jnp.sqrt(1.0 - a * a) * (ig * conv)

    row = lax.broadcasted_iota(I32, a.shape, 0)
    dist = 1
    while dist < tt:
        ok = row >= dist
        a_sh = pltpu.roll(a, dist, 0)
        b_sh = pltpu.roll(b, dist, 0)
        b = jnp.where(ok, a * b_sh + b, b)
        a = jnp.where(ok, a * a_sh, a)
        dist *= 2
    hs = b + a * hc_ref[0:1, :]
    hc_ref[...] = jnp.broadcast_to(hs[tt - 1:tt, :], hc_ref.shape)

    gt = gate_ref[...]
    gelu = 0.5 * gt * (1.0 + jnp.tanh(0.7978845608028654 * (gt + 0.044715 * gt * gt * gt)))
    o_ref[...] = (hs * gelu).astype(o_ref.dtype)


def _rglru_mixer(h, g_mix, w_in, conv_w, conv_b, w_a, b_a, w_x, b_x, lam, batch, seq, tt=256):
    n = batch * seq
    nt = seq // tt
    w = D_MODEL
    proj = _norm_proj(h, g_mix, w_in.astype(BF16))
    par = jnp.concatenate([conv_w.astype(F32), conv_b.reshape(1, w), b_a.reshape(1, w),
                           b_x.reshape(1, w), lam.reshape(1, w)], axis=0).astype(F32)
    return pl.pallas_call(
        functools.partial(_rglru_kernel, tt=tt),
        grid=(batch, nt),
        in_specs=[pl.BlockSpec((tt, w), lambda b, i: (b * nt + i, 0)),
                  pl.BlockSpec((tt, w), lambda b, i: (b * nt + i, 1)),
                  _full((8, w)), _full((RG_BLOCKS, RG_BW, RG_BW)), _full((RG_BLOCKS, RG_BW, RG_BW))],
        out_specs=pl.BlockSpec((tt, w), lambda b, i: (b * nt + i, 0)),
        out_shape=jax.ShapeDtypeStruct((n, w), BF16),
        scratch_shapes=[pltpu.VMEM((8, w), F32), pltpu.VMEM((8, w), F32)],
        compiler_params=_cparams(("parallel", "arbitrary")),
        name="rglru",
    )(proj, proj, par, w_a.astype(BF16), w_x.astype(BF16))


def _outproj_router_kernel(a_ref, h_ref, wo_ref, g_ref, wr_ref, br_ref, tri_ref,
                           hm_ref, xn_ref, idx_ref, gate_ref, rank_ref, cnt_ref):
    @pl.when(pl.program_id(0) == 0)
    def _():
        cnt_ref[...] = jnp.zeros_like(cnt_ref)

    hm = h_ref[...] + jnp.dot(a_ref[...], wo_ref[...], preferred_element_type=F32)
    hm_ref[...] = hm
    xn = _rms(hm, g_ref[...])
    xn_ref[...] = xn
    logit = lax.dot_general(wr_ref[...], xn, (((1,), (1,)), ((), ())),
                            preferred_element_type=F32, precision=HI) + br_ref[:, 0:1]
    ne, tm = logit.shape
    eidx = lax.broadcasted_iota(I32, (ne, tm), 0)
    work = logit
    vals, idxs, hots = [], [], []
    for _ in range(TOP_K):
        mx = jnp.max(work, axis=0, keepdims=True)
        pick = jnp.min(jnp.where(work == mx, eidx, ne), axis=0, keepdims=True)
        hot = eidx == pick
        work = jnp.where(hot, -jnp.inf, work)
        vals.append(mx)
        idxs.append(pick)
        hots.append(hot)
    ex = [jnp.exp(v - vals[0]) for v in vals]
    den = ex[0] + ex[1] + ex[2] + ex[3]
    gate_ref[...] = jnp.concatenate([e / den for e in ex], axis=0)
    idx_ref[...] = jnp.concatenate(idxs, axis=0)
    tok_hot = (hots[0] | hots[1] | hots[2] | hots[3]).astype(BF16)
    before = jnp.dot(tok_hot, tri_ref[...], preferred_element_type=F32) + cnt_ref[:, 0:1]
    rank_ref[...] = jnp.concatenate(
        [jnp.sum(jnp.where(h, before, 0.0), axis=0, keepdims=True) for h in hots], axis=0).astype(I32)
    cnt_ref[...] = cnt_ref[...] + jnp.sum(tok_hot.astype(F32), axis=1, keepdims=True)


def _outproj_router(act, h, w_out, g_ffn, w_r, b_r, tm=256):
    n, d = h.shape
    tri = jnp.asarray(np.triu(np.ones((tm, tm), np.float32), 1), BF16)
    row = lambda i: (i, 0)
    colb = lambda i: (0, i)
    return pl.pallas_call(
        _outproj_router_kernel,
        grid=(n // tm,),
        in_specs=[pl.BlockSpec((tm, d), row), pl.BlockSpec((tm, d), row), _full((d, d)), _full((1, d)),
                  _full((N_EXPERTS, d)), _full((N_EXPERTS, 128)), _full((tm, tm))],
        out_specs=[pl.BlockSpec((tm, d), row), pl.BlockSpec((tm, d), row),
                   pl.BlockSpec((TOP_K, tm), colb), pl.BlockSpec((TOP_K, tm), colb),
                   pl.BlockSpec((TOP_K, tm), colb), _full((N_EXPERTS, 128))],
        out_shape=[jax.ShapeDtypeStruct((n, d), F32), jax.ShapeDtypeStruct((n, d), F32),
                   jax.ShapeDtypeStruct((TOP_K, n), I32), jax.ShapeDtypeStruct((TOP_K, n), F32),
                   jax.ShapeDtypeStruct((TOP_K, n), I32), jax.ShapeDtypeStruct((N_EXPERTS, 128), F32)],
        compiler_params=_cparams(("arbitrary",)),
        name="outproj_router",
    )(act, h, w_out.astype(BF16), g_ffn.reshape(1, d), w_r.T.astype(F32),
      jnp.broadcast_to(b_r.astype(F32)[:, None], (N_EXPERTS, 128)), tri)


def _stage_indices(idx_hbm, idx_smem, isem, step, nsteps):
    slot = step % 2

    def copy(s, sl):
        return pltpu.make_async_copy(idx_hbm.at[s], idx_smem.at[sl], isem.at[sl])

    @pl.when(step == 0)
    def _():
        copy(0, 0).start()

    copy(step, slot).wait()

    @pl.when(step + 1 < nsteps)
    def _():
        copy(step + 1, 1 - slot).start()

    return slot


def _dispatch_kernel(cnt_ref, pst_ref, dest_hbm, x_ref, xs_hbm, idx_smem, zrow, zblk, isem, rsem, zsem, *, td):
    step = pl.program_id(0)
    slot = _stage_indices(dest_hbm, idx_smem, isem, step, pl.num_programs(0))

    def issue(r, carry):
        src = x_ref.at[pl.ds(r, 1), :]
        for k in range(TOP_K):
            d = idx_smem[slot, k * td + r]
            pltpu.make_async_copy(src, xs_hbm.at[pl.ds(d, 1), :], rsem).start()
        return carry

    lax.fori_loop(0, td, issue, 0)

    @pl.when(step == 0)
    def _():
        zrow[...] = jnp.zeros_like(zrow)

        def per_expert(e, carry):
            cnt = cnt_ref[e]
            first = pst_ref[e] + cnt
            npad = (((cnt + (MOE_BLK - 1)) // MOE_BLK) * MOE_BLK) - cnt

            def zcopy(r):
                return pltpu.make_async_copy(zrow.at[pl.ds(0, 1), :], xs_hbm.at[pl.ds(first + r, 1), :], zsem)

            def zstart(r, c):
                zcopy(r).start()
                return c

            def zwait(r, c):
                zcopy(r).wait()
                return c

            lax.fori_loop(0, npad, zstart, 0)
            lax.fori_loop(0, npad, zwait, 0)
            return carry

        lax.fori_loop(0, N_EXPERTS, per_expert, 0)

        zblk[...] = jnp.zeros_like(zblk)
        last = N_EXPERTS - 1
        used = (pst_ref[last] + cnt_ref[last] + (MOE_BLK - 1)) // MOE_BLK

        def ztail(b, c):
            start = pl.multiple_of(b * MOE_BLK, MOE_BLK)
            cp = pltpu.make_async_copy(zblk, xs_hbm.at[pl.ds(start, MOE_BLK), :], zsem)
            cp.start()
            cp.wait()
            return c

        lax.fori_loop(used, xs_hbm.shape[0] // MOE_BLK, ztail, 0)

    pltpu.make_async_copy(xs_hbm.at[pl.ds(0, TOP_K * td), :], xs_hbm.at[pl.ds(0, TOP_K * td), :], rsem).wait()


def _dispatch(xn, dest, counts, pstart, cap, td=256):
    n, d = xn.shape
    dest_t = dest.reshape(TOP_K, n // td, td).transpose(1, 0, 2).reshape(n // td, TOP_K * td)
    any_spec = pl.BlockSpec(memory_space=pl.ANY)
    grid_spec = pltpu.PrefetchScalarGridSpec(
        num_scalar_prefetch=2,
        grid=(n // td,),
        in_specs=[any_spec, pl.BlockSpec((td, d), lambda i, c, s: (i, 0))],
        out_specs=any_spec,
        scratch_shapes=[pltpu.SMEM((2, TOP_K * td), I32), pltpu.VMEM((8, d), F32), pltpu.VMEM((MOE_BLK, d), F32),
                        pltpu.SemaphoreType.DMA((2,)), pltpu.SemaphoreType.DMA, pltpu.SemaphoreType.DMA],
    )
    return pl.pallas_call(
        functools.partial(_dispatch_kernel, td=td),
        grid_spec=grid_spec,
        out_shape=jax.ShapeDtypeStruct((cap, d), F32),
        compiler_params=_cparams(("arbitrary",)),
        name="moe_dispatch",
    )(counts, pstart, dest_t, xn)


def _expert_kernel(be_ref, nu_ref, x_ref, wgu_ref, bgu_ref, wdn_ref, bdn_ref, y_ref, wgu_bf, wdn_bf):
    d = D_MODEL
    b = pl.program_id(0)

    @pl.when((b == 0) | (be_ref[b] != be_ref[jnp.maximum(b - 1, 0)]))
    def _():
        wgu_bf[...] = wgu_ref[...].astype(BF16)
        wdn_bf[...] = wdn_ref[...].astype(BF16)

    @pl.when(b < nu_ref[0])
    def _():
        xb = x_ref[...].astype(BF16)
        glu = jnp.dot(xb, wgu_bf[:, 0:d], preferred_element_type=F32) + bgu_ref[:, 0:d]
        lin = jnp.dot(xb, wgu_bf[:, d:2 * d], preferred_element_type=F32) + bgu_ref[:, d:2 * d]
        glu = jnp.minimum(glu, SWIGLU_LIMIT)
        lin = jnp.clip(lin, -SWIGLU_LIMIT, SWIGLU_LIMIT)
        act = glu * jax.nn.sigmoid(SWIGLU_ALPHA * glu) * (lin + 1.0)
        y_ref[...] = jnp.dot(act.astype(BF16), wdn_bf[...], preferred_element_type=F32) + bdn_ref[...]

    @pl.when(b >= nu_ref[0])
    def _():
        y_ref[...] = jnp.zeros_like(y_ref)


def _experts(xs, block_expert, n_used, w_gu, b_gu, w_dn, b_dn):
    cap, d = xs.shape
    nb = cap // MOE_BLK
    ne = w_gu.shape[0]

    def xmap(b, be, nu):
        return (jnp.minimum(b, nu[0] - 1), 0)

    def wmap(b, be, nu):
        return (be[b], 0, 0)

    grid_spec = pltpu.PrefetchScalarGridSpec(
        num_scalar_prefetch=2,
        grid=(nb,),
        in_specs=[pl.BlockSpec((MOE_BLK, d), xmap),
                  pl.BlockSpec((None, d, 2 * d), wmap), pl.BlockSpec((None, 1, 2 * d), wmap),
                  pl.BlockSpec((None, d, d), wmap), pl.BlockSpec((None, 1, d), wmap)],
        out_specs=pl.BlockSpec((MOE_BLK, d), lambda b, be, nu: (b, 0)),
        scratch_shapes=[pltpu.VMEM((d, 2 * d), BF16), pltpu.VMEM((d, d), BF16)],
    )
    return pl.pallas_call(
        _expert_kernel,
        grid_spec=grid_spec,
        out_shape=jax.ShapeDtypeStruct((cap, d), F32),
        compiler_params=_cparams(("arbitrary",)),
        name="moe_experts",
    )(block_expert, n_used, xs, w_gu, b_gu.reshape(ne, 1, 2 * d), w_dn, b_dn.reshape(ne, 1, d))


def _combine_ple_kernel(dest_hbm, y_hbm, hm_ref, gate_ref, p_ref, pw_ref, pn_ref, gn_ref, gw_ref,
                        o_ref, idx_smem, ybuf, isem, rsem, *, tc):
    step = pl.program_id(0)
    slot = _stage_indices(dest_hbm, idx_smem, isem, step, pl.num_programs(0))

    def issue(r, carry):
        for k in range(TOP_K):
            s = idx_smem[slot, k * tc + r]
            pltpu.make_async_copy(y_hbm.at[pl.ds(s, 1), :], ybuf.at[k, pl.ds(r, 1), :], rsem).start()
        return carry

    lax.fori_loop(0, tc, issue, 0)
    ple = _rms(jnp.dot(p_ref[...].astype(BF16), pw_ref[...], preferred_element_type=F32), pn_ref[...])
    for k in range(TOP_K):
        pltpu.make_async_copy(y_hbm.at[pl.ds(0, tc), :], ybuf.at[k], rsem).wait()
    h2 = hm_ref[...]
    for k in range(TOP_K):
        h2 = h2 + ybuf[k] * gate_ref[:, k:k + 1]
    gate = jax.nn.sigmoid(jnp.dot(_rms(h2, gn_ref[...]).astype(BF16), gw_ref[...], preferred_element_type=F32))
    o_ref[...] = h2 + ple * gate


def _combine_ple(y, dest, h_mid, gates, p, ple_w, ple_norm, gate_norm, gate_w, tc=256):
    n, d = h_mid.shape
    dest_t = dest.reshape(TOP_K, n // tc, tc).transpose(1, 0, 2).reshape(n // tc, TOP_K * tc)
    any_spec = pl.BlockSpec(memory_space=pl.ANY)
    row = lambda i: (i, 0)
    return pl.pallas_call(
        functools.partial(_combine_ple_kernel, tc=tc),
        grid=(n // tc,),
        in_specs=[any_spec, any_spec, pl.BlockSpec((tc, d), row), pl.BlockSpec((tc, TOP_K), row),
                  pl.BlockSpec((tc, PLE_DIM), row), _full((PLE_DIM, d)), _full((1, d)), _full((1, d)),
                  _full((d, d))],
        out_specs=pl.BlockSpec((tc, d), row),
        out_shape=jax.ShapeDtypeStruct((n, d), F32),
        scratch_shapes=[pltpu.SMEM((2, TOP_K * tc), I32), pltpu.VMEM((TOP_K, tc, d), F32),
                        pltpu.SemaphoreType.DMA((2,)), pltpu.SemaphoreType.DMA],
        compiler_params=_cparams(("arbitrary",)),
        name="moe_combine_ple",
    )(dest_t, y, h_mid, gates.T, p, ple_w.astype(BF16), ple_norm.reshape(1, d),
      gate_norm.reshape(1, d), gate_w.astype(BF16))


def _moe_ple(act, h, w_out, g_ffn, w_r, b_r, layer, w_gu, b_gu, w_dn, b_dn, p, ple_w, ple_norm, gate_norm, gate_w):
    n, d = h.shape
    h_mid, xn, idx, gates, rank, cnt = _outproj_router(act, h, w_out, g_ffn, w_r, b_r)
    counts = cnt[:, 0].astype(I32)
    padded = (counts + MOE_BLK - 1) // MOE_BLK * MOE_BLK
    pend = jnp.cumsum(padded)
    pstart = pend - padded
    hot = idx[:, :, None] == jnp.arange(N_EXPERTS, dtype=I32)[None, None, :]
    dest = jnp.sum(jnp.where(hot, pstart[None, None, :], 0), axis=-1) + rank
    nb = -(-(n * TOP_K) // MOE_BLK) + N_EXPERTS
    cap = nb * MOE_BLK
    block_start = jnp.arange(nb, dtype=I32) * MOE_BLK
    block_expert = jnp.minimum(jnp.sum(pend[None, :] <= block_start[:, None], axis=1), N_EXPERTS - 1).astype(I32)
    n_used = (pend[-1:] // MOE_BLK).astype(I32)
    xs = _dispatch(xn, dest, counts, pstart.astype(I32), cap)
    y = _experts(xs, block_expert + layer * N_EXPERTS, n_used, w_gu, b_gu, w_dn, b_dn)
    return _combine_ple(y, dest, h_mid, gates, p, ple_w, ple_norm, gate_norm, gate_w)


def kernel(x, p, norm_mix, norm_ffn, hg_w_in, hg_w_out, hg_gnorm, hg_lb_param, fox_w_in, fox_f_bias, fox_qnorm, fox_knorm, fox_w_out, rg_w_in, rg_conv_w, rg_conv_b, rg_wa, rg_ba, rg_wx, rg_bx, rg_lambda, rg_w_out, router_w, router_b, moe_w_gu, moe_b_gu, moe_w_dn, moe_b_dn, ple_w, ple_norm, ple_gate_norm, ple_gate_w):
    batch, seq, d = x.shape
    depth = p.shape[0]
    n = batch * seq
    lb_all = jnp.cumsum(jax.nn.softmax(hg_lb_param.astype(F32), axis=0), axis=0)
    lb_all = lb_all - lb_all[0]
    h = x.reshape(n, d).astype(F32)
    ne = depth * N_EXPERTS
    w_gu = moe_w_gu.astype(F32).reshape(ne, d, 2 * d)
    b_gu = moe_b_gu.astype(F32).reshape(ne, 2 * d)
    w_dn = moe_w_dn.astype(F32).reshape(ne, d, d)
    b_dn = moe_b_dn.astype(F32).reshape(ne, d)
    for i in range(depth):
        j = i // 3
        kind = i % 3
        if kind == 0:
            act = _hgrn2_mixer(h, norm_mix[i], hg_w_in[j], None, hg_gnorm[j], lb_all[i], batch, seq)
            w_out = hg_w_out[j]
        elif kind == 1:
            act = _fox_mixer(h, norm_mix[i], fox_w_in[j], fox_f_bias[j], fox_qnorm[j], fox_knorm[j], batch, seq)
            w_out = fox_w_out[j]
        else:
            act = _rglru_mixer(h, norm_mix[i], rg_w_in[j], rg_conv_w[j], rg_conv_b[j], rg_wa[j], rg_ba[j],
                               rg_wx[j], rg_bx[j], rg_lambda[j], batch, seq)
            w_out = rg_w_out[j]
        h = _moe_ple(act, h, w_out, norm_ffn[i], router_w[i], router_b[i], i, w_gu, b_gu, w_dn, b_dn,
                     p[i].reshape(n, PLE_DIM), ple_w[i], ple_norm[i], ple_gate_norm[i], ple_gate_w[i])
    return h.reshape(batch, seq, d)
```

```python
import functools

import jax
import jax.numpy as jnp
import numpy as np
from jax import lax
from jax.experimental import pallas as pl
from jax.experimental.pallas import tpu as pltpu

F32 = jnp.float32
BF16 = jnp.bfloat16
I32 = jnp.int32

D_MODEL = 1024
EPS = 1e-6
PLE_DIM = 256

HG_HEADS = 8
HG_DK = 128
HG_CHUNK = 128
HG_LEVELS = 7

FOX_HEADS = 16
FOX_HD = 64
FOX_ZERO_EXP = -105.0

RG_BLOCKS = 4
RG_BW = 256
CONV_W = 4
RG_C = 8.0

N_EXPERTS = 32
TOP_K = 4
SWIGLU_LIMIT = 7.0
SWIGLU_ALPHA = 1.702
MOE_BLK = 256
MOE_TILE = 512
STRIP_ALIGN = 8
STAGE_ROWS = -(-(TOP_K * MOE_TILE + N_EXPERTS * (STRIP_ALIGN - 1)) // 256) * 256

VMEM_LIMIT = 56 * 1024 * 1024

HI = lax.Precision.HIGHEST


def _cparams(sem):
    return pltpu.CompilerParams(dimension_semantics=sem, vmem_limit_bytes=VMEM_LIMIT)


def _bdot(a, b):
    return jnp.dot(a.astype(BF16), b.astype(BF16), preferred_element_type=F32)


def _bdot_nt(a, b):
    return lax.dot_general(a.astype(BF16), b.astype(BF16), (((1,), (1,)), ((), ())),
                           preferred_element_type=F32)


def _bdot_tn(a, b):
    return lax.dot_general(a.astype(BF16), b.astype(BF16), (((0,), (0,)), ((), ())),
                           preferred_element_type=F32)


def _rms(x, g):
    return x * lax.rsqrt(jnp.mean(x * x, axis=-1, keepdims=True) + EPS) * g


def _split2(x):
    hi = x.astype(BF16)
    lo = (x - hi.astype(F32)).astype(BF16)
    return hi, lo


def _split3(x):
    hi = x.astype(BF16)
    r = x - hi.astype(F32)
    mid = r.astype(BF16)
    lo = (r - mid.astype(F32)).astype(BF16)
    return hi, mid, lo


def _log_sigmoid(z):
    return jnp.minimum(z, 0.0) - jnp.log1p(jnp.exp(-jnp.abs(z)))


def _full(shape):
    return pl.BlockSpec(shape, lambda *_: (0,) * len(shape))


def _norm_proj_kernel(h_ref, g_ref, w_ref, o_ref, *, cn):
    xn = _rms(h_ref[...], g_ref[...]).astype(BF16)
    m = w_ref.shape[1]
    for c in range(m // cn):
        o_ref[:, c * cn:(c + 1) * cn] = jnp.dot(
            xn, w_ref[:, c * cn:(c + 1) * cn], preferred_element_type=F32).astype(o_ref.dtype)


def _norm_proj(h, g, w, tm=256, out_dtype=F32):
    n, d = h.shape
    m = w.shape[1]
    return pl.pallas_call(
        functools.partial(_norm_proj_kernel, cn=512),
        grid=(n // tm,),
        in_specs=[pl.BlockSpec((tm, d), lambda i: (i, 0)), _full((1, d)), _full((d, m))],
        out_specs=pl.BlockSpec((tm, m), lambda i: (i, 0)),
        out_shape=jax.ShapeDtypeStruct((n, m), out_dtype),
        compiler_params=_cparams(("parallel",)),
        name="norm_proj",
    )(h, g.reshape(1, d), w)


def _hgrn2_consts():
    c = HG_CHUNK
    t = np.arange(c)
    tril = (t[:, None] >= t[None, :]).astype(np.float32)
    sel = np.zeros((HG_LEVELS, c, c), np.float32)
    for l in range(HG_LEVELS):
        hs = 1 << l
        m = (t // (2 * hs)) * (2 * hs) + hs - 1
        sel[l, t, m] = 1.0
    return jnp.asarray(tril, BF16), jnp.asarray(sel.reshape(HG_LEVELS * c, c), BF16)


def _hgrn2_kernel(q_ref, z_ref, v_ref, g_ref, par_ref, tril_ref, sel_ref, o_ref, st_ref, *, nchunk):
    c = HG_CHUNK

    @pl.when(pl.program_id(2) == 0)
    def _():
        st_ref[...] = jnp.zeros_like(st_ref)

    log_lb = par_ref[0:1, :]
    log1m_lb = par_ref[1:2, :]
    one_m_lb = par_ref[2:3, :]
    gnorm = par_ref[3:4, :]
    row = lax.broadcasted_iota(I32, (c, c), 0)
    col = lax.broadcasted_iota(I32, (c, c), 1)

    def chunk(ci, carry):
        r = pl.ds(pl.multiple_of(ci * c, c), c)
        qr = q_ref[r, :]
        z = z_ref[r, :]
        vb = v_ref[r, :].astype(BF16)
        gt = g_ref[r, :]
        q = qr * jax.nn.sigmoid(qr) * (HG_DK ** -0.5)
        b = log1m_lb + _log_sigmoid(z)
        lf = jnp.maximum(log_lb, b) + jnp.log1p(jnp.exp(-jnp.abs(log_lb - b)))
        k = one_m_lb * jax.nn.sigmoid(-z)
        tril = tril_ref[...]
        hi, mid, lo = _split3(lf)
        cum = (jnp.dot(tril, hi, preferred_element_type=F32)
               + jnp.dot(tril, mid, preferred_element_type=F32)
               + jnp.dot(tril, lo, preferred_element_type=F32))
        chi, clo = _split2(cum)
        refs = (jnp.dot(sel_ref[...], chi, preferred_element_type=F32)
                + jnp.dot(sel_ref[...], clo, preferred_element_type=F32))
        a = jnp.where(row == col, _bdot_nt(q, k), 0.0)
        for l in range(HG_LEVELS):
            hs = 1 << l
            ref = refs[l * c:(l + 1) * c, :]
            if hs >= 8:
                parts = []
                for blk in range(c // hs):
                    sl = slice(blk * hs, (blk + 1) * hs)
                    if blk % 2:
                        parts.append(q[sl] * jnp.exp(jnp.minimum(cum[sl] - ref[sl], 0.0)))
                    else:
                        parts.append(k[sl] * jnp.exp(jnp.minimum(ref[sl] - cum[sl], 0.0)))
                x = jnp.concatenate(parts, axis=0).astype(BF16)
            else:
                x = (jnp.where((row & hs) != 0, q, k) * jnp.exp(-jnp.abs(cum - ref))).astype(BF16)
            a = jnp.where((((row ^ col) >> l) == 1) & (row > col), _bdot_nt(x, x), a)
        st = st_ref[...]
        o = _bdot(a, vb) + _bdot_nt(q * jnp.exp(cum), st)
        last = cum[c - 1:c, :]
        kd = k * jnp.exp(last - cum)
        st_ref[...] = st * jnp.exp(last) + _bdot_tn(vb, kd)
        y = _rms(o, gnorm) * (gt * jax.nn.sigmoid(gt))
        o_ref[r, :] = y.astype(o_ref.dtype)
        return carry

    lax.fori_loop(0, nchunk, chunk, 0, unroll=True)


def _hgrn2_recurrence(proj, par, batch, seq, tt=512):
    n = batch * seq
    nt = seq // tt
    tril, sel = _hgrn2_consts()

    def part(p):
        return pl.BlockSpec((tt, HG_DK), lambda b, h, i, p=p: (b * nt + i, p * HG_HEADS + h))

    return pl.pallas_call(
        functools.partial(_hgrn2_kernel, nchunk=tt // HG_CHUNK),
        grid=(batch, HG_HEADS, nt),
        in_specs=[part(0), part(1), part(2), part(3),
                  pl.BlockSpec((8, HG_DK), lambda b, h, i: (0, h)),
                  _full(tril.shape), _full(sel.shape)],
        out_specs=pl.BlockSpec((tt, HG_DK), lambda b, h, i: (b * nt + i, h)),
        out_shape=jax.ShapeDtypeStruct((n, D_MODEL), BF16),
        scratch_shapes=[pltpu.VMEM((HG_DK, HG_DK), F32)],
        compiler_params=_cparams(("parallel", "parallel", "arbitrary")),
        name="hgrn2_recurrence",
    )(proj, proj, proj, proj, par, tril, sel)


def _hgrn2_mixer(h, g_mix, w_in, w_out_unused, g_norm, lb, batch, seq):
    del w_out_unused
    proj = _norm_proj(h, g_mix, w_in.astype(BF16))
    par = jnp.zeros((8, D_MODEL), F32)
    par = par.at[0].set(jnp.log(lb)).at[1].set(jnp.log1p(-lb)).at[2].set(1.0 - lb)
    par = par.at[3].set(jnp.tile(g_norm.astype(F32), HG_HEADS))
    return _hgrn2_recurrence(proj, par, batch, seq)


def _fox_proj_kernel(h_ref, g_ref, w_ref, wf_ref, fb_ref, qg_ref, kg_ref, gs_ref, gst_ref, tril_ref,
                     o_ref, cum_ref, carry_ref, *, tiles_per_seq):
    d = D_MODEL

    @pl.when(pl.program_id(0) % tiles_per_seq == 0)
    def _():
        carry_ref[...] = jnp.zeros_like(carry_ref)

    xn = _rms(h_ref[...], g_ref[...])
    xb = xn.astype(BF16)

    def headnorm(t, gain):
        shi, slo = _split2(t * t)
        ssq = (jnp.dot(shi, gs_ref[...], preferred_element_type=F32)
               + jnp.dot(slo, gs_ref[...], preferred_element_type=F32))
        inv = lax.rsqrt(ssq * (1.0 / FOX_HD) + EPS)
        ihi, ilo = _split2(inv)
        invf = (jnp.dot(ihi, gst_ref[...], preferred_element_type=F32)
                + jnp.dot(ilo, gst_ref[...], preferred_element_type=F32))
        return t * invf * gain

    q = jnp.dot(xb, w_ref[:, 0:d], preferred_element_type=F32)
    o_ref[:, 0:d] = (headnorm(q, qg_ref[...]) * (FOX_HD ** -0.5)).astype(o_ref.dtype)
    k = jnp.dot(xb, w_ref[:, d:2 * d], preferred_element_type=F32)
    o_ref[:, d:2 * d] = headnorm(k, kg_ref[...]).astype(o_ref.dtype)
    for c in range(2, 4):
        o_ref[:, c * d:(c + 1) * d] = jnp.dot(
            xb, w_ref[:, c * d:(c + 1) * d], preferred_element_type=F32).astype(o_ref.dtype)

    fl = jnp.dot(xn, wf_ref[...], preferred_element_type=F32, precision=HI) + fb_ref[...]
    hi, mid, lo = _split3(_log_sigmoid(fl))
    tril = tril_ref[...]
    cum = (jnp.dot(tril, hi, preferred_element_type=F32)
           + jnp.dot(tril, mid, preferred_element_type=F32)
           + jnp.dot(tril, lo, preferred_element_type=F32)) + carry_ref[0:1, :]
    cum_ref[...] = cum
    tm = cum.shape[0]
    carry_ref[...] = jnp.broadcast_to(cum[tm - 1:tm, :], carry_ref.shape)


def _fox_proj(h, g_mix, w_in, f_bias, q_norm, k_norm, seq, tm=256):
    n, d = h.shape
    w_main = w_in[:, :4 * d].astype(BF16)
    w_f = jnp.zeros((d, 128), F32).at[:, :FOX_HEADS].set(w_in[:, 4 * d:].astype(F32))
    fb = jnp.zeros((1, 128), F32).at[0, :FOX_HEADS].set(f_bias.astype(F32))
    head_of = np.arange(d) // FOX_HD
    gs_np = (head_of[:, None] == np.arange(128)[None, :]).astype(np.float32)
    gs = jnp.asarray(gs_np, BF16)
    gst = jnp.asarray(gs_np.T, BF16)
    tril = jnp.asarray(np.tril(np.ones((tm, tm), np.float32)), BF16)
    qg = jnp.tile(q_norm.astype(F32), FOX_HEADS).reshape(1, d)
    kg = jnp.tile(k_norm.astype(F32), FOX_HEADS).reshape(1, d)
    return pl.pallas_call(
        functools.partial(_fox_proj_kernel, tiles_per_seq=seq // tm),
        grid=(n // tm,),
        in_specs=[pl.BlockSpec((tm, d), lambda i: (i, 0)), _full((1, d)), _full((d, 4 * d)),
                  _full((d, 128)), _full((1, 128)), _full((1, d)), _full((1, d)),
                  _full((d, 128)), _full((128, d)), _full((tm, tm))],
        out_specs=[pl.BlockSpec((tm, 4 * d), lambda i: (i, 0)),
                   pl.BlockSpec((tm, 128), lambda i: (i, 0))],
        out_shape=[jax.ShapeDtypeStruct((n, 4 * d), BF16), jax.ShapeDtypeStruct((n, 128), F32)],
        scratch_shapes=[pltpu.VMEM((8, 128), F32)],
        compiler_params=_cparams(("arbitrary",)),
        name="fox_proj",
    )(h, g_mix.reshape(1, d), w_main, w_f, fb, qg, kg, gs, gst, tril)


def _fox_attn_kernel(jlo_ref, q_ref, k_ref, v_ref, g_ref, cq_ref, ck_ref, o_ref,
                     qh_ref, m_ref, l_ref, acc_ref, *, tq, nq):
    qi = pl.program_id(2)
    lane = lax.broadcasted_iota(I32, (tq, 128), 1)
    qv = q_ref[...]
    qh_ref[0] = jnp.where(lane < FOX_HD, qv, jnp.zeros_like(qv))
    qh_ref[1] = jnp.where(lane < FOX_HD, jnp.zeros_like(qv), qv)
    m_ref[...] = jnp.full_like(m_ref, -jnp.inf)
    l_ref[...] = jnp.zeros_like(l_ref)
    acc_ref[...] = jnp.zeros_like(acc_ref)

    def sweep(kj, masked):
        rows = pl.ds(pl.multiple_of(kj * tq, tq), tq)
        kb = k_ref[rows, :]
        vb = v_ref[rows, :]
        for hh in range(2):
            s = lax.dot_general(qh_ref[hh], kb, (((1,), (1,)), ((), ())), preferred_element_type=F32)
            s = s + (cq_ref[hh:hh + 1, 0:1] - ck_ref[hh:hh + 1, rows])
            if masked:
                r = lax.broadcasted_iota(I32, s.shape, 0)
                cc = lax.broadcasted_iota(I32, s.shape, 1)
                s = jnp.where(cc <= r, s, -jnp.inf)
            m_old = m_ref[hh]
            m_new = jnp.maximum(m_old, jnp.max(s, axis=-1, keepdims=True))
            alpha = jnp.exp(m_old - m_new)
            p = jnp.exp(s - m_new[:, 0:1])
            l_ref[hh] = alpha * l_ref[hh] + jnp.sum(p, axis=-1, keepdims=True)
            acc_ref[hh] = alpha * acc_ref[hh] + jnp.dot(p.astype(BF16), vb, preferred_element_type=F32)
            m_ref[hh] = m_new

    def body(kj, carry):
        sweep(kj, False)
        return carry

    lo = jlo_ref[(pl.program_id(0) * pl.num_programs(1) + pl.program_id(1)) * nq + qi]
    lax.fori_loop(lo, qi, body, 0)
    sweep(qi, True)
    o = jnp.where(lane < FOX_HD, acc_ref[0] / l_ref[0], acc_ref[1] / l_ref[1])
    gt = g_ref[...].astype(F32)
    o_ref[...] = (o * jax.nn.sigmoid(gt)).astype(o_ref.dtype)


def _fox_first_block(cum_t, logit_bound, tq):
    bh, _, seq = cum_t.shape
    nq = seq // tq
    blk = cum_t.reshape(bh, 2, nq, tq)
    gap = blk[:, :, :, None, 0] - blk[:, :, None, :, tq - 1]
    dead = jnp.all(2.0 * logit_bound + gap < FOX_ZERO_EXP, axis=1)
    dead = dead & (jnp.arange(nq)[None, None, :] < jnp.arange(nq)[None, :, None])
    return jnp.sum(jnp.cumprod(dead.astype(I32), axis=-1), axis=-1).astype(I32).reshape(-1)


def _fox_attention(qkvg, cum, logit_bound, batch, seq, tq=512):
    n = batch * seq
    nq = seq // tq
    hp = FOX_HEADS // 2
    cum_t = cum[:, :FOX_HEADS].reshape(batch, seq, hp, 2).transpose(0, 2, 3, 1).reshape(batch * hp, 2, seq)
    jlo = _fox_first_block(cum_t, logit_bound, tq)
    grid_spec = pltpu.PrefetchScalarGridSpec(
        num_scalar_prefetch=1,
        grid=(batch, hp, nq),
        in_specs=[
            pl.BlockSpec((tq, 128), lambda b, h, i, jlo: (b * nq + i, h)),
            pl.BlockSpec((seq, 128), lambda b, h, i, jlo: (b, hp + h)),
            pl.BlockSpec((seq, 128), lambda b, h, i, jlo: (b, 2 * hp + h)),
            pl.BlockSpec((tq, 128), lambda b, h, i, jlo: (b * nq + i, 3 * hp + h)),
            pl.BlockSpec((None, 2, tq), lambda b, h, i, jlo: (b * hp + h, 0, i)),
            pl.BlockSpec((None, 2, seq), lambda b, h, i, jlo: (b * hp + h, 0, 0)),
        ],
        out_specs=pl.BlockSpec((tq, 128), lambda b, h, i, jlo: (b * nq + i, h)),
        scratch_shapes=[pltpu.VMEM((2, tq, 128), BF16), pltpu.VMEM((2, tq, 128), F32),
                        pltpu.VMEM((2, tq, 128), F32), pltpu.VMEM((2, tq, 128), F32)],
    )
    return pl.pallas_call(
        functools.partial(_fox_attn_kernel, tq=tq, nq=nq),
        grid_spec=grid_spec,
        out_shape=jax.ShapeDtypeStruct((n, D_MODEL), BF16),
        compiler_params=_cparams(("parallel", "parallel", "arbitrary")),
        name="fox_attention",
    )(jlo, qkvg, qkvg, qkvg, qkvg, cum_t, cum_t)


def _fox_mixer(h, g_mix, w_in, f_bias, q_norm, k_norm, batch, seq):
    qkvg, cum = _fox_proj(h, g_mix, w_in, f_bias, q_norm, k_norm, seq)
    logit_bound = 1.02 * FOX_HD ** 0.5 * jnp.max(jnp.abs(q_norm.astype(F32))) * jnp.max(jnp.abs(k_norm.astype(F32)))
    return _fox_attention(qkvg, cum, logit_bound, batch, seq)


def _rglru_kernel(gate_ref, u_ref, par_ref, wa_ref, wx_ref, o_ref, prev_ref, hc_ref, *, tt):
    @pl.when(pl.program_id(1) == 0)
    def _():
        prev_ref[...] = jnp.zeros_like(prev_ref)
        hc_ref[...] = jnp.zeros_like(hc_ref)

    u = u_ref[...]
    ext = jnp.concatenate([prev_ref[...], u], axis=0)
    conv = par_ref[4:5, :] + u * par_ref[3:4, :]
    for shift in range(1, CONV_W):
        conv = conv + pltpu.roll(ext, shift, 0)[8:, :] * par_ref[3 - shift:4 - shift, :]
    prev_ref[...] = u[tt - 8:, :]

    cb = conv.astype(BF16)
    ra, ia = [], []
    for nb in range(RG_BLOCKS):
        blk = cb[:, nb * RG_BW:(nb + 1) * RG_BW]
        ra.append(jnp.dot(blk, wa_ref[nb], preferred_element_type=F32))
        ia.append(jnp.dot(blk, wx_ref[nb], preferred_element_type=F32))
    r = jax.nn.sigmoid(jnp.concatenate(ra, axis=1) + par_ref[5:6, :])
    ig = jax.nn.sigmoid(jnp.concatenate(ia, axis=1) + par_ref[6:7, :])
    lam = par_ref[7:8, :]
    softplus = jnp.maximum(-lam, 0.0) + jnp.log1p(jnp.exp(-jnp.abs(lam)))
    log_a = -RG_C * r * softplus
    a = jnp.exp(log_a)
    b = jnp.sqrt(1.0 - a * a) * (ig * conv)

    row = lax.broadcasted_iota(I32, a.shape, 0)
    dist = 1
    while dist < tt:
        ok = row >= dist
        a_sh = pltpu.roll(a, dist, 0)
        b_sh = pltpu.roll(b, dist, 0)
        b = jnp.where(ok, a * b_sh + b, b)
        a = jnp.where(ok, a * a_sh, a)
        dist *= 2
    hs = b + a * hc_ref[0:1, :]
    hc_ref[...] = jnp.broadcast_to(hs[tt - 1:tt, :], hc_ref.shape)

    gt = gate_ref[...]
    gelu = 0.5 * gt * (1.0 + jnp.tanh(0.7978845608028654 * (gt + 0.044715 * gt * gt * gt)))
    o_ref[...] = (hs * gelu).astype(o_ref.dtype)


def _rglru_mixer(h, g_mix, w_in, conv_w, conv_b, w_a, b_a, w_x, b_x, lam, batch, seq, tt=256):
    n = batch * seq
    nt = seq // tt
    w = D_MODEL
    proj = _norm_proj(h, g_mix, w_in.astype(BF16))
    par = jnp.concatenate([conv_w.astype(F32), conv_b.reshape(1, w), b_a.reshape(1, w),
                           b_x.reshape(1, w), lam.reshape(1, w)], axis=0).astype(F32)
    return pl.pallas_call(
        functools.partial(_rglru_kernel, tt=tt),
        grid=(batch, nt),
        in_specs=[pl.BlockSpec((tt, w), lambda b, i: (b * nt + i, 0)),
                  pl.BlockSpec((tt, w), lambda b, i: (b * nt + i, 1)),
                  _full((8, w)), _full((RG_BLOCKS, RG_BW, RG_BW)), _full((RG_BLOCKS, RG_BW, RG_BW))],
        out_specs=pl.BlockSpec((tt, w), lambda b, i: (b * nt + i, 0)),
        out_shape=jax.ShapeDtypeStruct((n, w), BF16),
        scratch_shapes=[pltpu.VMEM((8, w), F32), pltpu.VMEM((8, w), F32)],
        compiler_params=_cparams(("parallel", "arbitrary")),
        name="rglru",
    )(proj, proj, par, w_a.astype(BF16), w_x.astype(BF16))


def _outproj_router_kernel(a_ref, h_ref, wo_ref, g_ref, wr_ref, br_ref, tri_ref, lst_ref,
                           hm_ref, xn_ref, gate_ref, pos_ref, cnt_ref):
    hm = h_ref[...] + jnp.dot(a_ref[...], wo_ref[...], preferred_element_type=F32)
    hm_ref[...] = hm
    xn = _rms(hm, g_ref[...])
    xn_ref[...] = xn.astype(xn_ref.dtype)
    logit = lax.dot_general(wr_ref[...], xn, (((1,), (1,)), ((), ())),
                            preferred_element_type=F32, precision=HI) + br_ref[:, 0:1]
    ne, tm = logit.shape
    eidx = lax.broadcasted_iota(I32, (ne, tm), 0)
    work = logit
    vals, hots = [], []
    for _ in range(TOP_K):
        mx = jnp.max(work, axis=0, keepdims=True)
        pick = jnp.min(jnp.where(work == mx, eidx, ne), axis=0, keepdims=True)
        hot = eidx == pick
        work = jnp.where(hot, -jnp.inf, work)
        vals.append(mx)
        hots.append(hot)
    ex = [jnp.exp(v - vals[0]) for v in vals]
    den = ex[0] + ex[1] + ex[2] + ex[3]
    gate_ref[...] = jnp.concatenate([e / den for e in ex], axis=0)
    tok_hot = (hots[0] | hots[1] | hots[2] | hots[3]).astype(BF16)
    before = jnp.dot(tok_hot, tri_ref[...], preferred_element_type=F32)
    count = jnp.sum(tok_hot.astype(F32), axis=1, keepdims=True)
    units = jnp.floor((count + (STRIP_ALIGN - 1)) * (1.0 / STRIP_ALIGN))
    units_b = jnp.broadcast_to(units, (ne, 128)).astype(BF16)
    start = STRIP_ALIGN * jnp.dot(lst_ref[...], units_b, preferred_element_type=F32)[:, 0:1]
    where = before + start
    pos_ref[...] = jnp.concatenate(
        [jnp.sum(jnp.where(h, where, 0.0), axis=0, keepdims=True) for h in hots], axis=0).astype(I32)
    cnt_ref[...] = jnp.broadcast_to(STRIP_ALIGN * units, cnt_ref.shape)


def _outproj_router(act, h, w_out, g_ffn, w_r, b_r):
    n, d = h.shape
    tm = MOE_TILE
    tri = jnp.asarray(np.triu(np.ones((tm, tm), np.float32), 1), BF16)
    lst = jnp.asarray(np.tril(np.ones((N_EXPERTS, N_EXPERTS), np.float32), -1), BF16)
    row = lambda i: (i, 0)
    colb = lambda i: (0, i)
    return pl.pallas_call(
        _outproj_router_kernel,
        grid=(n // tm,),
        in_specs=[pl.BlockSpec((tm, d), row), pl.BlockSpec((tm, d), row), _full((d, d)), _full((1, d)),
                  _full((N_EXPERTS, d)), _full((N_EXPERTS, 128)), _full((tm, tm)),
                  _full((N_EXPERTS, N_EXPERTS))],
        out_specs=[pl.BlockSpec((tm, d), row), pl.BlockSpec((tm, d), row),
                   pl.BlockSpec((TOP_K, tm), colb), pl.BlockSpec((TOP_K, tm), colb),
                   pl.BlockSpec((N_EXPERTS, 128), row)],
        out_shape=[jax.ShapeDtypeStruct((n, d), F32), jax.ShapeDtypeStruct((n, d), BF16),
                   jax.ShapeDtypeStruct((TOP_K, n), F32), jax.ShapeDtypeStruct((TOP_K, n), I32),
                   jax.ShapeDtypeStruct((n // tm * N_EXPERTS, 128), F32)],
        compiler_params=_cparams(("parallel",)),
        name="outproj_router",
    )(act, h, w_out.astype(BF16), g_ffn.reshape(1, d), w_r.T.astype(F32),
      jnp.broadcast_to(b_r.astype(F32)[:, None], (N_EXPERTS, 128)), tri, lst)


def _for_strip_chunks(nrows, max_rows, fn):
    size = max_rows
    while size >= STRIP_ALIGN:
        @pl.when((nrows & size) != 0)
        def _(size=size):
            fn(pl.multiple_of(nrows & ~(2 * size - 1), STRIP_ALIGN), size)
        size //= 2


def _for_tile_strips(n8_ref, sbase_ref, gdst_ref, tile, fn):
    def per_expert(e, carry):
        i = tile * N_EXPERTS + e
        so = sbase_ref[i]
        gd = gdst_ref[i]
        _for_strip_chunks(n8_ref[i], MOE_TILE,
                          lambda o, size: fn(pl.multiple_of(so + o, STRIP_ALIGN), pl.multiple_of(gd + o, STRIP_ALIGN), size))
        return carry

    lax.fori_loop(0, N_EXPERTS, per_expert, 0)


def _dispatch_kernel(n8_ref, sbase_ref, gdst_ref, tot_ref, pst_ref, x_ref, pos_ref, xs_hbm,
                     stage, zblk, rsem, zsem):
    tile = pl.program_id(0)
    rows = lax.broadcasted_iota(I32, (STAGE_ROWS, MOE_TILE), 0)
    onehot = jnp.zeros((STAGE_ROWS, MOE_TILE), F32)
    for k in range(TOP_K):
        onehot = jnp.where(rows == pos_ref[k:k + 1, :], 1.0, onehot)
    stage[...] = jnp.dot(onehot.astype(BF16), x_ref[...], preferred_element_type=F32)

    def copy(so, gd, size):
        return pltpu.make_async_copy(stage.at[pl.ds(so, size), :], xs_hbm.at[pl.ds(gd, size), :], rsem)

    _for_tile_strips(n8_ref, sbase_ref, gdst_ref, tile, lambda so, gd, size: copy(so, gd, size).start())

    @pl.when(tile == 0)
    def _():
        zblk[...] = jnp.zeros_like(zblk)

        def zero(first, size):
            cp = pltpu.make_async_copy(zblk.at[pl.ds(0, size), :], xs_hbm.at[pl.ds(first, size), :], zsem)
            cp.start()
            cp.wait()

        def per_expert(e, carry):
            tot = tot_ref[e]
            first = pst_ref[e] + tot
            npad = ((tot + (MOE_BLK - 1)) // MOE_BLK) * MOE_BLK - tot
            _for_strip_chunks(npad, MOE_BLK // 2,
                              lambda o, size: zero(pl.multiple_of(first + o, STRIP_ALIGN), size))
            return carry

        lax.fori_loop(0, N_EXPERTS, per_expert, 0)

        last = N_EXPERTS - 1
        used = (pst_ref[last] + tot_ref[last] + (MOE_BLK - 1)) // MOE_BLK

        def ztail(b, c):
            zero(pl.multiple_of(b * MOE_BLK, MOE_BLK), MOE_BLK)
            return c

        lax.fori_loop(used, xs_hbm.shape[0] // MOE_BLK, ztail, 0)

    _for_tile_strips(n8_ref, sbase_ref, gdst_ref, tile, lambda so, gd, size: copy(so, gd, size).wait())


def _dispatch(xn, pos, n8, sbase, gdst, total, pstart, cap):
    n, d = xn.shape
    tm = MOE_TILE
    grid_spec = pltpu.PrefetchScalarGridSpec(
        num_scalar_prefetch=5,
        grid=(n // tm,),
        in_specs=[pl.BlockSpec((tm, d), lambda i, *_: (i, 0)), pl.BlockSpec((TOP_K, tm), lambda i, *_: (0, i))],
        out_specs=pl.BlockSpec(memory_space=pl.ANY),
        scratch_shapes=[pltpu.VMEM((STAGE_ROWS, d), F32), pltpu.VMEM((MOE_BLK, d), F32),
                        pltpu.SemaphoreType.DMA, pltpu.SemaphoreType.DMA],
    )
    return pl.pallas_call(
        _dispatch_kernel,
        grid_spec=grid_spec,
        out_shape=jax.ShapeDtypeStruct((cap, d), F32),
        compiler_params=_cparams(("arbitrary",)),
        name="moe_dispatch",
    )(n8, sbase, gdst, total, pstart, xn, pos)


def _expert_kernel(be_ref, nu_ref, x_ref, wgu_ref, bgu_ref, wdn_ref, bdn_ref, y_ref, wgu_bf, wdn_bf):
    d = D_MODEL
    b = pl.program_id(0)

    @pl.when((b == 0) | (be_ref[b] != be_ref[jnp.maximum(b - 1, 0)]))
    def _():
        wgu_bf[...] = wgu_ref[...].astype(BF16)
        wdn_bf[...] = wdn_ref[...].astype(BF16)

    @pl.when(b < nu_ref[0])
    def _():
        xb = x_ref[...].astype(BF16)
        glu = jnp.dot(xb, wgu_bf[:, 0:d], preferred_element_type=F32) + bgu_ref[:, 0:d]
        lin = jnp.dot(xb, wgu_bf[:, d:2 * d], preferred_element_type=F32) + bgu_ref[:, d:2 * d]
        glu = jnp.minimum(glu, SWIGLU_LIMIT)
        lin = jnp.clip(lin, -SWIGLU_LIMIT, SWIGLU_LIMIT)
        act = glu * jax.nn.sigmoid(SWIGLU_ALPHA * glu) * (lin + 1.0)
        y_ref[...] = jnp.dot(act.astype(BF16), wdn_bf[...], preferred_element_type=F32) + bdn_ref[...]

    @pl.when(b >= nu_ref[0])
    def _():
        y_ref[...] = jnp.zeros_like(y_ref)


def _experts(xs, block_expert, n_used, w_gu, b_gu, w_dn, b_dn):
    cap, d = xs.shape
    nb = cap // MOE_BLK
    ne = w_gu.shape[0]

    def xmap(b, be, nu):
        return (jnp.minimum(b, nu[0] - 1), 0)

    def wmap(b, be, nu):
        return (be[b], 0, 0)

    grid_spec = pltpu.PrefetchScalarGridSpec(
        num_scalar_prefetch=2,
        grid=(nb,),
        in_specs=[pl.BlockSpec((MOE_BLK, d), xmap),
                  pl.BlockSpec((None, d, 2 * d), wmap), pl.BlockSpec((None, 1, 2 * d), wmap),
                  pl.BlockSpec((None, d, d), wmap), pl.BlockSpec((None, 1, d), wmap)],
        out_specs=pl.BlockSpec((MOE_BLK, d), lambda b, be, nu: (b, 0)),
        scratch_shapes=[pltpu.VMEM((d, 2 * d), BF16), pltpu.VMEM((d, d), BF16)],
    )
    return pl.pallas_call(
        _expert_kernel,
        grid_spec=grid_spec,
        out_shape=jax.ShapeDtypeStruct((cap, d), F32),
        compiler_params=_cparams(("arbitrary",)),
        name="moe_experts",
    )(block_expert, n_used, xs, w_gu, b_gu.reshape(ne, 1, 2 * d), w_dn, b_dn.reshape(ne, 1, d))


def _combine_ple_kernel(n8_ref, sbase_ref, gdst_ref, y_hbm, hm_ref, gate_ref, pos_ref, p_ref, pw_ref, pn_ref,
                        gn_ref, gw_ref, o_ref, stage, rsem):
    tile = pl.program_id(0)

    @pl.when(tile == 0)
    def _():
        stage[...] = jnp.zeros_like(stage)

    def copy(so, gd, size):
        return pltpu.make_async_copy(y_hbm.at[pl.ds(gd, size), :], stage.at[pl.ds(so, size), :], rsem)

    _for_tile_strips(n8_ref, sbase_ref, gdst_ref, tile, lambda so, gd, size: copy(so, gd, size).start())
    ple = _rms(jnp.dot(p_ref[...].astype(BF16), pw_ref[...], preferred_element_type=F32), pn_ref[...])
    lanes = lax.broadcasted_iota(I32, (MOE_TILE, STAGE_ROWS), 1)
    weights = jnp.zeros((MOE_TILE, STAGE_ROWS), F32)
    for k in range(TOP_K):
        weights = jnp.where(lanes == pos_ref[:, k:k + 1], gate_ref[:, k:k + 1], weights)
    _for_tile_strips(n8_ref, sbase_ref, gdst_ref, tile, lambda so, gd, size: copy(so, gd, size).wait())
    h2 = hm_ref[...] + jnp.dot(weights.astype(BF16), stage[...].astype(BF16), preferred_element_type=F32)
    gate = jax.nn.sigmoid(jnp.dot(_rms(h2, gn_ref[...]).astype(BF16), gw_ref[...], preferred_element_type=F32))
    o_ref[...] = h2 + ple * gate


def _combine_ple(y, pos, n8, sbase, gdst, h_mid, gates, p, ple_w, ple_norm, gate_norm, gate_w):
    n, d = h_mid.shape
    tm = MOE_TILE
    row = lambda i, *_: (i, 0)
    full = lambda shape: pl.BlockSpec(shape, lambda i, *_: (0,) * len(shape))
    grid_spec = pltpu.PrefetchScalarGridSpec(
        num_scalar_prefetch=3,
        grid=(n // tm,),
        in_specs=[pl.BlockSpec(memory_space=pl.ANY), pl.BlockSpec((tm, d), row), pl.BlockSpec((tm, TOP_K), row),
                  pl.BlockSpec((tm, TOP_K), row), pl.BlockSpec((tm, PLE_DIM), row), full((PLE_DIM, d)),
                  full((1, d)), full((1, d)), full((d, d))],
        out_specs=pl.BlockSpec((tm, d), row),
        scratch_shapes=[pltpu.VMEM((STAGE_ROWS, d), F32), pltpu.SemaphoreType.DMA],
    )
    return pl.pallas_call(
        _combine_ple_kernel,
        grid_spec=grid_spec,
        out_shape=jax.ShapeDtypeStruct((n, d), F32),
        compiler_params=_cparams(("arbitrary",)),
        name="moe_combine_ple",
    )(n8, sbase, gdst, y, h_mid, gates.T, pos.T, p, ple_w.astype(BF16), ple_norm.reshape(1, d),
      gate_norm.reshape(1, d), gate_w.astype(BF16))


def _moe_ple(act, h, w_out, g_ffn, w_r, b_r, layer, w_gu, b_gu, w_dn, b_dn, p, ple_w, ple_norm, gate_norm, gate_w):
    n, d = h.shape
    ntiles = n // MOE_TILE
    h_mid, xn, gates, pos, cnt = _outproj_router(act, h, w_out, g_ffn, w_r, b_r)
    n8 = cnt[:, 0].astype(I32).reshape(ntiles, N_EXPERTS)
    sbase = jnp.cumsum(n8, axis=1) - n8
    total = jnp.sum(n8, axis=0)
    padded = (total + MOE_BLK - 1) // MOE_BLK * MOE_BLK
    pend = jnp.cumsum(padded)
    pstart = pend - padded
    gdst = pstart[None, :] + jnp.cumsum(n8, axis=0) - n8
    nb = -(-(n * TOP_K + ntiles * N_EXPERTS * (STRIP_ALIGN - 1)) // MOE_BLK) + N_EXPERTS
    cap = nb * MOE_BLK
    block_start = jnp.arange(nb, dtype=I32) * MOE_BLK
    block_expert = jnp.minimum(jnp.sum(pend[None, :] <= block_start[:, None], axis=1), N_EXPERTS - 1).astype(I32)
    n_used = (pend[-1:] // MOE_BLK).astype(I32)
    n8f, sbf, gdf = n8.reshape(-1), sbase.reshape(-1).astype(I32), gdst.reshape(-1).astype(I32)
    xs = _dispatch(xn, pos, n8f, sbf, gdf, total.astype(I32), pstart.astype(I32), cap)
    y = _experts(xs, block_expert + layer * N_EXPERTS, n_used, w_gu, b_gu, w_dn, b_dn)
    return _combine_ple(y, pos, n8f, sbf, gdf, h_mid, gates, p, ple_w, ple_norm, gate_norm, gate_w)


def kernel(x, p, norm_mix, norm_ffn, hg_w_in, hg_w_out, hg_gnorm, hg_lb_param, fox_w_in, fox_f_bias, fox_qnorm, fox_knorm, fox_w_out, rg_w_in, rg_conv_w, rg_conv_b, rg_wa, rg_ba, rg_wx, rg_bx, rg_lambda, rg_w_out, router_w, router_b, moe_w_gu, moe_b_gu, moe_w_dn, moe_b_dn, ple_w, ple_norm, ple_gate_norm, ple_gate_w):
    batch, seq, d = x.shape
    depth = p.shape[0]
    n = batch * seq
    lb_all = jnp.cumsum(jax.nn.softmax(hg_lb_param.astype(F32), axis=0), axis=0)
    lb_all = lb_all - lb_all[0]
    h = x.reshape(n, d).astype(F32)
    ne = depth * N_EXPERTS
    w_gu = moe_w_gu.astype(F32).reshape(ne, d, 2 * d)
    b_gu = moe_b_gu.astype(F32).reshape(ne, 2 * d)
    w_dn = moe_w_dn.astype(F32).reshape(ne, d, d)
    b_dn = moe_b_dn.astype(F32).reshape(ne, d)
    for i in range(depth):
        j = i // 3
        kind = i % 3
        if kind == 0:
            act = _hgrn2_mixer(h, norm_mix[i], hg_w_in[j], None, hg_gnorm[j], lb_all[i], batch, seq)
            w_out = hg_w_out[j]
        elif kind == 1:
            act = _fox_mixer(h, norm_mix[i], fox_w_in[j], fox_f_bias[j], fox_qnorm[j], fox_knorm[j], batch, seq)
            w_out = fox_w_out[j]
        else:
            act = _rglru_mixer(h, norm_mix[i], rg_w_in[j], rg_conv_w[j], rg_conv_b[j], rg_wa[j], rg_ba[j],
                               rg_wx[j], rg_bx[j], rg_lambda[j], batch, seq)
            w_out = rg_w_out[j]
        h = _moe_ple(act, h, w_out, norm_ffn[i], router_w[i], router_b[i], i, w_gu, b_gu, w_dn, b_dn,
                     p[i].reshape(n, PLE_DIM), ple_w[i], ple_norm[i], ple_gate_norm[i], ple_gate_w[i])
    return h.reshape(batch, seq, d)
```

```python
import functools

import jax
import jax.numpy as jnp
import numpy as np
from jax import lax
from jax.experimental import pallas as pl
from jax.experimental.pallas import tpu as pltpu

F32 = jnp.float32
BF16 = jnp.bfloat16
I32 = jnp.int32

D_MODEL = 1024
EPS = 1e-6
PLE_DIM = 256

HG_HEADS = 8
HG_DK = 128
HG_CHUNK = 128
HG_LEVELS = 7

FOX_HEADS = 16
FOX_HD = 64
FOX_ZERO_EXP = -105.0

RG_BLOCKS = 4
RG_BW = 256
CONV_W = 4
RG_C = 8.0

N_EXPERTS = 32
TOP_K = 4
SWIGLU_LIMIT = 7.0
SWIGLU_ALPHA = 1.702
MOE_BLK = 512
EXPERT_CHUNK = 512
MOE_TILE = 512
STRIP_ALIGN = 8
STAGE_ROWS = -(-(TOP_K * MOE_TILE + N_EXPERTS * (STRIP_ALIGN - 1)) // 256) * 256

VMEM_LIMIT = 56 * 1024 * 1024

HI = lax.Precision.HIGHEST


def _cparams(sem):
    return pltpu.CompilerParams(dimension_semantics=sem, vmem_limit_bytes=VMEM_LIMIT)


def _bdot(a, b):
    return jnp.dot(a.astype(BF16), b.astype(BF16), preferred_element_type=F32)


def _bdot_nt(a, b):
    return lax.dot_general(a.astype(BF16), b.astype(BF16), (((1,), (1,)), ((), ())),
                           preferred_element_type=F32)


def _bdot_tn(a, b):
    return lax.dot_general(a.astype(BF16), b.astype(BF16), (((0,), (0,)), ((), ())),
                           preferred_element_type=F32)


def _rms(x, g):
    return x * lax.rsqrt(jnp.mean(x * x, axis=-1, keepdims=True) + EPS) * g


def _split2(x):
    hi = x.astype(BF16)
    lo = (x - hi.astype(F32)).astype(BF16)
    return hi, lo


def _split3(x):
    hi = x.astype(BF16)
    r = x - hi.astype(F32)
    mid = r.astype(BF16)
    lo = (r - mid.astype(F32)).astype(BF16)
    return hi, mid, lo


def _log_sigmoid(z):
    return jnp.minimum(z, 0.0) - jnp.log1p(jnp.exp(-jnp.abs(z)))


def _full(shape):
    return pl.BlockSpec(shape, lambda *_: (0,) * len(shape))


def _norm_proj_kernel(h_ref, g_ref, w_ref, o_ref, *, cn):
    xn = _rms(h_ref[...], g_ref[...]).astype(BF16)
    m = w_ref.shape[1]
    for c in range(m // cn):
        o_ref[:, c * cn:(c + 1) * cn] = jnp.dot(
            xn, w_ref[:, c * cn:(c + 1) * cn], preferred_element_type=F32).astype(o_ref.dtype)


def _norm_proj(h, g, w, tm=256, out_dtype=F32):
    n, d = h.shape
    m = w.shape[1]
    return pl.pallas_call(
        functools.partial(_norm_proj_kernel, cn=512),
        grid=(n // tm,),
        in_specs=[pl.BlockSpec((tm, d), lambda i: (i, 0)), _full((1, d)), _full((d, m))],
        out_specs=pl.BlockSpec((tm, m), lambda i: (i, 0)),
        out_shape=jax.ShapeDtypeStruct((n, m), out_dtype),
        compiler_params=_cparams(("parallel",)),
        name="norm_proj",
    )(h, g.reshape(1, d), w)


def _hgrn2_consts():
    c = HG_CHUNK
    t = np.arange(c)
    tril = (t[:, None] >= t[None, :]).astype(np.float32)
    sel = np.zeros((HG_LEVELS, c, c), np.float32)
    for l in range(HG_LEVELS):
        hs = 1 << l
        m = (t // (2 * hs)) * (2 * hs) + hs - 1
        sel[l, t, m] = 1.0
    return jnp.asarray(tril, BF16), jnp.asarray(sel.reshape(HG_LEVELS * c, c), BF16)


def _hgrn2_kernel(q_ref, z_ref, v_ref, g_ref, par_ref, tril_ref, sel_ref, o_ref, st_ref, *, nchunk):
    c = HG_CHUNK

    @pl.when(pl.program_id(2) == 0)
    def _():
        st_ref[...] = jnp.zeros_like(st_ref)

    log_lb = par_ref[0:1, :]
    log1m_lb = par_ref[1:2, :]
    one_m_lb = par_ref[2:3, :]
    gnorm = par_ref[3:4, :]
    row = lax.broadcasted_iota(I32, (c, c), 0)
    col = lax.broadcasted_iota(I32, (c, c), 1)

    cs = range(nchunk)
    rs = [slice(ci * c, (ci + 1) * c) for ci in cs]
    tril = tril_ref[...]
    q, k, vb, cum, refs, a = [], [], [], [], [], []
    for r in rs:
        qr = q_ref[r, :]
        z = z_ref[r, :]
        vb.append(v_ref[r, :].astype(BF16))
        q.append(qr * jax.nn.sigmoid(qr) * (HG_DK ** -0.5))
        b = log1m_lb + _log_sigmoid(z)
        lf = jnp.maximum(log_lb, b) + jnp.log1p(jnp.exp(-jnp.abs(log_lb - b)))
        k.append(one_m_lb * jax.nn.sigmoid(-z))
        hi, mid, lo = _split3(lf)
        cum.append(jnp.dot(tril, hi, preferred_element_type=F32)
                   + jnp.dot(tril, mid, preferred_element_type=F32)
                   + jnp.dot(tril, lo, preferred_element_type=F32))
    for i in cs:
        chi, clo = _split2(cum[i])
        refs.append(jnp.dot(sel_ref[...], chi, preferred_element_type=F32)
                    + jnp.dot(sel_ref[...], clo, preferred_element_type=F32))
        a.append(jnp.where(row == col, _bdot_nt(q[i], k[i]), 0.0))
    for l in range(HG_LEVELS):
        hs = 1 << l
        mask = (((row ^ col) >> l) == 1) & (row > col)
        for i in cs:
            ref = refs[i][l * c:(l + 1) * c, :]
            if hs >= 8:
                parts = []
                for blk in range(c // hs):
                    sl = slice(blk * hs, (blk + 1) * hs)
                    if blk % 2:
                        parts.append(q[i][sl] * jnp.exp(jnp.minimum(cum[i][sl] - ref[sl], 0.0)))
                    else:
                        parts.append(k[i][sl] * jnp.exp(jnp.minimum(ref[sl] - cum[i][sl], 0.0)))
                x = jnp.concatenate(parts, axis=0).astype(BF16)
            else:
                x = (jnp.where((row & hs) != 0, q[i], k[i]) * jnp.exp(-jnp.abs(cum[i] - ref))).astype(BF16)
            a[i] = jnp.where(mask, _bdot_nt(x, x), a[i])
    intra = [_bdot(a[i], vb[i]) for i in cs]
    last = [cum[i][c - 1:c, :] for i in cs]
    qe = [(q[i] * jnp.exp(cum[i])).astype(BF16) for i in cs]
    kd = [(k[i] * jnp.exp(last[i] - cum[i])).astype(BF16) for i in cs]
    st = st_ref[...]
    for i in cs:
        o = intra[i] + _bdot_nt(qe[i], st)
        st = st * jnp.exp(last[i]) + _bdot_tn(vb[i], kd[i])
        gt = g_ref[rs[i], :]
        y = _rms(o, gnorm) * (gt * jax.nn.sigmoid(gt))
        o_ref[rs[i], :] = y.astype(o_ref.dtype)
    st_ref[...] = st


def _hgrn2_recurrence(proj, par, batch, seq, tt=512):
    n = batch * seq
    nt = seq // tt
    tril, sel = _hgrn2_consts()

    def part(p):
        return pl.BlockSpec((tt, HG_DK), lambda b, h, i, p=p: (b * nt + i, p * HG_HEADS + h))

    return pl.pallas_call(
        functools.partial(_hgrn2_kernel, nchunk=tt // HG_CHUNK),
        grid=(batch, HG_HEADS, nt),
        in_specs=[part(0), part(1), part(2), part(3),
                  pl.BlockSpec((8, HG_DK), lambda b, h, i: (0, h)),
                  _full(tril.shape), _full(sel.shape)],
        out_specs=pl.BlockSpec((tt, HG_DK), lambda b, h, i: (b * nt + i, h)),
        out_shape=jax.ShapeDtypeStruct((n, D_MODEL), BF16),
        scratch_shapes=[pltpu.VMEM((HG_DK, HG_DK), F32)],
        compiler_params=_cparams(("parallel", "parallel", "arbitrary")),
        name="hgrn2_recurrence",
    )(proj, proj, proj, proj, par, tril, sel)


def _hgrn2_mixer(h, g_mix, w_in, w_out_unused, g_norm, lb, batch, seq):
    del w_out_unused
    proj = _norm_proj(h, g_mix, w_in.astype(BF16))
    par = jnp.zeros((8, D_MODEL), F32)
    par = par.at[0].set(jnp.log(lb)).at[1].set(jnp.log1p(-lb)).at[2].set(1.0 - lb)
    par = par.at[3].set(jnp.tile(g_norm.astype(F32), HG_HEADS))
    return _hgrn2_recurrence(proj, par, batch, seq)


def _fox_proj_kernel(h_ref, g_ref, w_ref, wf_ref, fb_ref, qg_ref, kg_ref, gs_ref, gst_ref, tril_ref,
                     o_ref, cum_ref, carry_ref, *, tiles_per_seq):
    d = D_MODEL

    @pl.when(pl.program_id(0) % tiles_per_seq == 0)
    def _():
        carry_ref[...] = jnp.zeros_like(carry_ref)

    xn = _rms(h_ref[...], g_ref[...])
    xb = xn.astype(BF16)

    def headnorm(t, gain):
        shi, slo = _split2(t * t)
        ssq = (jnp.dot(shi, gs_ref[...], preferred_element_type=F32)
               + jnp.dot(slo, gs_ref[...], preferred_element_type=F32))
        inv = lax.rsqrt(ssq * (1.0 / FOX_HD) + EPS)
        ihi, ilo = _split2(inv)
        invf = (jnp.dot(ihi, gst_ref[...], preferred_element_type=F32)
                + jnp.dot(ilo, gst_ref[...], preferred_element_type=F32))
        return t * invf * gain

    q = jnp.dot(xb, w_ref[:, 0:d], preferred_element_type=F32)
    o_ref[:, 0:d] = (headnorm(q, qg_ref[...]) * (FOX_HD ** -0.5)).astype(o_ref.dtype)
    k = jnp.dot(xb, w_ref[:, d:2 * d], preferred_element_type=F32)
    o_ref[:, d:2 * d] = headnorm(k, kg_ref[...]).astype(o_ref.dtype)
    for c in range(2, 4):
        o_ref[:, c * d:(c + 1) * d] = jnp.dot(
            xb, w_ref[:, c * d:(c + 1) * d], preferred_element_type=F32).astype(o_ref.dtype)

    fl = jnp.dot(xn, wf_ref[...], preferred_element_type=F32, precision=HI) + fb_ref[...]
    hi, mid, lo = _split3(_log_sigmoid(fl))
    tril = tril_ref[...]
    cum = (jnp.dot(tril, hi, preferred_element_type=F32)
           + jnp.dot(tril, mid, preferred_element_type=F32)
           + jnp.dot(tril, lo, preferred_element_type=F32)) + carry_ref[0:1, :]
    cum_ref[...] = cum
    tm = cum.shape[0]
    carry_ref[...] = jnp.broadcast_to(cum[tm - 1:tm, :], carry_ref.shape)


def _fox_proj(h, g_mix, w_in, f_bias, q_norm, k_norm, seq, tm=256):
    n, d = h.shape
    w_main = w_in[:, :4 * d].astype(BF16)
    w_f = jnp.zeros((d, 128), F32).at[:, :FOX_HEADS].set(w_in[:, 4 * d:].astype(F32))
    fb = jnp.zeros((1, 128), F32).at[0, :FOX_HEADS].set(f_bias.astype(F32))
    head_of = np.arange(d) // FOX_HD
    gs_np = (head_of[:, None] == np.arange(128)[None, :]).astype(np.float32)
    gs = jnp.asarray(gs_np, BF16)
    gst = jnp.asarray(gs_np.T, BF16)
    tril = jnp.asarray(np.tril(np.ones((tm, tm), np.float32)), BF16)
    qg = jnp.tile(q_norm.astype(F32), FOX_HEADS).reshape(1, d)
    kg = jnp.tile(k_norm.astype(F32), FOX_HEADS).reshape(1, d)
    return pl.pallas_call(
        functools.partial(_fox_proj_kernel, tiles_per_seq=seq // tm),
        grid=(n // tm,),
        in_specs=[pl.BlockSpec((tm, d), lambda i: (i, 0)), _full((1, d)), _full((d, 4 * d)),
                  _full((d, 128)), _full((1, 128)), _full((1, d)), _full((1, d)),
                  _full((d, 128)), _full((128, d)), _full((tm, tm))],
        out_specs=[pl.BlockSpec((tm, 4 * d), lambda i: (i, 0)),
                   pl.BlockSpec((tm, 128), lambda i: (i, 0))],
        out_shape=[jax.ShapeDtypeStruct((n, 4 * d), BF16), jax.ShapeDtypeStruct((n, 128), F32)],
        scratch_shapes=[pltpu.VMEM((8, 128), F32)],
        compiler_params=_cparams(("arbitrary",)),
        name="fox_proj",
    )(h, g_mix.reshape(1, d), w_main, w_f, fb, qg, kg, gs, gst, tril)


def _fox_attn_kernel(jlo_ref, q_ref, k_ref, v_ref, g_ref, cq_ref, ck_ref, o_ref,
                     qh_ref, m_ref, l_ref, acc_ref, *, tq, nq):
    qi = pl.program_id(2)
    lane = lax.broadcasted_iota(I32, (tq, 128), 1)
    qv = q_ref[...]
    qh_ref[0] = jnp.where(lane < FOX_HD, qv, jnp.zeros_like(qv))
    qh_ref[1] = jnp.where(lane < FOX_HD, jnp.zeros_like(qv), qv)
    m_ref[...] = jnp.full_like(m_ref, -jnp.inf)
    l_ref[...] = jnp.zeros_like(l_ref)
    acc_ref[...] = jnp.zeros_like(acc_ref)

    def sweep(kj, masked):
        rows = pl.ds(pl.multiple_of(kj * tq, tq), tq)
        kb = k_ref[rows, :]
        vb = v_ref[rows, :]
        for hh in range(2):
            s = lax.dot_general(qh_ref[hh], kb, (((1,), (1,)), ((), ())), preferred_element_type=F32)
            s = s + (cq_ref[hh:hh + 1, 0:1] - ck_ref[hh:hh + 1, rows])
            if masked:
                r = lax.broadcasted_iota(I32, s.shape, 0)
                cc = lax.broadcasted_iota(I32, s.shape, 1)
                s = jnp.where(cc <= r, s, -jnp.inf)
            m_old = m_ref[hh]
            m_new = jnp.maximum(m_old, jnp.max(s, axis=-1, keepdims=True))
            alpha = jnp.exp(m_old - m_new)
            p = jnp.exp(s - m_new[:, 0:1])
            l_ref[hh] = alpha * l_ref[hh] + jnp.sum(p, axis=-1, keepdims=True)
            acc_ref[hh] = alpha * acc_ref[hh] + jnp.dot(p.astype(BF16), vb, preferred_element_type=F32)
            m_ref[hh] = m_new

    def body(kj, carry):
        sweep(kj, False)
        return carry

    lo = jlo_ref[(pl.program_id(0) * pl.num_programs(1) + pl.program_id(1)) * nq + qi]
    lax.fori_loop(lo, qi, body, 0)
    sweep(qi, True)
    o = jnp.where(lane < FOX_HD, acc_ref[0] / l_ref[0], acc_ref[1] / l_ref[1])
    gt = g_ref[...].astype(F32)
    o_ref[...] = (o * jax.nn.sigmoid(gt)).astype(o_ref.dtype)


def _fox_first_block(cum_t, logit_bound, tq):
    bh, _, seq = cum_t.shape
    nq = seq // tq
    blk = cum_t.reshape(bh, 2, nq, tq)
    gap = blk[:, :, :, None, 0] - blk[:, :, None, :, tq - 1]
    dead = jnp.all(2.0 * logit_bound + gap < FOX_ZERO_EXP, axis=1)
    dead = dead & (jnp.arange(nq)[None, None, :] < jnp.arange(nq)[None, :, None])
    return jnp.sum(jnp.cumprod(dead.astype(I32), axis=-1), axis=-1).astype(I32).reshape(-1)


def _fox_attention(qkvg, cum, logit_bound, batch, seq, tq=512):
    n = batch * seq
    nq = seq // tq
    hp = FOX_HEADS // 2
    cum_t = cum[:, :FOX_HEADS].reshape(batch, seq, hp, 2).transpose(0, 2, 3, 1).reshape(batch * hp, 2, seq)
    jlo = _fox_first_block(cum_t, logit_bound, tq)
    grid_spec = pltpu.PrefetchScalarGridSpec(
        num_scalar_prefetch=1,
        grid=(batch, hp, nq),
        in_specs=[
            pl.BlockSpec((tq, 128), lambda b, h, i, jlo: (b * nq + i, h)),
            pl.BlockSpec((seq, 128), lambda b, h, i, jlo: (b, hp + h)),
            pl.BlockSpec((seq, 128), lambda b, h, i, jlo: (b, 2 * hp + h)),
            pl.BlockSpec((tq, 128), lambda b, h, i, jlo: (b * nq + i, 3 * hp + h)),
            pl.BlockSpec((None, 2, tq), lambda b, h, i, jlo: (b * hp + h, 0, i)),
            pl.BlockSpec((None, 2, seq), lambda b, h, i, jlo: (b * hp + h, 0, 0)),
        ],
        out_specs=pl.BlockSpec((tq, 128), lambda b, h, i, jlo: (b * nq + i, h)),
        scratch_shapes=[pltpu.VMEM((2, tq, 128), BF16), pltpu.VMEM((2, tq, 128), F32),
                        pltpu.VMEM((2, tq, 128), F32), pltpu.VMEM((2, tq, 128), F32)],
    )
    return pl.pallas_call(
        functools.partial(_fox_attn_kernel, tq=tq, nq=nq),
        grid_spec=grid_spec,
        out_shape=jax.ShapeDtypeStruct((n, D_MODEL), BF16),
        compiler_params=_cparams(("parallel", "parallel", "arbitrary")),
        name="fox_attention",
    )(jlo, qkvg, qkvg, qkvg, qkvg, cum_t, cum_t)


def _fox_mixer(h, g_mix, w_in, f_bias, q_norm, k_norm, batch, seq):
    qkvg, cum = _fox_proj(h, g_mix, w_in, f_bias, q_norm, k_norm, seq)
    logit_bound = 1.02 * FOX_HD ** 0.5 * jnp.max(jnp.abs(q_norm.astype(F32))) * jnp.max(jnp.abs(k_norm.astype(F32)))
    return _fox_attention(qkvg, cum, logit_bound, batch, seq)


def _rglru_kernel(gate_ref, u_ref, par_ref, wa_ref, wx_ref, o_ref, prev_ref, hc_ref, *, tt):
    @pl.when(pl.program_id(1) == 0)
    def _():
        prev_ref[...] = jnp.zeros_like(prev_ref)
        hc_ref[...] = jnp.zeros_like(hc_ref)

    u = u_ref[...]
    ext = jnp.concatenate([prev_ref[...], u], axis=0)
    conv = par_ref[4:5, :] + u * par_ref[3:4, :]
    for shift in range(1, CONV_W):
        conv = conv + pltpu.roll(ext, shift, 0)[8:, :] * par_ref[3 - shift:4 - shift, :]
    prev_ref[...] = u[tt - 8:, :]

    cb = conv.astype(BF16)
    ra, ia = [], []
    for nb in range(RG_BLOCKS):
        blk = cb[:, nb * RG_BW:(nb + 1) * RG_BW]
        ra.append(jnp.dot(blk, wa_ref[nb], preferred_element_type=F32))
        ia.append(jnp.dot(blk, wx_ref[nb], preferred_element_type=F32))
    r = jax.nn.sigmoid(jnp.concatenate(ra, axis=1) + par_ref[5:6, :])
    ig = jax.nn.sigmoid(jnp.concatenate(ia, axis=1) + par_ref[6:7, :])
    lam = par_ref[7:8, :]
    softplus = jnp.maximum(-lam, 0.0) + jnp.log1p(jnp.exp(-jnp.abs(lam)))
    log_a = -RG_C * r * softplus
    a = jnp.exp(log_a)
    b = jnp.sqrt(1.0 - a * a) * (ig * conv)

    row = lax.broadcasted_iota(I32, a.shape, 0)
    dist = 1
    while dist < tt:
        ok = row >= dist
        a_sh = pltpu.roll(a, dist, 0)
        b_sh = pltpu.roll(b, dist, 0)
        b = jnp.where(ok, a * b_sh + b, b)
        a = jnp.where(ok, a * a_sh, a)
        dist *= 2
    hs = b + a * hc_ref[0:1, :]
    hc_ref[...] = jnp.broadcast_to(hs[tt - 1:tt, :], hc_ref.shape)

    gt = gate_ref[...]
    gelu = 0.5 * gt * (1.0 + jnp.tanh(0.7978845608028654 * (gt + 0.044715 * gt * gt * gt)))
    o_ref[...] = (hs * gelu).astype(o_ref.dtype)


def _rglru_mixer(h, g_mix, w_in, conv_w, conv_b, w_a, b_a, w_x, b_x, lam, batch, seq, tt=256):
    n = batch * seq
    nt = seq // tt
    w = D_MODEL
    proj = _norm_proj(h, g_mix, w_in.astype(BF16))
    par = jnp.concatenate([conv_w.astype(F32), conv_b.reshape(1, w), b_a.reshape(1, w),
                           b_x.reshape(1, w), lam.reshape(1, w)], axis=0).astype(F32)
    return pl.pallas_call(
        functools.partial(_rglru_kernel, tt=tt),
        grid=(batch, nt),
        in_specs=[pl.BlockSpec((tt, w), lambda b, i: (b * nt + i, 0)),
                  pl.BlockSpec((tt, w), lambda b, i: (b * nt + i, 1)),
                  _full((8, w)), _full((RG_BLOCKS, RG_BW, RG_BW)), _full((RG_BLOCKS, RG_BW, RG_BW))],
        out_specs=pl.BlockSpec((tt, w), lambda b, i: (b * nt + i, 0)),
        out_shape=jax.ShapeDtypeStruct((n, w), BF16),
        scratch_shapes=[pltpu.VMEM((8, w), F32), pltpu.VMEM((8, w), F32)],
        compiler_params=_cparams(("parallel", "arbitrary")),
        name="rglru",
    )(proj, proj, par, w_a.astype(BF16), w_x.astype(BF16))


def _outproj_router_kernel(a_ref, h_ref, wo_ref, g_ref, wr_ref, br_ref, tri_ref, lst_ref,
                           hm_ref, xn_ref, gate_ref, pos_ref, cnt_ref):
    hm = h_ref[...] + jnp.dot(a_ref[...], wo_ref[...], preferred_element_type=F32)
    hm_ref[...] = hm
    xn = _rms(hm, g_ref[...])
    xn_ref[...] = xn.astype(xn_ref.dtype)
    logit = lax.dot_general(wr_ref[...], xn, (((1,), (1,)), ((), ())),
                            preferred_element_type=F32, precision=HI) + br_ref[:, 0:1]
    ne, tm = logit.shape
    eidx = lax.broadcasted_iota(I32, (ne, tm), 0)
    work = logit
    vals, hots = [], []
    for _ in range(TOP_K):
        mx = jnp.max(work, axis=0, keepdims=True)
        pick = jnp.min(jnp.where(work == mx, eidx, ne), axis=0, keepdims=True)
        hot = eidx == pick
        work = jnp.where(hot, -jnp.inf, work)
        vals.append(mx)
        hots.append(hot)
    ex = [jnp.exp(v - vals[0]) for v in vals]
    den = ex[0] + ex[1] + ex[2] + ex[3]
    gate_ref[...] = jnp.concatenate([e / den for e in ex], axis=0)
    tok_hot = (hots[0] | hots[1] | hots[2] | hots[3]).astype(BF16)
    before = jnp.dot(tok_hot, tri_ref[...], preferred_element_type=F32)
    count = jnp.sum(tok_hot.astype(F32), axis=1, keepdims=True)
    units = jnp.floor((count + (STRIP_ALIGN - 1)) * (1.0 / STRIP_ALIGN))
    units_b = jnp.broadcast_to(units, (ne, 128)).astype(BF16)
    start = STRIP_ALIGN * jnp.dot(lst_ref[...], units_b, preferred_element_type=F32)[:, 0:1]
    where = before + start
    pos_ref[...] = jnp.concatenate(
        [jnp.sum(jnp.where(h, where, 0.0), axis=0, keepdims=True) for h in hots], axis=0).astype(I32)
    cnt_ref[...] = jnp.broadcast_to(STRIP_ALIGN * units, cnt_ref.shape)


def _outproj_router(act, h, w_out, g_ffn, w_r, b_r):
    n, d = h.shape
    tm = MOE_TILE
    tri = jnp.asarray(np.triu(np.ones((tm, tm), np.float32), 1), BF16)
    lst = jnp.asarray(np.tril(np.ones((N_EXPERTS, N_EXPERTS), np.float32), -1), BF16)
    row = lambda i: (i, 0)
    colb = lambda i: (0, i)
    return pl.pallas_call(
        _outproj_router_kernel,
        grid=(n // tm,),
        in_specs=[pl.BlockSpec((tm, d), row), pl.BlockSpec((tm, d), row), _full((d, d)), _full((1, d)),
                  _full((N_EXPERTS, d)), _full((N_EXPERTS, 128)), _full((tm, tm)),
                  _full((N_EXPERTS, N_EXPERTS))],
        out_specs=[pl.BlockSpec((tm, d), row), pl.BlockSpec((tm, d), row),
                   pl.BlockSpec((TOP_K, tm), colb), pl.BlockSpec((TOP_K, tm), colb),
                   pl.BlockSpec((N_EXPERTS, 128), row)],
        out_shape=[jax.ShapeDtypeStruct((n, d), F32), jax.ShapeDtypeStruct((n, d), BF16),
                   jax.ShapeDtypeStruct((TOP_K, n), F32), jax.ShapeDtypeStruct((TOP_K, n), I32),
                   jax.ShapeDtypeStruct((n // tm * N_EXPERTS, 128), F32)],
        compiler_params=_cparams(("parallel",)),
        name="outproj_router",
    )(act, h, w_out.astype(BF16), g_ffn.reshape(1, d), w_r.T.astype(F32),
      jnp.broadcast_to(b_r.astype(F32)[:, None], (N_EXPERTS, 128)), tri, lst)


def _for_strip_chunks(nrows, max_rows, fn):
    size = max_rows
    while size >= STRIP_ALIGN:
        @pl.when((nrows & size) != 0)
        def _(size=size):
            fn(pl.multiple_of(nrows & ~(2 * size - 1), STRIP_ALIGN), size)
        size //= 2


def _for_tile_strips(n8_ref, sbase_ref, gdst_ref, tile, fn):
    def per_expert(e, carry):
        i = tile * N_EXPERTS + e
        so = sbase_ref[i]
        gd = gdst_ref[i]
        _for_strip_chunks(n8_ref[i], MOE_TILE,
                          lambda o, size: fn(pl.multiple_of(so + o, STRIP_ALIGN), pl.multiple_of(gd + o, STRIP_ALIGN), size))
        return carry

    lax.fori_loop(0, N_EXPERTS, per_expert, 0)


def _dispatch_kernel(n8_ref, sbase_ref, gdst_ref, tot_ref, pst_ref, x_ref, pos_ref, xs_hbm,
                     stage, zblk, rsem, zsem):
    tile = pl.program_id(0)
    rows = lax.broadcasted_iota(I32, (STAGE_ROWS, MOE_TILE), 0)
    onehot = jnp.zeros((STAGE_ROWS, MOE_TILE), F32)
    for k in range(TOP_K):
        onehot = jnp.where(rows == pos_ref[k:k + 1, :], 1.0, onehot)
    slot = tile % 2
    stage[slot] = jnp.dot(onehot.astype(BF16), x_ref[...], preferred_element_type=F32)

    def copy(sl, so, gd, size):
        return pltpu.make_async_copy(stage.at[sl, pl.ds(so, size), :], xs_hbm.at[pl.ds(gd, size), :], rsem.at[sl])

    _for_tile_strips(n8_ref, sbase_ref, gdst_ref, tile, lambda so, gd, size: copy(slot, so, gd, size).start())

    @pl.when(tile > 0)
    def _():
        _for_tile_strips(n8_ref, sbase_ref, gdst_ref, tile - 1,
                         lambda so, gd, size: copy(1 - slot, so, gd, size).wait())

    @pl.when(tile == 0)
    def _():
        zblk[...] = jnp.zeros_like(zblk)

        def zero(first, size):
            cp = pltpu.make_async_copy(zblk.at[pl.ds(0, size), :], xs_hbm.at[pl.ds(first, size), :], zsem)
            cp.start()
            cp.wait()

        def per_expert(e, carry):
            tot = tot_ref[e]
            first = pst_ref[e] + tot
            npad = ((tot + (MOE_BLK - 1)) // MOE_BLK) * MOE_BLK - tot
            _for_strip_chunks(npad, MOE_BLK // 2,
                              lambda o, size: zero(pl.multiple_of(first + o, STRIP_ALIGN), size))
            return carry

        lax.fori_loop(0, N_EXPERTS, per_expert, 0)

        last = N_EXPERTS - 1
        used = (pst_ref[last] + tot_ref[last] + (MOE_BLK - 1)) // MOE_BLK

        def ztail(b, c):
            zero(pl.multiple_of(b * MOE_BLK, MOE_BLK), MOE_BLK)
            return c

        lax.fori_loop(used, xs_hbm.shape[0] // MOE_BLK, ztail, 0)

    @pl.when(tile == pl.num_programs(0) - 1)
    def _():
        _for_tile_strips(n8_ref, sbase_ref, gdst_ref, tile, lambda so, gd, size: copy(slot, so, gd, size).wait())


def _dispatch(xn, pos, n8, sbase, gdst, total, pstart, cap):
    n, d = xn.shape
    tm = MOE_TILE
    grid_spec = pltpu.PrefetchScalarGridSpec(
        num_scalar_prefetch=5,
        grid=(n // tm,),
        in_specs=[pl.BlockSpec((tm, d), lambda i, *_: (i, 0)), pl.BlockSpec((TOP_K, tm), lambda i, *_: (0, i))],
        out_specs=pl.BlockSpec(memory_space=pl.ANY),
        scratch_shapes=[pltpu.VMEM((2, STAGE_ROWS, d), F32), pltpu.VMEM((MOE_BLK, d), F32),
                        pltpu.SemaphoreType.DMA((2,)), pltpu.SemaphoreType.DMA],
    )
    return pl.pallas_call(
        _dispatch_kernel,
        grid_spec=grid_spec,
        out_shape=jax.ShapeDtypeStruct((cap, d), F32),
        compiler_params=_cparams(("arbitrary",)),
        name="moe_dispatch",
    )(n8, sbase, gdst, total, pstart, xn, pos)


def _expert_kernel(be_ref, nu_ref, x_ref, wgu_ref, bgu_ref, wdn_ref, bdn_ref, y_ref, wgu_bf, wdn_bf):
    d = D_MODEL
    b = pl.program_id(0)

    @pl.when((b == 0) | (be_ref[b] != be_ref[jnp.maximum(b - 1, 0)]))
    def _():
        wgu_bf[...] = wgu_ref[...].astype(BF16)
        wdn_bf[...] = wdn_ref[...].astype(BF16)

    @pl.when(b < nu_ref[0])
    def _():
        xb = x_ref[...].astype(BF16)
        acc = None
        for c in range(d // EXPERT_CHUNK):
            lo, hi = c * EXPERT_CHUNK, (c + 1) * EXPERT_CHUNK
            glu = jnp.dot(xb, wgu_bf[:, lo:hi], preferred_element_type=F32) + bgu_ref[:, lo:hi]
            lin = jnp.dot(xb, wgu_bf[:, d + lo:d + hi], preferred_element_type=F32) + bgu_ref[:, d + lo:d + hi]
            glu = jnp.minimum(glu, SWIGLU_LIMIT)
            lin = jnp.clip(lin, -SWIGLU_LIMIT, SWIGLU_LIMIT)
            act = glu * jax.nn.sigmoid(SWIGLU_ALPHA * glu) * (lin + 1.0)
            part = jnp.dot(act.astype(BF16), wdn_bf[lo:hi, :], preferred_element_type=F32)
            acc = part if acc is None else acc + part
        y_ref[...] = acc + bdn_ref[...]

    @pl.when(b >= nu_ref[0])
    def _():
        y_ref[...] = jnp.zeros_like(y_ref)


def _experts(xs, block_expert, n_used, w_gu, b_gu, w_dn, b_dn):
    cap, d = xs.shape
    nb = cap // MOE_BLK
    ne = w_gu.shape[0]

    def xmap(b, be, nu):
        return (jnp.minimum(b, nu[0] - 1), 0)

    def wmap(b, be, nu):
        return (be[b], 0, 0)

    grid_spec = pltpu.PrefetchScalarGridSpec(
        num_scalar_prefetch=2,
        grid=(nb,),
        in_specs=[pl.BlockSpec((MOE_BLK, d), xmap),
                  pl.BlockSpec((None, d, 2 * d), wmap), pl.BlockSpec((None, 1, 2 * d), wmap),
                  pl.BlockSpec((None, d, d), wmap), pl.BlockSpec((None, 1, d), wmap)],
        out_specs=pl.BlockSpec((MOE_BLK, d), lambda b, be, nu: (b, 0)),
        scratch_shapes=[pltpu.VMEM((d, 2 * d), BF16), pltpu.VMEM((d, d), BF16)],
    )
    return pl.pallas_call(
        _expert_kernel,
        grid_spec=grid_spec,
        out_shape=jax.ShapeDtypeStruct((cap, d), F32),
        compiler_params=_cparams(("arbitrary",)),
        name="moe_experts",
    )(block_expert, n_used, xs, w_gu, b_gu.reshape(ne, 1, 2 * d), w_dn, b_dn.reshape(ne, 1, d))


def _combine_ple_kernel(n8_ref, sbase_ref, gdst_ref, y_hbm, hm_ref, gate_ref, pos_ref, p_ref, pw_ref, pn_ref,
                        gn_ref, gw_ref, o_ref, stage, rsem):
    tile = pl.program_id(0)
    slot = tile % 2

    def copy(sl, so, gd, size):
        return pltpu.make_async_copy(y_hbm.at[pl.ds(gd, size), :], stage.at[sl, pl.ds(so, size), :], rsem.at[sl])

    def fetch(t, sl):
        _for_tile_strips(n8_ref, sbase_ref, gdst_ref, t, lambda so, gd, size: copy(sl, so, gd, size).start())

    @pl.when(tile == 0)
    def _():
        stage[...] = jnp.zeros_like(stage)
        fetch(0, 0)

    @pl.when(tile + 1 < pl.num_programs(0))
    def _():
        fetch(tile + 1, 1 - slot)

    ple = _rms(jnp.dot(p_ref[...].astype(BF16), pw_ref[...], preferred_element_type=F32), pn_ref[...])
    lanes = lax.broadcasted_iota(I32, (MOE_TILE, STAGE_ROWS), 1)
    weights = jnp.zeros((MOE_TILE, STAGE_ROWS), F32)
    for k in range(TOP_K):
        weights = jnp.where(lanes == pos_ref[:, k:k + 1], gate_ref[:, k:k + 1], weights)
    _for_tile_strips(n8_ref, sbase_ref, gdst_ref, tile, lambda so, gd, size: copy(slot, so, gd, size).wait())
    h2 = hm_ref[...] + jnp.dot(weights.astype(BF16), stage[slot].astype(BF16), preferred_element_type=F32)
    gate = jax.nn.sigmoid(jnp.dot(_rms(h2, gn_ref[...]).astype(BF16), gw_ref[...], preferred_element_type=F32))
    o_ref[...] = h2 + ple * gate


def _combine_ple(y, pos, n8, sbase, gdst, h_mid, gates, p, ple_w, ple_norm, gate_norm, gate_w):
    n, d = h_mid.shape
    tm = MOE_TILE
    row = lambda i, *_: (i, 0)
    full = lambda shape: pl.BlockSpec(shape, lambda i, *_: (0,) * len(shape))
    grid_spec = pltpu.PrefetchScalarGridSpec(
        num_scalar_prefetch=3,
        grid=(n // tm,),
        in_specs=[pl.BlockSpec(memory_space=pl.ANY), pl.BlockSpec((tm, d), row), pl.BlockSpec((tm, TOP_K), row),
                  pl.BlockSpec((tm, TOP_K), row), pl.BlockSpec((tm, PLE_DIM), row), full((PLE_DIM, d)),
                  full((1, d)), full((1, d)), full((d, d))],
        out_specs=pl.BlockSpec((tm, d), row),
        scratch_shapes=[pltpu.VMEM((2, STAGE_ROWS, d), F32), pltpu.SemaphoreType.DMA((2,))],
    )
    return pl.pallas_call(
        _combine_ple_kernel,
        grid_spec=grid_spec,
        out_shape=jax.ShapeDtypeStruct((n, d), F32),
        compiler_params=_cparams(("arbitrary",)),
        name="moe_combine_ple",
    )(n8, sbase, gdst, y, h_mid, gates.T, pos.T, p, ple_w.astype(BF16), ple_norm.reshape(1, d),
      gate_norm.reshape(1, d), gate_w.astype(BF16))


def _moe_ple(act, h, w_out, g_ffn, w_r, b_r, layer, w_gu, b_gu, w_dn, b_dn, p, ple_w, ple_norm, gate_norm, gate_w):
    n, d = h.shape
    ntiles = n // MOE_TILE
    h_mid, xn, gates, pos, cnt = _outproj_router(act, h, w_out, g_ffn, w_r, b_r)
    n8 = cnt[:, 0].astype(I32).reshape(ntiles, N_EXPERTS)
    sbase = jnp.cumsum(n8, axis=1) - n8
    total = jnp.sum(n8, axis=0)
    padded = (total + MOE_BLK - 1) // MOE_BLK * MOE_BLK
    pend = jnp.cumsum(padded)
    pstart = pend - padded
    gdst = pstart[None, :] + jnp.cumsum(n8, axis=0) - n8
    nb = -(-(n * TOP_K + ntiles * N_EXPERTS * (STRIP_ALIGN - 1)) // MOE_BLK) + N_EXPERTS
    cap = nb * MOE_BLK
    block_start = jnp.arange(nb, dtype=I32) * MOE_BLK
    block_expert = jnp.minimum(jnp.sum(pend[None, :] <= block_start[:, None], axis=1), N_EXPERTS - 1).astype(I32)
    n_used = (pend[-1:] // MOE_BLK).astype(I32)
    n8f, sbf, gdf = n8.reshape(-1), sbase.reshape(-1).astype(I32), gdst.reshape(-1).astype(I32)
    xs = _dispatch(xn, pos, n8f, sbf, gdf, total.astype(I32), pstart.astype(I32), cap)
    y = _experts(xs, block_expert + layer * N_EXPERTS, n_used, w_gu, b_gu, w_dn, b_dn)
    return _combine_ple(y, pos, n8f, sbf, gdf, h_mid, gates, p, ple_w, ple_norm, gate_norm, gate_w)


def kernel(x, p, norm_mix, norm_ffn, hg_w_in, hg_w_out, hg_gnorm, hg_lb_param, fox_w_in, fox_f_bias, fox_qnorm, fox_knorm, fox_w_out, rg_w_in, rg_conv_w, rg_conv_b, rg_wa, rg_ba, rg_wx, rg_bx, rg_lambda, rg_w_out, router_w, router_b, moe_w_gu, moe_b_gu, moe_w_dn, moe_b_dn, ple_w, ple_norm, ple_gate_norm, ple_gate_w):
    batch, seq, d = x.shape
    depth = p.shape[0]
    n = batch * seq
    lb_all = jnp.cumsum(jax.nn.softmax(hg_lb_param.astype(F32), axis=0), axis=0)
    lb_all = lb_all - lb_all[0]
    h = x.reshape(n, d).astype(F32)
    ne = depth * N_EXPERTS
    w_gu = moe_w_gu.astype(F32).reshape(ne, d, 2 * d)
    b_gu = moe_b_gu.astype(F32).reshape(ne, 2 * d)
    w_dn = moe_w_dn.astype(F32).reshape(ne, d, d)
    b_dn = moe_b_dn.astype(F32).reshape(ne, d)
    for i in range(depth):
        j = i // 3
        kind = i % 3
        if kind == 0:
            act = _hgrn2_mixer(h, norm_mix[i], hg_w_in[j], None, hg_gnorm[j], lb_all[i], batch, seq)
            w_out = hg_w_out[j]
        elif kind == 1:
            act = _fox_mixer(h, norm_mix[i], fox_w_in[j], fox_f_bias[j], fox_qnorm[j], fox_knorm[j], batch, seq)
            w_out = fox_w_out[j]
        else:
            act = _rglru_mixer(h, norm_mix[i], rg_w_in[j], rg_conv_w[j], rg_conv_b[j], rg_wa[j], rg_ba[j],
                               rg_wx[j], rg_bx[j], rg_lambda[j], batch, seq)
            w_out = rg_w_out[j]
        h = _moe_ple(act, h, w_out, norm_ffn[i], router_w[i], router_b[i], i, w_gu, b_gu, w_dn, b_dn,
                     p[i].reshape(n, PLE_DIM), ple_w[i], ple_norm[i], ple_gate_norm[i], ple_gate_w[i])
    return h.reshape(batch, seq, d)
```

```python
import functools

import jax
import jax.numpy as jnp
import numpy as np
from jax import lax
from jax.experimental import pallas as pl
from jax.experimental.pallas import tpu as pltpu

F32 = jnp.float32
BF16 = jnp.bfloat16
I32 = jnp.int32

D_MODEL = 1024
EPS = 1e-6
PLE_DIM = 256

HG_HEADS = 8
HG_DK = 128
HG_CHUNK = 128
HG_LEVELS = 7

FOX_HEADS = 16
FOX_HD = 64
FOX_ZERO_EXP = -105.0

RG_BLOCKS = 4
RG_BW = 256
CONV_W = 4
RG_C = 8.0

N_EXPERTS = 32
TOP_K = 4
SWIGLU_LIMIT = 7.0
SWIGLU_ALPHA = 1.702
MOE_BLK = 512
EXPERT_CHUNK = 512
MOE_TILE = 512
STRIP_ALIGN = 8
STAGE_ROWS = -(-(TOP_K * MOE_TILE + N_EXPERTS * (STRIP_ALIGN - 1)) // 256) * 256

VMEM_LIMIT = 56 * 1024 * 1024

HI = lax.Precision.HIGHEST


def _cparams(sem):
    return pltpu.CompilerParams(dimension_semantics=sem, vmem_limit_bytes=VMEM_LIMIT)


def _bdot(a, b):
    return jnp.dot(a.astype(BF16), b.astype(BF16), preferred_element_type=F32)


def _bdot_nt(a, b):
    return lax.dot_general(a.astype(BF16), b.astype(BF16), (((1,), (1,)), ((), ())),
                           preferred_element_type=F32)


def _bdot_tn(a, b):
    return lax.dot_general(a.astype(BF16), b.astype(BF16), (((0,), (0,)), ((), ())),
                           preferred_element_type=F32)


def _rms(x, g):
    return x * lax.rsqrt(jnp.mean(x * x, axis=-1, keepdims=True) + EPS) * g


def _split2(x):
    hi = x.astype(BF16)
    lo = (x - hi.astype(F32)).astype(BF16)
    return hi, lo


def _split3(x):
    hi = x.astype(BF16)
    r = x - hi.astype(F32)
    mid = r.astype(BF16)
    lo = (r - mid.astype(F32)).astype(BF16)
    return hi, mid, lo


def _log_sigmoid(z):
    return jnp.minimum(z, 0.0) - jnp.log1p(jnp.exp(-jnp.abs(z)))


def _full(shape):
    return pl.BlockSpec(shape, lambda *_: (0,) * len(shape))


def _norm_proj_kernel(h_ref, g_ref, w_ref, o_ref, *, cn):
    xn = _rms(h_ref[...], g_ref[...]).astype(BF16)
    m = w_ref.shape[1]
    for c in range(m // cn):
        o_ref[:, c * cn:(c + 1) * cn] = jnp.dot(
            xn, w_ref[:, c * cn:(c + 1) * cn], preferred_element_type=F32).astype(o_ref.dtype)


def _norm_proj(h, g, w, tm=256, out_dtype=F32):
    n, d = h.shape
    m = w.shape[1]
    return pl.pallas_call(
        functools.partial(_norm_proj_kernel, cn=512),
        grid=(n // tm,),
        in_specs=[pl.BlockSpec((tm, d), lambda i: (i, 0)), _full((1, d)), _full((d, m))],
        out_specs=pl.BlockSpec((tm, m), lambda i: (i, 0)),
        out_shape=jax.ShapeDtypeStruct((n, m), out_dtype),
        compiler_params=_cparams(("parallel",)),
        name="norm_proj",
    )(h, g.reshape(1, d), w)


def _hgrn2_consts():
    c = HG_CHUNK
    t = np.arange(c)
    tril = (t[:, None] >= t[None, :]).astype(np.float32)
    sel = np.zeros((HG_LEVELS, c, c), np.float32)
    for l in range(HG_LEVELS):
        hs = 1 << l
        m = (t // (2 * hs)) * (2 * hs) + hs - 1
        sel[l, t, m] = 1.0
    return jnp.asarray(tril, BF16), jnp.asarray(sel.reshape(HG_LEVELS * c, c), BF16)


def _hgrn2_kernel(q_ref, z_ref, v_ref, g_ref, par_ref, tril_ref, sel_ref, o_ref, st_ref, *, nchunk):
    c = HG_CHUNK

    @pl.when(pl.program_id(2) == 0)
    def _():
        st_ref[...] = jnp.zeros_like(st_ref)

    log_lb = par_ref[0:1, :]
    log1m_lb = par_ref[1:2, :]
    one_m_lb = par_ref[2:3, :]
    gnorm = par_ref[3:4, :]
    row = lax.broadcasted_iota(I32, (c, c), 0)
    col = lax.broadcasted_iota(I32, (c, c), 1)

    cs = range(nchunk)
    rs = [slice(ci * c, (ci + 1) * c) for ci in cs]
    tril = tril_ref[...]
    q, k, vb, cum, refs, a = [], [], [], [], [], []
    for r in rs:
        qr = q_ref[r, :]
        z = z_ref[r, :]
        vb.append(v_ref[r, :].astype(BF16))
        q.append(qr * jax.nn.sigmoid(qr) * (HG_DK ** -0.5))
        b = log1m_lb + _log_sigmoid(z)
        lf = jnp.maximum(log_lb, b) + jnp.log1p(jnp.exp(-jnp.abs(log_lb - b)))
        k.append(one_m_lb * jax.nn.sigmoid(-z))
        hi, mid, lo = _split3(lf)
        cum.append(jnp.dot(tril, hi, preferred_element_type=F32)
                   + jnp.dot(tril, mid, preferred_element_type=F32)
                   + jnp.dot(tril, lo, preferred_element_type=F32))
    for i in cs:
        chi, clo = _split2(cum[i])
        refs.append(jnp.dot(sel_ref[...], chi, preferred_element_type=F32)
                    + jnp.dot(sel_ref[...], clo, preferred_element_type=F32))
        a.append(jnp.where(row == col, _bdot_nt(q[i], k[i]), 0.0))
    for l in range(HG_LEVELS):
        hs = 1 << l
        mask = (((row ^ col) >> l) == 1) & (row > col)
        for i in cs:
            ref = refs[i][l * c:(l + 1) * c, :]
            if hs >= 8:
                parts = []
                for blk in range(c // hs):
                    sl = slice(blk * hs, (blk + 1) * hs)
                    if blk % 2:
                        parts.append(q[i][sl] * jnp.exp(jnp.minimum(cum[i][sl] - ref[sl], 0.0)))
                    else:
                        parts.append(k[i][sl] * jnp.exp(jnp.minimum(ref[sl] - cum[i][sl], 0.0)))
                x = jnp.concatenate(parts, axis=0).astype(BF16)
            else:
                x = (jnp.where((row & hs) != 0, q[i], k[i]) * jnp.exp(-jnp.abs(cum[i] - ref))).astype(BF16)
            a[i] = jnp.where(mask, _bdot_nt(x, x), a[i])
    intra = [_bdot(a[i], vb[i]) for i in cs]
    last = [cum[i][c - 1:c, :] for i in cs]
    qe = [(q[i] * jnp.exp(cum[i])).astype(BF16) for i in cs]
    kd = [(k[i] * jnp.exp(last[i] - cum[i])).astype(BF16) for i in cs]
    st = st_ref[...]
    for i in cs:
        o = intra[i] + _bdot_nt(qe[i], st)
        st = st * jnp.exp(last[i]) + _bdot_tn(vb[i], kd[i])
        gt = g_ref[rs[i], :]
        y = _rms(o, gnorm) * (gt * jax.nn.sigmoid(gt))
        o_ref[rs[i], :] = y.astype(o_ref.dtype)
    st_ref[...] = st


def _hgrn2_recurrence(proj, par, batch, seq, tt=512):
    n = batch * seq
    nt = seq // tt
    tril, sel = _hgrn2_consts()

    def part(p):
        return pl.BlockSpec((tt, HG_DK), lambda b, h, i, p=p: (b * nt + i, p * HG_HEADS + h))

    return pl.pallas_call(
        functools.partial(_hgrn2_kernel, nchunk=tt // HG_CHUNK),
        grid=(batch, HG_HEADS, nt),
        in_specs=[part(0), part(1), part(2), part(3),
                  pl.BlockSpec((8, HG_DK), lambda b, h, i: (0, h)),
                  _full(tril.shape), _full(sel.shape)],
        out_specs=pl.BlockSpec((tt, HG_DK), lambda b, h, i: (b * nt + i, h)),
        out_shape=jax.ShapeDtypeStruct((n, D_MODEL), BF16),
        scratch_shapes=[pltpu.VMEM((HG_DK, HG_DK), F32)],
        compiler_params=_cparams(("parallel", "parallel", "arbitrary")),
        name="hgrn2_recurrence",
    )(proj, proj, proj, proj, par, tril, sel)


def _hgrn2_mixer(h, g_mix, w_in, w_out_unused, g_norm, lb, batch, seq):
    del w_out_unused
    proj = _norm_proj(h, g_mix, w_in.astype(BF16))
    par = jnp.zeros((8, D_MODEL), F32)
    par = par.at[0].set(jnp.log(lb)).at[1].set(jnp.log1p(-lb)).at[2].set(1.0 - lb)
    par = par.at[3].set(jnp.tile(g_norm.astype(F32), HG_HEADS))
    return _hgrn2_recurrence(proj, par, batch, seq)


def _fox_proj_kernel(h_ref, g_ref, w_ref, wvt_ref, wf_ref, fb_ref, qg_ref, kg_ref, gs_ref, gst_ref, tril_ref,
                     place_ref, q_ref, gate_ref, ka_ref, vt_ref, cum_ref, carry_ref, *, tiles_per_seq):
    d = D_MODEL

    @pl.when(pl.program_id(0) % tiles_per_seq == 0)
    def _():
        carry_ref[...] = jnp.zeros_like(carry_ref)

    xn = _rms(h_ref[...], g_ref[...])
    xb = xn.astype(BF16)

    def headnorm(t, gain):
        shi, slo = _split2(t * t)
        ssq = (jnp.dot(shi, gs_ref[...], preferred_element_type=F32)
               + jnp.dot(slo, gs_ref[...], preferred_element_type=F32))
        inv = lax.rsqrt(ssq * (1.0 / FOX_HD) + EPS)
        ihi, ilo = _split2(inv)
        invf = (jnp.dot(ihi, gst_ref[...], preferred_element_type=F32)
                + jnp.dot(ilo, gst_ref[...], preferred_element_type=F32))
        return t * invf * gain

    q = jnp.dot(xb, w_ref[:, 0:d], preferred_element_type=F32)
    q_ref[...] = (headnorm(q, qg_ref[...]) * (FOX_HD ** -0.5)).astype(q_ref.dtype)
    k = jnp.dot(xb, w_ref[:, d:2 * d], preferred_element_type=F32)
    kn = headnorm(k, kg_ref[...]).astype(BF16)
    gate_ref[...] = jnp.dot(xb, w_ref[:, 3 * d:4 * d], preferred_element_type=F32).astype(gate_ref.dtype)
    vt_ref[...] = lax.dot_general(wvt_ref[...], xb, (((1,), (1,)), ((), ())),
                                  preferred_element_type=F32).astype(vt_ref.dtype)

    fl = jnp.dot(xn, wf_ref[...], preferred_element_type=F32, precision=HI) + fb_ref[...]
    hi, mid, lo = _split3(_log_sigmoid(fl))
    tril = tril_ref[...]
    cum = (jnp.dot(tril, hi, preferred_element_type=F32)
           + jnp.dot(tril, mid, preferred_element_type=F32)
           + jnp.dot(tril, lo, preferred_element_type=F32)) + carry_ref[0:1, :]
    cum_ref[...] = cum
    tm = cum.shape[0]
    carry_ref[...] = jnp.broadcast_to(cum[tm - 1:tm, :], carry_ref.shape)
    nhi, nmid, nlo = _split3(-cum)
    feat = (jnp.dot(nhi, place_ref[0], preferred_element_type=F32)
            + jnp.dot(nmid, place_ref[1], preferred_element_type=F32)
            + jnp.dot(nlo, place_ref[2], preferred_element_type=F32)).astype(BF16)
    for hp in range(FOX_HEADS // 2):
        ka_ref[:, hp * 256:hp * 256 + 128] = kn[:, hp * 128:(hp + 1) * 128]
        ka_ref[:, hp * 256 + 128:(hp + 1) * 256] = feat[:, hp * 128:(hp + 1) * 128]


def _fox_proj(h, g_mix, w_in, f_bias, q_norm, k_norm, seq, tm=256):
    n, d = h.shape
    w_main = w_in[:, :4 * d].astype(BF16)
    w_vt = w_in[:, 2 * d:3 * d].T.astype(BF16)
    w_f = jnp.zeros((d, 128), F32).at[:, :FOX_HEADS].set(w_in[:, 4 * d:].astype(F32))
    fb = jnp.zeros((1, 128), F32).at[0, :FOX_HEADS].set(f_bias.astype(F32))
    head_of = np.arange(d) // FOX_HD
    gs_np = (head_of[:, None] == np.arange(128)[None, :]).astype(np.float32)
    gs = jnp.asarray(gs_np, BF16)
    gst = jnp.asarray(gs_np.T, BF16)
    tril = jnp.asarray(np.tril(np.ones((tm, tm), np.float32)), BF16)
    place_np = np.zeros((3, 128, d), np.float32)
    for hd in range(FOX_HEADS):
        for c in range(3):
            place_np[c, hd, (hd // 2) * 128 + 3 * (hd % 2) + c] = 1.0
    place = jnp.asarray(place_np, BF16)
    qg = jnp.tile(q_norm.astype(F32), FOX_HEADS).reshape(1, d)
    kg = jnp.tile(k_norm.astype(F32), FOX_HEADS).reshape(1, d)
    row = lambda i: (i, 0)
    return pl.pallas_call(
        functools.partial(_fox_proj_kernel, tiles_per_seq=seq // tm),
        grid=(n // tm,),
        in_specs=[pl.BlockSpec((tm, d), row), _full((1, d)), _full((d, 4 * d)), _full((d, d)),
                  _full((d, 128)), _full((1, 128)), _full((1, d)), _full((1, d)),
                  _full((d, 128)), _full((128, d)), _full((tm, tm)), _full((3, 128, d))],
        out_specs=[pl.BlockSpec((tm, d), row), pl.BlockSpec((tm, d), row), pl.BlockSpec((tm, 2 * d), row),
                   pl.BlockSpec((d, tm), lambda i: (0, i)), pl.BlockSpec((tm, 128), row)],
        out_shape=[jax.ShapeDtypeStruct((n, d), BF16), jax.ShapeDtypeStruct((n, d), BF16),
                   jax.ShapeDtypeStruct((n, 2 * d), BF16), jax.ShapeDtypeStruct((d, n), BF16),
                   jax.ShapeDtypeStruct((n, 128), F32)],
        scratch_shapes=[pltpu.VMEM((8, 128), F32)],
        compiler_params=_cparams(("arbitrary",)),
        name="fox_proj",
    )(h, g_mix.reshape(1, d), w_main, w_vt, w_f, fb, qg, kg, gs, gst, tril, place)


def _fox_attn_kernel(jlo_ref, q_ref, k_ref, vt_ref, g_ref, o_ref, qh_ref, m_ref, l_ref, acc_ref, *, tq, nq):
    qi = pl.program_id(2)
    lane = lax.broadcasted_iota(I32, (tq, 128), 1)
    qv = q_ref[...]
    for hh in range(2):
        own = (lane < FOX_HD) if hh == 0 else (lane >= FOX_HD)
        qh_ref[hh, :, 0:128] = jnp.where(own, qv, jnp.zeros_like(qv))
        pick = jnp.where(lane < 3 * hh, 0.0, jnp.where(lane < 3 * hh + 3, 1.0, 0.0))
        qh_ref[hh, :, 128:256] = pick.astype(BF16)
    m_ref[...] = jnp.full_like(m_ref, -jnp.inf)
    l_ref[...] = jnp.zeros_like(l_ref)
    acc_ref[...] = jnp.zeros_like(acc_ref)

    def sweep(kj, masked):
        rows = pl.ds(pl.multiple_of(kj * tq, tq), tq)
        kb = k_ref[rows, :]
        vt = vt_ref[:, rows]
        sts = [lax.dot_general(kb, qh_ref[hh], (((1,), (1,)), ((), ())), preferred_element_type=F32)
               for hh in range(2)]
        ps, alphas = [], []
        for hh in range(2):
            st = sts[hh]
            if masked:
                key = lax.broadcasted_iota(I32, st.shape, 0)
                qry = lax.broadcasted_iota(I32, st.shape, 1)
                st = jnp.where(key <= qry, st, -jnp.inf)
            m_old = m_ref[hh]
            m_new = jnp.maximum(m_old, jnp.max(st, axis=0, keepdims=True))
            alpha = jnp.exp(m_old - m_new)
            p = jnp.exp(st - m_new[0:1, :])
            l_ref[hh] = alpha * l_ref[hh] + jnp.sum(p, axis=0, keepdims=True)
            m_ref[hh] = m_new
            ps.append(p.astype(BF16))
            alphas.append(alpha[0:1, :])
        for hh in range(2):
            acc_ref[hh] = alphas[hh] * acc_ref[hh] + jnp.dot(
                vt[hh * FOX_HD:(hh + 1) * FOX_HD, :], ps[hh], preferred_element_type=F32)

    def body(kj, carry):
        sweep(kj, False)
        return carry

    lo = jlo_ref[(pl.program_id(0) * pl.num_programs(1) + pl.program_id(1)) * nq + qi]
    lax.fori_loop(lo, qi, body, 0)
    sweep(qi, True)
    ot = jnp.concatenate([acc_ref[0] / l_ref[0][0:1, :], acc_ref[1] / l_ref[1][0:1, :]], axis=0)
    gt = g_ref[...].astype(F32)
    o_ref[...] = (ot.T * jax.nn.sigmoid(gt)).astype(o_ref.dtype)


def _fox_first_block(cum_t, logit_bound, tq):
    bh, _, seq = cum_t.shape
    nq = seq // tq
    blk = cum_t.reshape(bh, 2, nq, tq)
    gap = blk[:, :, :, None, 0] - blk[:, :, None, :, tq - 1]
    dead = jnp.all(2.0 * logit_bound + gap < FOX_ZERO_EXP, axis=1)
    dead = dead & (jnp.arange(nq)[None, None, :] < jnp.arange(nq)[None, :, None])
    return jnp.sum(jnp.cumprod(dead.astype(I32), axis=-1), axis=-1).astype(I32).reshape(-1)


def _fox_attention(q, gate, ka, vt, cum, logit_bound, batch, seq, tq=512):
    n = batch * seq
    nq = seq // tq
    hp = FOX_HEADS // 2
    cum_t = cum[:, :FOX_HEADS].reshape(batch, seq, hp, 2).transpose(0, 2, 3, 1).reshape(batch * hp, 2, seq)
    jlo = _fox_first_block(cum_t, logit_bound, tq)
    grid_spec = pltpu.PrefetchScalarGridSpec(
        num_scalar_prefetch=1,
        grid=(batch, hp, nq),
        in_specs=[
            pl.BlockSpec((tq, 128), lambda b, h, i, jlo: (b * nq + i, h)),
            pl.BlockSpec((seq, 256), lambda b, h, i, jlo: (b, h)),
            pl.BlockSpec((128, seq), lambda b, h, i, jlo: (h, b)),
            pl.BlockSpec((tq, 128), lambda b, h, i, jlo: (b * nq + i, h)),
        ],
        out_specs=pl.BlockSpec((tq, 128), lambda b, h, i, jlo: (b * nq + i, h)),
        scratch_shapes=[pltpu.VMEM((2, tq, 256), BF16), pltpu.VMEM((2, 8, tq), F32),
                        pltpu.VMEM((2, 8, tq), F32), pltpu.VMEM((2, FOX_HD, tq), F32)],
    )
    return pl.pallas_call(
        functools.partial(_fox_attn_kernel, tq=tq, nq=nq),
        grid_spec=grid_spec,
        out_shape=jax.ShapeDtypeStruct((n, D_MODEL), BF16),
        compiler_params=_cparams(("parallel", "parallel", "arbitrary")),
        name="fox_attention",
    )(jlo, q, ka, vt, gate)


def _fox_mixer(h, g_mix, w_in, f_bias, q_norm, k_norm, batch, seq):
    q, gate, ka, vt, cum = _fox_proj(h, g_mix, w_in, f_bias, q_norm, k_norm, seq)
    logit_bound = 1.02 * FOX_HD ** 0.5 * jnp.max(jnp.abs(q_norm.astype(F32))) * jnp.max(jnp.abs(k_norm.astype(F32)))
    return _fox_attention(q, gate, ka, vt, cum, logit_bound, batch, seq)


def _rglru_kernel(gate_ref, u_ref, par_ref, wa_ref, wx_ref, o_ref, prev_ref, hc_ref, *, tt):
    @pl.when(pl.program_id(1) == 0)
    def _():
        prev_ref[...] = jnp.zeros_like(prev_ref)
        hc_ref[...] = jnp.zeros_like(hc_ref)

    u = u_ref[...]
    ext = jnp.concatenate([prev_ref[...], u], axis=0)
    conv = par_ref[4:5, :] + u * par_ref[3:4, :]
    for shift in range(1, CONV_W):
        conv = conv + pltpu.roll(ext, shift, 0)[8:, :] * par_ref[3 - shift:4 - shift, :]
    prev_ref[...] = u[tt - 8:, :]

    cb = conv.astype(BF16)
    ra, ia = [], []
    for nb in range(RG_BLOCKS):
        blk = cb[:, nb * RG_BW:(nb + 1) * RG_BW]
        ra.append(jnp.dot(blk, wa_ref[nb], preferred_element_type=F32))
        ia.append(jnp.dot(blk, wx_ref[nb], preferred_element_type=F32))
    r = jax.nn.sigmoid(jnp.concatenate(ra, axis=1) + par_ref[5:6, :])
    ig = jax.nn.sigmoid(jnp.concatenate(ia, axis=1) + par_ref[6:7, :])
    lam = par_ref[7:8, :]
    softplus = jnp.maximum(-lam, 0.0) + jnp.log1p(jnp.exp(-jnp.abs(lam)))
    log_a = -RG_C * r * softplus
    a = jnp.exp(log_a)
    b = jnp.sqrt(1.0 - a * a) * (ig * conv)

    row = lax.broadcasted_iota(I32, a.shape, 0)
    dist = 1
    while dist < tt:
        ok = row >= dist
        a_sh = pltpu.roll(a, dist, 0)
        b_sh = pltpu.roll(b, dist, 0)
        b = jnp.where(ok, a * b_sh + b, b)
        a = jnp.where(ok, a * a_sh, a)
        dist *= 2
    hs = b + a * hc_ref[0:1, :]
    hc_ref[...] = jnp.broadcast_to(hs[tt - 1:tt, :], hc_ref.shape)

    gt = gate_ref[...]
    gelu = 0.5 * gt * (1.0 + jnp.tanh(0.7978845608028654 * (gt + 0.044715 * gt * gt * gt)))
    o_ref[...] = (hs * gelu).astype(o_ref.dtype)


def _rglru_mixer(h, g_mix, w_in, conv_w, conv_b, w_a, b_a, w_x, b_x, lam, batch, seq, tt=256):
    n = batch * seq
    nt = seq // tt
    w = D_MODEL
    proj = _norm_proj(h, g_mix, w_in.astype(BF16))
    par = jnp.concatenate([conv_w.astype(F32), conv_b.reshape(1, w), b_a.reshape(1, w),
                           b_x.reshape(1, w), lam.reshape(1, w)], axis=0).astype(F32)
    return pl.pallas_call(
        functools.partial(_rglru_kernel, tt=tt),
        grid=(batch, nt),
        in_specs=[pl.BlockSpec((tt, w), lambda b, i: (b * nt + i, 0)),
                  pl.BlockSpec((tt, w), lambda b, i: (b * nt + i, 1)),
                  _full((8, w)), _full((RG_BLOCKS, RG_BW, RG_BW)), _full((RG_BLOCKS, RG_BW, RG_BW))],
        out_specs=pl.BlockSpec((tt, w), lambda b, i: (b * nt + i, 0)),
        out_shape=jax.ShapeDtypeStruct((n, w), BF16),
        scratch_shapes=[pltpu.VMEM((8, w), F32), pltpu.VMEM((8, w), F32)],
        compiler_params=_cparams(("parallel", "arbitrary")),
        name="rglru",
    )(proj, proj, par, w_a.astype(BF16), w_x.astype(BF16))


def _outproj_router_kernel(a_ref, h_ref, wo_ref, g_ref, wr_ref, br_ref, tri_ref, lst_ref,
                           hm_ref, xn_ref, gate_ref, pos_ref, cnt_ref):
    hm = h_ref[...] + jnp.dot(a_ref[...], wo_ref[...], preferred_element_type=F32)
    hm_ref[...] = hm
    xn = _rms(hm, g_ref[...])
    xn_ref[...] = xn.astype(xn_ref.dtype)
    logit = lax.dot_general(wr_ref[...], xn, (((1,), (1,)), ((), ())),
                            preferred_element_type=F32, precision=HI) + br_ref[:, 0:1]
    ne, tm = logit.shape
    eidx = lax.broadcasted_iota(I32, (ne, tm), 0)
    work = logit
    vals, hots = [], []
    for _ in range(TOP_K):
        mx = jnp.max(work, axis=0, keepdims=True)
        pick = jnp.min(jnp.where(work == mx, eidx, ne), axis=0, keepdims=True)
        hot = eidx == pick
        work = jnp.where(hot, -jnp.inf, work)
        vals.append(mx)
        hots.append(hot)
    ex = [jnp.exp(v - vals[0]) for v in vals]
    den = ex[0] + ex[1] + ex[2] + ex[3]
    gate_ref[...] = jnp.concatenate([e / den for e in ex], axis=0)
    tok_hot = (hots[0] | hots[1] | hots[2] | hots[3]).astype(BF16)
    before = jnp.dot(tok_hot, tri_ref[...], preferred_element_type=F32)
    count = jnp.sum(tok_hot.astype(F32), axis=1, keepdims=True)
    units = jnp.floor((count + (STRIP_ALIGN - 1)) * (1.0 / STRIP_ALIGN))
    units_b = jnp.broadcast_to(units, (ne, 128)).astype(BF16)
    start = STRIP_ALIGN * jnp.dot(lst_ref[...], units_b, preferred_element_type=F32)[:, 0:1]
    where = before + start
    pos_ref[...] = jnp.concatenate(
        [jnp.sum(jnp.where(h, where, 0.0), axis=0, keepdims=True) for h in hots], axis=0).astype(I32)
    cnt_ref[...] = jnp.broadcast_to(STRIP_ALIGN * units, cnt_ref.shape)


def _outproj_router(act, h, w_out, g_ffn, w_r, b_r):
    n, d = h.shape
    tm = MOE_TILE
    tri = jnp.asarray(np.triu(np.ones((tm, tm), np.float32), 1), BF16)
    lst = jnp.asarray(np.tril(np.ones((N_EXPERTS, N_EXPERTS), np.float32), -1), BF16)
    row = lambda i: (i, 0)
    colb = lambda i: (0, i)
    return pl.pallas_call(
        _outproj_router_kernel,
        grid=(n // tm,),
        in_specs=[pl.BlockSpec((tm, d), row), pl.BlockSpec((tm, d), row), _full((d, d)), _full((1, d)),
                  _full((N_EXPERTS, d)), _full((N_EXPERTS, 128)), _full((tm, tm)),
                  _full((N_EXPERTS, N_EXPERTS))],
        out_specs=[pl.BlockSpec((tm, d), row), pl.BlockSpec((tm, d), row),
                   pl.BlockSpec((TOP_K, tm), colb), pl.BlockSpec((TOP_K, tm), colb),
                   pl.BlockSpec((N_EXPERTS, 128), row)],
        out_shape=[jax.ShapeDtypeStruct((n, d), F32), jax.ShapeDtypeStruct((n, d), BF16),
                   jax.ShapeDtypeStruct((TOP_K, n), F32), jax.ShapeDtypeStruct((TOP_K, n), I32),
                   jax.ShapeDtypeStruct((n // tm * N_EXPERTS, 128), F32)],
        compiler_params=_cparams(("parallel",)),
        name="outproj_router",
    )(act, h, w_out.astype(BF16), g_ffn.reshape(1, d), w_r.T.astype(F32),
      jnp.broadcast_to(b_r.astype(F32)[:, None], (N_EXPERTS, 128)), tri, lst)


def _for_strip_chunks(nrows, max_rows, fn):
    size = max_rows
    while size >= STRIP_ALIGN:
        @pl.when((nrows & size) != 0)
        def _(size=size):
            fn(pl.multiple_of(nrows & ~(2 * size - 1), STRIP_ALIGN), size)
        size //= 2


def _for_tile_strips(n8_ref, sbase_ref, gdst_ref, tile, fn):
    def per_expert(e, carry):
        i = tile * N_EXPERTS + e
        so = sbase_ref[i]
        gd = gdst_ref[i]
        _for_strip_chunks(n8_ref[i], MOE_TILE,
                          lambda o, size: fn(pl.multiple_of(so + o, STRIP_ALIGN), pl.multiple_of(gd + o, STRIP_ALIGN), size))
        return carry

    lax.fori_loop(0, N_EXPERTS, per_expert, 0)


def _dispatch_kernel(n8_ref, sbase_ref, gdst_ref, tot_ref, pst_ref, x_ref, pos_ref, xs_hbm,
                     stage, zblk, rsem, zsem):
    tile = pl.program_id(0)
    rows = lax.broadcasted_iota(I32, (STAGE_ROWS, MOE_TILE), 0)
    onehot = jnp.zeros((STAGE_ROWS, MOE_TILE), F32)
    for k in range(TOP_K):
        onehot = jnp.where(rows == pos_ref[k:k + 1, :], 1.0, onehot)
    slot = tile % 2
    stage[slot] = jnp.dot(onehot.astype(BF16), x_ref[...], preferred_element_type=F32)

    def copy(sl, so, gd, size):
        return pltpu.make_async_copy(stage.at[sl, pl.ds(so, size), :], xs_hbm.at[pl.ds(gd, size), :], rsem.at[sl])

    _for_tile_strips(n8_ref, sbase_ref, gdst_ref, tile, lambda so, gd, size: copy(slot, so, gd, size).start())

    @pl.when(tile > 0)
    def _():
        _for_tile_strips(n8_ref, sbase_ref, gdst_ref, tile - 1,
                         lambda so, gd, size: copy(1 - slot, so, gd, size).wait())

    @pl.when(tile == 0)
    def _():
        zblk[...] = jnp.zeros_like(zblk)

        def zero(first, size):
            cp = pltpu.make_async_copy(zblk.at[pl.ds(0, size), :], xs_hbm.at[pl.ds(first, size), :], zsem)
            cp.start()
            cp.wait()

        def per_expert(e, carry):
            tot = tot_ref[e]
            first = pst_ref[e] + tot
            npad = ((tot + (MOE_BLK - 1)) // MOE_BLK) * MOE_BLK - tot
            _for_strip_chunks(npad, MOE_BLK // 2,
                              lambda o, size: zero(pl.multiple_of(first + o, STRIP_ALIGN), size))
            return carry

        lax.fori_loop(0, N_EXPERTS, per_expert, 0)

        last = N_EXPERTS - 1
        used = (pst_ref[last] + tot_ref[last] + (MOE_BLK - 1)) // MOE_BLK

        def ztail(b, c):
            zero(pl.multiple_of(b * MOE_BLK, MOE_BLK), MOE_BLK)
            return c

        lax.fori_loop(used, xs_hbm.shape[0] // MOE_BLK, ztail, 0)

    @pl.when(tile == pl.num_programs(0) - 1)
    def _():
        _for_tile_strips(n8_ref, sbase_ref, gdst_ref, tile, lambda so, gd, size: copy(slot, so, gd, size).wait())


def _dispatch(xn, pos, n8, sbase, gdst, total, pstart, cap):
    n, d = xn.shape
    tm = MOE_TILE
    grid_spec = pltpu.PrefetchScalarGridSpec(
        num_scalar_prefetch=5,
        grid=(n // tm,),
        in_specs=[pl.BlockSpec((tm, d), lambda i, *_: (i, 0)), pl.BlockSpec((TOP_K, tm), lambda i, *_: (0, i))],
        out_specs=pl.BlockSpec(memory_space=pl.ANY),
        scratch_shapes=[pltpu.VMEM((2, STAGE_ROWS, d), F32), pltpu.VMEM((MOE_BLK, d), F32),
                        pltpu.SemaphoreType.DMA((2,)), pltpu.SemaphoreType.DMA],
    )
    return pl.pallas_call(
        _dispatch_kernel,
        grid_spec=grid_spec,
        out_shape=jax.ShapeDtypeStruct((cap, d), F32),
        compiler_params=_cparams(("arbitrary",)),
        name="moe_dispatch",
    )(n8, sbase, gdst, total, pstart, xn, pos)


def _expert_kernel(be_ref, nu_ref, x_ref, wgu_ref, bgu_ref, wdn_ref, bdn_ref, y_ref, wgu_bf, wdn_bf):
    d = D_MODEL
    b = pl.program_id(0)

    @pl.when((b == 0) | (be_ref[b] != be_ref[jnp.maximum(b - 1, 0)]))
    def _():
        wgu_bf[...] = wgu_ref[...].astype(BF16)
        wdn_bf[...] = wdn_ref[...].astype(BF16)

    @pl.when(b < nu_ref[0])
    def _():
        xb = x_ref[...].astype(BF16)
        acc = None
        for c in range(d // EXPERT_CHUNK):
            lo, hi = c * EXPERT_CHUNK, (c + 1) * EXPERT_CHUNK
            glu = jnp.dot(xb, wgu_bf[:, lo:hi], preferred_element_type=F32) + bgu_ref[:, lo:hi]
            lin = jnp.dot(xb, wgu_bf[:, d + lo:d + hi], preferred_element_type=F32) + bgu_ref[:, d + lo:d + hi]
            glu = jnp.minimum(glu, SWIGLU_LIMIT)
            lin = jnp.clip(lin, -SWIGLU_LIMIT, SWIGLU_LIMIT)
            act = glu * jax.nn.sigmoid(SWIGLU_ALPHA * glu) * (lin + 1.0)
            part = jnp.dot(act.astype(BF16), wdn_bf[lo:hi, :], preferred_element_type=F32)
            acc = part if acc is None else acc + part
        y_ref[...] = acc + bdn_ref[...]

    @pl.when(b >= nu_ref[0])
    def _():
        y_ref[...] = jnp.zeros_like(y_ref)


def _experts(xs, block_expert, n_used, w_gu, b_gu, w_dn, b_dn):
    cap, d = xs.shape
    nb = cap // MOE_BLK
    ne = w_gu.shape[0]

    def xmap(b, be, nu):
        return (jnp.minimum(b, nu[0] - 1), 0)

    def wmap(b, be, nu):
        return (be[b], 0, 0)

    grid_spec = pltpu.PrefetchScalarGridSpec(
        num_scalar_prefetch=2,
        grid=(nb,),
        in_specs=[pl.BlockSpec((MOE_BLK, d), xmap),
                  pl.BlockSpec((None, d, 2 * d), wmap), pl.BlockSpec((None, 1, 2 * d), wmap),
                  pl.BlockSpec((None, d, d), wmap), pl.BlockSpec((None, 1, d), wmap)],
        out_specs=pl.BlockSpec((MOE_BLK, d), lambda b, be, nu: (b, 0)),
        scratch_shapes=[pltpu.VMEM((d, 2 * d), BF16), pltpu.VMEM((d, d), BF16)],
    )
    return pl.pallas_call(
        _expert_kernel,
        grid_spec=grid_spec,
        out_shape=jax.ShapeDtypeStruct((cap, d), F32),
        compiler_params=_cparams(("arbitrary",)),
        name="moe_experts",
    )(block_expert, n_used, xs, w_gu, b_gu.reshape(ne, 1, 2 * d), w_dn, b_dn.reshape(ne, 1, d))


def _combine_ple_kernel(n8_ref, sbase_ref, gdst_ref, y_hbm, hm_ref, gate_ref, pos_ref, p_ref, pw_ref, pn_ref,
                        gn_ref, gw_ref, o_ref, stage, rsem):
    tile = pl.program_id(0)
    slot = tile % 2

    def copy(sl, so, gd, size):
        return pltpu.make_async_copy(y_hbm.at[pl.ds(gd, size), :], stage.at[sl, pl.ds(so, size), :], rsem.at[sl])

    def fetch(t, sl):
        _for_tile_strips(n8_ref, sbase_ref, gdst_ref, t, lambda so, gd, size: copy(sl, so, gd, size).start())

    @pl.when(tile == 0)
    def _():
        stage[...] = jnp.zeros_like(stage)
        fetch(0, 0)

    @pl.when(tile + 1 < pl.num_programs(0))
    def _():
        fetch(tile + 1, 1 - slot)

    ple = _rms(jnp.dot(p_ref[...].astype(BF16), pw_ref[...], preferred_element_type=F32), pn_ref[...])
    lanes = lax.broadcasted_iota(I32, (MOE_TILE, STAGE_ROWS), 1)
    weights = jnp.zeros((MOE_TILE, STAGE_ROWS), F32)
    for k in range(TOP_K):
        weights = jnp.where(lanes == pos_ref[:, k:k + 1], gate_ref[:, k:k + 1], weights)
    _for_tile_strips(n8_ref, sbase_ref, gdst_ref, tile, lambda so, gd, size: copy(slot, so, gd, size).wait())
    h2 = hm_ref[...] + jnp.dot(weights.astype(BF16), stage[slot].astype(BF16), preferred_element_type=F32)
    gate = jax.nn.sigmoid(jnp.dot(_rms(h2, gn_ref[...]).astype(BF16), gw_ref[...], preferred_element_type=F32))
    o_ref[...] = h2 + ple * gate


def _combine_ple(y, pos, n8, sbase, gdst, h_mid, gates, p, ple_w, ple_norm, gate_norm, gate_w):
    n, d = h_mid.shape
    tm = MOE_TILE
    row = lambda i, *_: (i, 0)
    full = lambda shape: pl.BlockSpec(shape, lambda i, *_: (0,) * len(shape))
    grid_spec = pltpu.PrefetchScalarGridSpec(
        num_scalar_prefetch=3,
        grid=(n // tm,),
        in_specs=[pl.BlockSpec(memory_space=pl.ANY), pl.BlockSpec((tm, d), row), pl.BlockSpec((tm, TOP_K), row),
                  pl.BlockSpec((tm, TOP_K), row), pl.BlockSpec((tm, PLE_DIM), row), full((PLE_DIM, d)),
                  full((1, d)), full((1, d)), full((d, d))],
        out_specs=pl.BlockSpec((tm, d), row),
        scratch_shapes=[pltpu.VMEM((2, STAGE_ROWS, d), F32), pltpu.SemaphoreType.DMA((2,))],
    )
    return pl.pallas_call(
        _combine_ple_kernel,
        grid_spec=grid_spec,
        out_shape=jax.ShapeDtypeStruct((n, d), F32),
        compiler_params=_cparams(("arbitrary",)),
        name="moe_combine_ple",
    )(n8, sbase, gdst, y, h_mid, gates.T, pos.T, p, ple_w.astype(BF16), ple_norm.reshape(1, d),
      gate_norm.reshape(1, d), gate_w.astype(BF16))


def _moe_ple(act, h, w_out, g_ffn, w_r, b_r, layer, w_gu, b_gu, w_dn, b_dn, p, ple_w, ple_norm, gate_norm, gate_w):
    n, d = h.shape
    ntiles = n // MOE_TILE
    h_mid, xn, gates, pos, cnt = _outproj_router(act, h, w_out, g_ffn, w_r, b_r)
    n8 = cnt[:, 0].astype(I32).reshape(ntiles, N_EXPERTS)
    sbase = jnp.cumsum(n8, axis=1) - n8
    total = jnp.sum(n8, axis=0)
    padded = (total + MOE_BLK - 1) // MOE_BLK * MOE_BLK
    pend = jnp.cumsum(padded)
    pstart = pend - padded
    gdst = pstart[None, :] + jnp.cumsum(n8, axis=0) - n8
    nb = -(-(n * TOP_K + ntiles * N_EXPERTS * (STRIP_ALIGN - 1)) // MOE_BLK) + N_EXPERTS
    cap = nb * MOE_BLK
    block_start = jnp.arange(nb, dtype=I32) * MOE_BLK
    block_expert = jnp.minimum(jnp.sum(pend[None, :] <= block_start[:, None], axis=1), N_EXPERTS - 1).astype(I32)
    n_used = (pend[-1:] // MOE_BLK).astype(I32)
    n8f, sbf, gdf = n8.reshape(-1), sbase.reshape(-1).astype(I32), gdst.reshape(-1).astype(I32)
    xs = _dispatch(xn, pos, n8f, sbf, gdf, total.astype(I32), pstart.astype(I32), cap)
    y = _experts(xs, block_expert + layer * N_EXPERTS, n_used, w_gu, b_gu, w_dn, b_dn)
    return _combine_ple(y, pos, n8f, sbf, gdf, h_mid, gates, p, ple_w, ple_norm, gate_norm, gate_w)


def kernel(x, p, norm_mix, norm_ffn, hg_w_in, hg_w_out, hg_gnorm, hg_lb_param, fox_w_in, fox_f_bias, fox_qnorm, fox_knorm, fox_w_out, rg_w_in, rg_conv_w, rg_conv_b, rg_wa, rg_ba, rg_wx, rg_bx, rg_lambda, rg_w_out, router_w, router_b, moe_w_gu, moe_b_gu, moe_w_dn, moe_b_dn, ple_w, ple_norm, ple_gate_norm, ple_gate_w):
    batch, seq, d = x.shape
    depth = p.shape[0]
    n = batch * seq
    lb_all = jnp.cumsum(jax.nn.softmax(hg_lb_param.astype(F32), axis=0), axis=0)
    lb_all = lb_all - lb_all[0]
    h = x.reshape(n, d).astype(F32)
    ne = depth * N_EXPERTS
    w_gu = moe_w_gu.astype(F32).reshape(ne, d, 2 * d)
    b_gu = moe_b_gu.astype(F32).reshape(ne, 2 * d)
    w_dn = moe_w_dn.astype(F32).reshape(ne, d, d)
    b_dn = moe_b_dn.astype(F32).reshape(ne, d)
    for i in range(depth):
        j = i // 3
        kind = i % 3
        if kind == 0:
            act = _hgrn2_mixer(h, norm_mix[i], hg_w_in[j], None, hg_gnorm[j], lb_all[i], batch, seq)
            w_out = hg_w_out[j]
        elif kind == 1:
            act = _fox_mixer(h, norm_mix[i], fox_w_in[j], fox_f_bias[j], fox_qnorm[j], fox_knorm[j], batch, seq)
            w_out = fox_w_out[j]
        else:
            act = _rglru_mixer(h, norm_mix[i], rg_w_in[j], rg_conv_w[j], rg_conv_b[j], rg_wa[j], rg_ba[j],
                               rg_wx[j], rg_bx[j], rg_lambda[j], batch, seq)
            w_out = rg_w_out[j]
        h = _moe_ple(act, h, w_out, norm_ffn[i], router_w[i], router_b[i], i, w_gu, b_gu, w_dn, b_dn,
                     p[i].reshape(n, PLE_DIM), ple_w[i], ple_norm[i], ple_gate_norm[i], ple_gate_w[i])
    return h.reshape(batch, seq, d)
```

```python
import functools

import jax
import jax.numpy as jnp
import numpy as np
from jax import lax
from jax.experimental import pallas as pl
from jax.experimental.pallas import tpu as pltpu

F32 = jnp.float32
BF16 = jnp.bfloat16
I32 = jnp.int32

D_MODEL = 1024
EPS = 1e-6
PLE_DIM = 256

HG_HEADS = 8
HG_DK = 128
HG_CHUNK = 128
HG_LEVELS = 7

FOX_HEADS = 16
FOX_HD = 64
FOX_ZERO_EXP = -105.0

RG_BLOCKS = 4
RG_BW = 256
CONV_W = 4
RG_C = 8.0

N_EXPERTS = 32
TOP_K = 4
SWIGLU_LIMIT = 7.0
SWIGLU_ALPHA = 1.702
MOE_BLK = 512
EXPERT_CHUNK = 512
MOE_TILE = 512
STRIP_ALIGN = 8
STAGE_ROWS = -(-(TOP_K * MOE_TILE + N_EXPERTS * (STRIP_ALIGN - 1)) // 256) * 256
ONEHOT_CHUNK = 256
COMBINE_CHUNK = STAGE_ROWS // 3

VMEM_LIMIT = 56 * 1024 * 1024

HI = lax.Precision.HIGHEST


def _cparams(sem):
    return pltpu.CompilerParams(dimension_semantics=sem, vmem_limit_bytes=VMEM_LIMIT)


def _bdot(a, b):
    return jnp.dot(a.astype(BF16), b.astype(BF16), preferred_element_type=F32)


def _bdot_nt(a, b):
    return lax.dot_general(a.astype(BF16), b.astype(BF16), (((1,), (1,)), ((), ())),
                           preferred_element_type=F32)


def _bdot_tn(a, b):
    return lax.dot_general(a.astype(BF16), b.astype(BF16), (((0,), (0,)), ((), ())),
                           preferred_element_type=F32)


def _rms(x, g):
    return x * lax.rsqrt(jnp.mean(x * x, axis=-1, keepdims=True) + EPS) * g


def _split2(x):
    hi = x.astype(BF16)
    lo = (x - hi.astype(F32)).astype(BF16)
    return hi, lo


def _split3(x):
    hi = x.astype(BF16)
    r = x - hi.astype(F32)
    mid = r.astype(BF16)
    lo = (r - mid.astype(F32)).astype(BF16)
    return hi, mid, lo


def _log_sigmoid(z):
    return jnp.minimum(z, 0.0) - jnp.log1p(jnp.exp(-jnp.abs(z)))


def _full(shape):
    return pl.BlockSpec(shape, lambda *_: (0,) * len(shape))


def _norm_proj_kernel(h_ref, g_ref, w_ref, o_ref, *, cn):
    xn = _rms(h_ref[...], g_ref[...]).astype(BF16)
    m = w_ref.shape[1]
    for c in range(m // cn):
        o_ref[:, c * cn:(c + 1) * cn] = jnp.dot(
            xn, w_ref[:, c * cn:(c + 1) * cn], preferred_element_type=F32).astype(o_ref.dtype)


def _norm_proj(h, g, w, tm=256, out_dtype=F32):
    n, d = h.shape
    m = w.shape[1]
    return pl.pallas_call(
        functools.partial(_norm_proj_kernel, cn=512),
        grid=(n // tm,),
        in_specs=[pl.BlockSpec((tm, d), lambda i: (i, 0)), _full((1, d)), _full((d, m))],
        out_specs=pl.BlockSpec((tm, m), lambda i: (i, 0)),
        out_shape=jax.ShapeDtypeStruct((n, m), out_dtype),
        compiler_params=_cparams(("parallel",)),
        name="norm_proj",
    )(h, g.reshape(1, d), w)


def _hgrn2_consts():
    c = HG_CHUNK
    t = np.arange(c)
    tril = (t[:, None] >= t[None, :]).astype(np.float32)
    sel = np.zeros((HG_LEVELS, c, c), np.float32)
    for l in range(HG_LEVELS):
        hs = 1 << l
        m = (t // (2 * hs)) * (2 * hs) + hs - 1
        sel[l, t, m] = 1.0
    return jnp.asarray(tril, BF16), jnp.asarray(sel.reshape(HG_LEVELS * c, c), BF16)


def _hgrn2_kernel(q_ref, z_ref, v_ref, g_ref, par_ref, tril_ref, sel_ref, o_ref, st_ref, *, nchunk):
    c = HG_CHUNK

    @pl.when(pl.program_id(2) == 0)
    def _():
        st_ref[...] = jnp.zeros_like(st_ref)

    log_lb = par_ref[0:1, :]
    log1m_lb = par_ref[1:2, :]
    one_m_lb = par_ref[2:3, :]
    gnorm = par_ref[3:4, :]
    row = lax.broadcasted_iota(I32, (c, c), 0)
    col = lax.broadcasted_iota(I32, (c, c), 1)

    cs = range(nchunk)
    rs = [slice(ci * c, (ci + 1) * c) for ci in cs]
    tril = tril_ref[...]
    q, k, vb, cum, refs, a = [], [], [], [], [], []
    for r in rs:
        qr = q_ref[r, :]
        z = z_ref[r, :]
        vb.append(v_ref[r, :].astype(BF16))
        q.append(qr * jax.nn.sigmoid(qr) * (HG_DK ** -0.5))
        b = log1m_lb + _log_sigmoid(z)
        lf = jnp.maximum(log_lb, b) + jnp.log1p(jnp.exp(-jnp.abs(log_lb - b)))
        k.append(one_m_lb * jax.nn.sigmoid(-z))
        hi, mid, lo = _split3(lf)
        cum.append(jnp.dot(tril, hi, preferred_element_type=F32)
                   + jnp.dot(tril, mid, preferred_element_type=F32)
                   + jnp.dot(tril, lo, preferred_element_type=F32))
    for i in cs:
        chi, clo = _split2(cum[i])
        refs.append(jnp.dot(sel_ref[...], chi, preferred_element_type=F32)
                    + jnp.dot(sel_ref[...], clo, preferred_element_type=F32))
        a.append(jnp.where(row == col, _bdot_nt(q[i], k[i]), 0.0))
    for l in range(HG_LEVELS):
        hs = 1 << l
        mask = (((row ^ col) >> l) == 1) & (row > col)
        for i in cs:
            ref = refs[i][l * c:(l + 1) * c, :]
            if hs >= 8:
                parts = []
                for blk in range(c // hs):
                    sl = slice(blk * hs, (blk + 1) * hs)
                    if blk % 2:
                        parts.append(q[i][sl] * jnp.exp(jnp.minimum(cum[i][sl] - ref[sl], 0.0)))
                    else:
                        parts.append(k[i][sl] * jnp.exp(jnp.minimum(ref[sl] - cum[i][sl], 0.0)))
                x = jnp.concatenate(parts, axis=0).astype(BF16)
            else:
                x = (jnp.where((row & hs) != 0, q[i], k[i]) * jnp.exp(-jnp.abs(cum[i] - ref))).astype(BF16)
            a[i] = jnp.where(mask, _bdot_nt(x, x), a[i])
    intra = [_bdot(a[i], vb[i]) for i in cs]
    last = [cum[i][c - 1:c, :] for i in cs]
    qe = [(q[i] * jnp.exp(cum[i])).astype(BF16) for i in cs]
    kd = [(k[i] * jnp.exp(last[i] - cum[i])).astype(BF16) for i in cs]
    st = st_ref[...]
    for i in cs:
        o = intra[i] + _bdot_nt(qe[i], st)
        st = st * jnp.exp(last[i]) + _bdot_tn(vb[i], kd[i])
        gt = g_ref[rs[i], :]
        y = _rms(o, gnorm) * (gt * jax.nn.sigmoid(gt))
        o_ref[rs[i], :] = y.astype(o_ref.dtype)
    st_ref[...] = st


def _hgrn2_recurrence(proj, par, batch, seq, tt=512):
    n = batch * seq
    nt = seq // tt
    tril, sel = _hgrn2_consts()

    def part(p):
        return pl.BlockSpec((tt, HG_DK), lambda b, h, i, p=p: (b * nt + i, p * HG_HEADS + h))

    return pl.pallas_call(
        functools.partial(_hgrn2_kernel, nchunk=tt // HG_CHUNK),
        grid=(batch, HG_HEADS, nt),
        in_specs=[part(0), part(1), part(2), part(3),
                  pl.BlockSpec((8, HG_DK), lambda b, h, i: (0, h)),
                  _full(tril.shape), _full(sel.shape)],
        out_specs=pl.BlockSpec((tt, HG_DK), lambda b, h, i: (b * nt + i, h)),
        out_shape=jax.ShapeDtypeStruct((n, D_MODEL), BF16),
        scratch_shapes=[pltpu.VMEM((HG_DK, HG_DK), F32)],
        compiler_params=_cparams(("parallel", "parallel", "arbitrary")),
        name="hgrn2_recurrence",
    )(proj, proj, proj, proj, par, tril, sel)


def _hgrn2_mixer(h, g_mix, w_in, w_out_unused, g_norm, lb, batch, seq):
    del w_out_unused
    proj = _norm_proj(h, g_mix, w_in.astype(BF16))
    par = jnp.zeros((8, D_MODEL), F32)
    par = par.at[0].set(jnp.log(lb)).at[1].set(jnp.log1p(-lb)).at[2].set(1.0 - lb)
    par = par.at[3].set(jnp.tile(g_norm.astype(F32), HG_HEADS))
    return _hgrn2_recurrence(proj, par, batch, seq)


def _fox_proj_kernel(h_ref, g_ref, w_ref, wvt_ref, wf_ref, fb_ref, qg_ref, kg_ref, gs_ref, gst_ref, tril_ref,
                     place_ref, q_ref, gate_ref, ka_ref, vt_ref, cum_ref, carry_ref, *, tiles_per_seq):
    d = D_MODEL

    @pl.when(pl.program_id(0) % tiles_per_seq == 0)
    def _():
        carry_ref[...] = jnp.zeros_like(carry_ref)

    xn = _rms(h_ref[...], g_ref[...])
    xb = xn.astype(BF16)

    def headnorm(t, gain):
        shi, slo = _split2(t * t)
        ssq = (jnp.dot(shi, gs_ref[...], preferred_element_type=F32)
               + jnp.dot(slo, gs_ref[...], preferred_element_type=F32))
        inv = lax.rsqrt(ssq * (1.0 / FOX_HD) + EPS)
        ihi, ilo = _split2(inv)
        invf = (jnp.dot(ihi, gst_ref[...], preferred_element_type=F32)
                + jnp.dot(ilo, gst_ref[...], preferred_element_type=F32))
        return t * invf * gain

    q = jnp.dot(xb, w_ref[:, 0:d], preferred_element_type=F32)
    q_ref[...] = (headnorm(q, qg_ref[...]) * (FOX_HD ** -0.5)).astype(q_ref.dtype)
    k = jnp.dot(xb, w_ref[:, d:2 * d], preferred_element_type=F32)
    kn = headnorm(k, kg_ref[...]).astype(BF16)
    gate_ref[...] = jnp.dot(xb, w_ref[:, 3 * d:4 * d], preferred_element_type=F32).astype(gate_ref.dtype)
    vt_ref[...] = lax.dot_general(wvt_ref[...], xb, (((1,), (1,)), ((), ())),
                                  preferred_element_type=F32).astype(vt_ref.dtype)

    fl = jnp.dot(xn, wf_ref[...], preferred_element_type=F32, precision=HI) + fb_ref[...]
    hi, mid, lo = _split3(_log_sigmoid(fl))
    tril = tril_ref[...]
    cum = (jnp.dot(tril, hi, preferred_element_type=F32)
           + jnp.dot(tril, mid, preferred_element_type=F32)
           + jnp.dot(tril, lo, preferred_element_type=F32)) + carry_ref[0:1, :]
    cum_ref[...] = cum
    tm = cum.shape[0]
    carry_ref[...] = jnp.broadcast_to(cum[tm - 1:tm, :], carry_ref.shape)
    nhi, nmid, nlo = _split3(-cum)
    feat = (jnp.dot(nhi, place_ref[0], preferred_element_type=F32)
            + jnp.dot(nmid, place_ref[1], preferred_element_type=F32)
            + jnp.dot(nlo, place_ref[2], preferred_element_type=F32)).astype(BF16)
    for hp in range(FOX_HEADS // 2):
        ka_ref[:, hp * 256:hp * 256 + 128] = kn[:, hp * 128:(hp + 1) * 128]
        ka_ref[:, hp * 256 + 128:(hp + 1) * 256] = feat[:, hp * 128:(hp + 1) * 128]


def _fox_proj(h, g_mix, w_in, f_bias, q_norm, k_norm, seq, tm=256):
    n, d = h.shape
    w_main = w_in[:, :4 * d].astype(BF16)
    w_vt = w_in[:, 2 * d:3 * d].T.astype(BF16)
    w_f = jnp.zeros((d, 128), F32).at[:, :FOX_HEADS].set(w_in[:, 4 * d:].astype(F32))
    fb = jnp.zeros((1, 128), F32).at[0, :FOX_HEADS].set(f_bias.astype(F32))
    head_of = np.arange(d) // FOX_HD
    gs_np = (head_of[:, None] == np.arange(128)[None, :]).astype(np.float32)
    gs = jnp.asarray(gs_np, BF16)
    gst = jnp.asarray(gs_np.T, BF16)
    tril = jnp.asarray(np.tril(np.ones((tm, tm), np.float32)), BF16)
    place_np = np.zeros((3, 128, d), np.float32)
    for hd in range(FOX_HEADS):
        for c in range(3):
            place_np[c, hd, (hd // 2) * 128 + 3 * (hd % 2) + c] = 1.0
    place = jnp.asarray(place_np, BF16)
    qg = jnp.tile(q_norm.astype(F32), FOX_HEADS).reshape(1, d)
    kg = jnp.tile(k_norm.astype(F32), FOX_HEADS).reshape(1, d)
    row = lambda i: (i, 0)
    return pl.pallas_call(
        functools.partial(_fox_proj_kernel, tiles_per_seq=seq // tm),
        grid=(n // tm,),
        in_specs=[pl.BlockSpec((tm, d), row), _full((1, d)), _full((d, 4 * d)), _full((d, d)),
                  _full((d, 128)), _full((1, 128)), _full((1, d)), _full((1, d)),
                  _full((d, 128)), _full((128, d)), _full((tm, tm)), _full((3, 128, d))],
        out_specs=[pl.BlockSpec((tm, d), row), pl.BlockSpec((tm, d), row), pl.BlockSpec((tm, 2 * d), row),
                   pl.BlockSpec((d, tm), lambda i: (0, i)), pl.BlockSpec((tm, 128), row)],
        out_shape=[jax.ShapeDtypeStruct((n, d), BF16), jax.ShapeDtypeStruct((n, d), BF16),
                   jax.ShapeDtypeStruct((n, 2 * d), BF16), jax.ShapeDtypeStruct((d, n), BF16),
                   jax.ShapeDtypeStruct((n, 128), F32)],
        scratch_shapes=[pltpu.VMEM((8, 128), F32)],
        compiler_params=_cparams(("arbitrary",)),
        name="fox_proj",
    )(h, g_mix.reshape(1, d), w_main, w_vt, w_f, fb, qg, kg, gs, gst, tril, place)


def _fox_attn_kernel(jlo_ref, q_ref, k_ref, vt_ref, g_ref, o_ref, qh_ref, m_ref, l_ref, acc_ref, *, tq, nq):
    qi = pl.program_id(2)
    lane = lax.broadcasted_iota(I32, (tq, 128), 1)
    qv = q_ref[...]
    for hh in range(2):
        own = (lane < FOX_HD) if hh == 0 else (lane >= FOX_HD)
        qh_ref[hh, :, 0:128] = jnp.where(own, qv, jnp.zeros_like(qv))
        pick = jnp.where(lane < 3 * hh, 0.0, jnp.where(lane < 3 * hh + 3, 1.0, 0.0))
        qh_ref[hh, :, 128:256] = pick.astype(BF16)
    m_ref[...] = jnp.full_like(m_ref, -jnp.inf)
    l_ref[...] = jnp.zeros_like(l_ref)
    acc_ref[...] = jnp.zeros_like(acc_ref)

    def sweep(kj, masked, nblk=1):
        rows = pl.ds(pl.multiple_of(kj * tq, tq), nblk * tq)
        kb = k_ref[rows, :]
        vt = vt_ref[:, rows]
        sts = [lax.dot_general(kb, qh_ref[hh], (((1,), (1,)), ((), ())), preferred_element_type=F32)
               for hh in range(2)]
        ps, alphas = [], []
        for hh in range(2):
            st = sts[hh]
            if masked:
                key = lax.broadcasted_iota(I32, st.shape, 0)
                qry = lax.broadcasted_iota(I32, st.shape, 1)
                st = jnp.where(key <= qry, st, -jnp.inf)
            m_old = m_ref[hh]
            m_new = jnp.maximum(m_old, jnp.max(st, axis=0, keepdims=True))
            alpha = jnp.exp(m_old - m_new)
            p = jnp.exp(st - m_new[0:1, :])
            l_ref[hh] = alpha * l_ref[hh] + jnp.sum(p, axis=0, keepdims=True)
            m_ref[hh] = m_new
            ps.append(p.astype(BF16))
            alphas.append(alpha[0:1, :])
        for hh in range(2):
            acc_ref[hh] = alphas[hh] * acc_ref[hh] + jnp.dot(
                vt[hh * FOX_HD:(hh + 1) * FOX_HD, :], ps[hh], preferred_element_type=F32)

    lo = jlo_ref[(pl.program_id(0) * pl.num_programs(1) + pl.program_id(1)) * nq + qi]
    span = qi - lo

    def body(i, carry):
        sweep(lo + 2 * i, False, 2)
        return carry

    lax.fori_loop(0, span // 2, body, 0)

    @pl.when(span % 2 == 1)
    def _():
        sweep(qi - 1, False)

    sweep(qi, True)
    ot = jnp.concatenate([acc_ref[0] / l_ref[0][0:1, :], acc_ref[1] / l_ref[1][0:1, :]], axis=0)
    gt = g_ref[...].astype(F32)
    o_ref[...] = (ot.T * jax.nn.sigmoid(gt)).astype(o_ref.dtype)


def _fox_first_block(cum_t, logit_bound, tq):
    bh, _, seq = cum_t.shape
    nq = seq // tq
    blk = cum_t.reshape(bh, 2, nq, tq)
    gap = blk[:, :, :, None, 0] - blk[:, :, None, :, tq - 1]
    dead = jnp.all(2.0 * logit_bound + gap < FOX_ZERO_EXP, axis=1)
    dead = dead & (jnp.arange(nq)[None, None, :] < jnp.arange(nq)[None, :, None])
    return jnp.sum(jnp.cumprod(dead.astype(I32), axis=-1), axis=-1).astype(I32).reshape(-1)


def _fox_attention(q, gate, ka, vt, cum, logit_bound, batch, seq, tq=512):
    n = batch * seq
    nq = seq // tq
    hp = FOX_HEADS // 2
    cum_t = cum[:, :FOX_HEADS].reshape(batch, seq, hp, 2).transpose(0, 2, 3, 1).reshape(batch * hp, 2, seq)
    jlo = _fox_first_block(cum_t, logit_bound, tq)
    grid_spec = pltpu.PrefetchScalarGridSpec(
        num_scalar_prefetch=1,
        grid=(batch, hp, nq),
        in_specs=[
            pl.BlockSpec((tq, 128), lambda b, h, i, jlo: (b * nq + i, h)),
            pl.BlockSpec((seq, 256), lambda b, h, i, jlo: (b, h)),
            pl.BlockSpec((128, seq), lambda b, h, i, jlo: (h, b)),
            pl.BlockSpec((tq, 128), lambda b, h, i, jlo: (b * nq + i, h)),
        ],
        out_specs=pl.BlockSpec((tq, 128), lambda b, h, i, jlo: (b * nq + i, h)),
        scratch_shapes=[pltpu.VMEM((2, tq, 256), BF16), pltpu.VMEM((2, 8, tq), F32),
                        pltpu.VMEM((2, 8, tq), F32), pltpu.VMEM((2, FOX_HD, tq), F32)],
    )
    return pl.pallas_call(
        functools.partial(_fox_attn_kernel, tq=tq, nq=nq),
        grid_spec=grid_spec,
        out_shape=jax.ShapeDtypeStruct((n, D_MODEL), BF16),
        compiler_params=_cparams(("parallel", "parallel", "arbitrary")),
        name="fox_attention",
    )(jlo, q, ka, vt, gate)


def _fox_mixer(h, g_mix, w_in, f_bias, q_norm, k_norm, batch, seq):
    q, gate, ka, vt, cum = _fox_proj(h, g_mix, w_in, f_bias, q_norm, k_norm, seq)
    logit_bound = 1.02 * FOX_HD ** 0.5 * jnp.max(jnp.abs(q_norm.astype(F32))) * jnp.max(jnp.abs(k_norm.astype(F32)))
    return _fox_attention(q, gate, ka, vt, cum, logit_bound, batch, seq)


def _rglru_kernel(gate_ref, u_ref, par_ref, wa_ref, wx_ref, o_ref, prev_ref, hc_ref, *, tt):
    @pl.when(pl.program_id(1) == 0)
    def _():
        prev_ref[...] = jnp.zeros_like(prev_ref)
        hc_ref[...] = jnp.zeros_like(hc_ref)

    u = u_ref[...]
    ext = jnp.concatenate([prev_ref[...], u], axis=0)
    conv = par_ref[4:5, :] + u * par_ref[3:4, :]
    for shift in range(1, CONV_W):
        conv = conv + pltpu.roll(ext, shift, 0)[8:, :] * par_ref[3 - shift:4 - shift, :]
    prev_ref[...] = u[tt - 8:, :]

    cb = conv.astype(BF16)
    ra, ia = [], []
    for nb in range(RG_BLOCKS):
        blk = cb[:, nb * RG_BW:(nb + 1) * RG_BW]
        ra.append(jnp.dot(blk, wa_ref[nb], preferred_element_type=F32))
        ia.append(jnp.dot(blk, wx_ref[nb], preferred_element_type=F32))
    r = jax.nn.sigmoid(jnp.concatenate(ra, axis=1) + par_ref[5:6, :])
    ig = jax.nn.sigmoid(jnp.concatenate(ia, axis=1) + par_ref[6:7, :])
    lam = par_ref[7:8, :]
    softplus = jnp.maximum(-lam, 0.0) + jnp.log1p(jnp.exp(-jnp.abs(lam)))
    log_a = -RG_C * r * softplus
    a = jnp.exp(log_a)
    b = jnp.sqrt(1.0 - a * a) * (ig * conv)

    row = lax.broadcasted_iota(I32, a.shape, 0)
    dist = 1
    while dist < tt:
        ok = row >= dist
        a_sh = pltpu.roll(a, dist, 0)
        b_sh = pltpu.roll(b, dist, 0)
        b = jnp.where(ok, a * b_sh + b, b)
        a = jnp.where(ok, a * a_sh, a)
        dist *= 2
    hs = b + a * hc_ref[0:1, :]
    hc_ref[...] = jnp.broadcast_to(hs[tt - 1:tt, :], hc_ref.shape)

    gt = gate_ref[...]
    gelu = 0.5 * gt * (1.0 + jnp.tanh(0.7978845608028654 * (gt + 0.044715 * gt * gt * gt)))
    o_ref[...] = (hs * gelu).astype(o_ref.dtype)


def _rglru_mixer(h, g_mix, w_in, conv_w, conv_b, w_a, b_a, w_x, b_x, lam, batch, seq, tt=256):
    n = batch * seq
    nt = seq // tt
    w = D_MODEL
    proj = _norm_proj(h, g_mix, w_in.astype(BF16))
    par = jnp.concatenate([conv_w.astype(F32), conv_b.reshape(1, w), b_a.reshape(1, w),
                           b_x.reshape(1, w), lam.reshape(1, w)], axis=0).astype(F32)
    return pl.pallas_call(
        functools.partial(_rglru_kernel, tt=tt),
        grid=(batch, nt),
        in_specs=[pl.BlockSpec((tt, w), lambda b, i: (b * nt + i, 0)),
                  pl.BlockSpec((tt, w), lambda b, i: (b * nt + i, 1)),
                  _full((8, w)), _full((RG_BLOCKS, RG_BW, RG_BW)), _full((RG_BLOCKS, RG_BW, RG_BW))],
        out_specs=pl.BlockSpec((tt, w), lambda b, i: (b * nt + i, 0)),
        out_shape=jax.ShapeDtypeStruct((n, w), BF16),
        scratch_shapes=[pltpu.VMEM((8, w), F32), pltpu.VMEM((8, w), F32)],
        compiler_params=_cparams(("parallel", "arbitrary")),
        name="rglru",
    )(proj, proj, par, w_a.astype(BF16), w_x.astype(BF16))


def _outproj_router_kernel(a_ref, h_ref, wo_ref, g_ref, wr_ref, br_ref, tri_ref, lst_ref,
                           hm_ref, xn_ref, gate_ref, pos_ref, cnt_ref):
    hm = h_ref[...] + jnp.dot(a_ref[...], wo_ref[...], preferred_element_type=F32)
    hm_ref[...] = hm
    xn = _rms(hm, g_ref[...])
    xn_ref[...] = xn.astype(xn_ref.dtype)
    logit = lax.dot_general(wr_ref[...], xn, (((1,), (1,)), ((), ())),
                            preferred_element_type=F32, precision=HI) + br_ref[:, 0:1]
    ne, tm = logit.shape
    eidx = lax.broadcasted_iota(I32, (ne, tm), 0)
    work = logit
    vals, hots = [], []
    for _ in range(TOP_K):
        mx = jnp.max(work, axis=0, keepdims=True)
        pick = jnp.min(jnp.where(work == mx, eidx, ne), axis=0, keepdims=True)
        hot = eidx == pick
        work = jnp.where(hot, -jnp.inf, work)
        vals.append(mx)
        hots.append(hot)
    ex = [jnp.exp(v - vals[0]) for v in vals]
    den = ex[0] + ex[1] + ex[2] + ex[3]
    gate_ref[...] = jnp.concatenate([e / den for e in ex], axis=0)
    tok_hot = (hots[0] | hots[1] | hots[2] | hots[3]).astype(BF16)
    before = jnp.dot(tok_hot, tri_ref[...], preferred_element_type=F32)
    count = jnp.sum(tok_hot.astype(F32), axis=1, keepdims=True)
    units = jnp.floor((count + (STRIP_ALIGN - 1)) * (1.0 / STRIP_ALIGN))
    units_b = jnp.broadcast_to(units, (ne, 128)).astype(BF16)
    start = STRIP_ALIGN * jnp.dot(lst_ref[...], units_b, preferred_element_type=F32)[:, 0:1]
    where = before + start
    pos_ref[...] = jnp.concatenate(
        [jnp.sum(jnp.where(h, where, 0.0), axis=0, keepdims=True) for h in hots], axis=0).astype(I32)
    cnt_ref[...] = jnp.broadcast_to(STRIP_ALIGN * units, cnt_ref.shape)


def _outproj_router(act, h, w_out, g_ffn, w_r, b_r):
    n, d = h.shape
    tm = MOE_TILE
    tri = jnp.asarray(np.triu(np.ones((tm, tm), np.float32), 1), BF16)
    lst = jnp.asarray(np.tril(np.ones((N_EXPERTS, N_EXPERTS), np.float32), -1), BF16)
    row = lambda i: (i, 0)
    colb = lambda i: (0, i)
    return pl.pallas_call(
        _outproj_router_kernel,
        grid=(n // tm,),
        in_specs=[pl.BlockSpec((tm, d), row), pl.BlockSpec((tm, d), row), _full((d, d)), _full((1, d)),
                  _full((N_EXPERTS, d)), _full((N_EXPERTS, 128)), _full((tm, tm)),
                  _full((N_EXPERTS, N_EXPERTS))],
        out_specs=[pl.BlockSpec((tm, d), row), pl.BlockSpec((tm, d), row),
                   pl.BlockSpec((TOP_K, tm), colb), pl.BlockSpec((TOP_K, tm), colb),
                   pl.BlockSpec((N_EXPERTS, 128), row)],
        out_shape=[jax.ShapeDtypeStruct((n, d), F32), jax.ShapeDtypeStruct((n, d), BF16),
                   jax.ShapeDtypeStruct((TOP_K, n), F32), jax.ShapeDtypeStruct((TOP_K, n), I32),
                   jax.ShapeDtypeStruct((n // tm * N_EXPERTS, 128), F32)],
        compiler_params=_cparams(("parallel",)),
        name="outproj_router",
    )(act, h, w_out.astype(BF16), g_ffn.reshape(1, d), w_r.T.astype(F32),
      jnp.broadcast_to(b_r.astype(F32)[:, None], (N_EXPERTS, 128)), tri, lst)


def _for_strip_chunks(nrows, max_rows, fn):
    size = max_rows
    while size >= STRIP_ALIGN:
        @pl.when((nrows & size) != 0)
        def _(size=size):
            fn(pl.multiple_of(nrows & ~(2 * size - 1), STRIP_ALIGN), size)
        size //= 2


def _for_tile_strips(n8_ref, sbase_ref, gdst_ref, tile, fn):
    def per_expert(e, carry):
        i = tile * N_EXPERTS + e
        so = sbase_ref[i]
        gd = gdst_ref[i]
        _for_strip_chunks(n8_ref[i], MOE_TILE,
                          lambda o, size: fn(pl.multiple_of(so + o, STRIP_ALIGN), pl.multiple_of(gd + o, STRIP_ALIGN), size))
        return carry

    lax.fori_loop(0, N_EXPERTS, per_expert, 0)


def _dispatch_kernel(n8_ref, sbase_ref, gdst_ref, tot_ref, pst_ref, x_ref, pos_ref, xs_hbm,
                     stage, zblk, rsem, zsem):
    tile = pl.program_id(0)
    slot = tile % 2
    xb = x_ref[...]
    for c in range(STAGE_ROWS // ONEHOT_CHUNK):
        rows = lax.broadcasted_iota(I32, (ONEHOT_CHUNK, MOE_TILE), 0) + c * ONEHOT_CHUNK
        onehot = jnp.zeros((ONEHOT_CHUNK, MOE_TILE), F32)
        for k in range(TOP_K):
            onehot = jnp.where(rows == pos_ref[k:k + 1, :], 1.0, onehot)
        stage[slot, c * ONEHOT_CHUNK:(c + 1) * ONEHOT_CHUNK, :] = jnp.dot(
            onehot.astype(BF16), xb, preferred_element_type=F32)

    def copy(sl, so, gd, size):
        return pltpu.make_async_copy(stage.at[sl, pl.ds(so, size), :], xs_hbm.at[pl.ds(gd, size), :], rsem.at[sl])

    _for_tile_strips(n8_ref, sbase_ref, gdst_ref, tile, lambda so, gd, size: copy(slot, so, gd, size).start())

    @pl.when(tile > 0)
    def _():
        _for_tile_strips(n8_ref, sbase_ref, gdst_ref, tile - 1,
                         lambda so, gd, size: copy(1 - slot, so, gd, size).wait())

    @pl.when(tile == 0)
    def _():
        zblk[...] = jnp.zeros_like(zblk)

        def zero(first, size):
            cp = pltpu.make_async_copy(zblk.at[pl.ds(0, size), :], xs_hbm.at[pl.ds(first, size), :], zsem)
            cp.start()
            cp.wait()

        def per_expert(e, carry):
            tot = tot_ref[e]
            first = pst_ref[e] + tot
            npad = ((tot + (MOE_BLK - 1)) // MOE_BLK) * MOE_BLK - tot
            _for_strip_chunks(npad, MOE_BLK // 2,
                              lambda o, size: zero(pl.multiple_of(first + o, STRIP_ALIGN), size))
            return carry

        lax.fori_loop(0, N_EXPERTS, per_expert, 0)

        last = N_EXPERTS - 1
        used = (pst_ref[last] + tot_ref[last] + (MOE_BLK - 1)) // MOE_BLK

        def ztail(b, c):
            zero(pl.multiple_of(b * MOE_BLK, MOE_BLK), MOE_BLK)
            return c

        lax.fori_loop(used, xs_hbm.shape[0] // MOE_BLK, ztail, 0)

    @pl.when(tile == pl.num_programs(0) - 1)
    def _():
        _for_tile_strips(n8_ref, sbase_ref, gdst_ref, tile, lambda so, gd, size: copy(slot, so, gd, size).wait())


def _dispatch(xn, pos, n8, sbase, gdst, total, pstart, cap):
    n, d = xn.shape
    tm = MOE_TILE
    grid_spec = pltpu.PrefetchScalarGridSpec(
        num_scalar_prefetch=5,
        grid=(n // tm,),
        in_specs=[pl.BlockSpec((tm, d), lambda i, *_: (i, 0)), pl.BlockSpec((TOP_K, tm), lambda i, *_: (0, i))],
        out_specs=pl.BlockSpec(memory_space=pl.ANY),
        scratch_shapes=[pltpu.VMEM((2, STAGE_ROWS, d), F32), pltpu.VMEM((MOE_BLK, d), F32),
                        pltpu.SemaphoreType.DMA((2,)), pltpu.SemaphoreType.DMA],
    )
    return pl.pallas_call(
        _dispatch_kernel,
        grid_spec=grid_spec,
        out_shape=jax.ShapeDtypeStruct((cap, d), F32),
        compiler_params=_cparams(("arbitrary",)),
        name="moe_dispatch",
    )(n8, sbase, gdst, total, pstart, xn, pos)


def _expert_kernel(be_ref, nu_ref, x_ref, wgu_ref, bgu_ref, wdn_ref, bdn_ref, y_ref, wgu_bf, wdn_bf):
    d = D_MODEL
    b = pl.program_id(0)

    @pl.when((b == 0) | (be_ref[b] != be_ref[jnp.maximum(b - 1, 0)]))
    def _():
        wgu_bf[...] = wgu_ref[...].astype(BF16)
        wdn_bf[...] = wdn_ref[...].astype(BF16)

    @pl.when(b < nu_ref[0])
    def _():
        xb = x_ref[...].astype(BF16)
        acc = None
        for c in range(d // EXPERT_CHUNK):
            lo, hi = c * EXPERT_CHUNK, (c + 1) * EXPERT_CHUNK
            glu = jnp.dot(xb, wgu_bf[:, lo:hi], preferred_element_type=F32) + bgu_ref[:, lo:hi]
            lin = jnp.dot(xb, wgu_bf[:, d + lo:d + hi], preferred_element_type=F32) + bgu_ref[:, d + lo:d + hi]
            glu = jnp.minimum(glu, SWIGLU_LIMIT)
            lin = jnp.clip(lin, -SWIGLU_LIMIT, SWIGLU_LIMIT)
            act = glu * jax.nn.sigmoid(SWIGLU_ALPHA * glu) * (lin + 1.0)
            part = jnp.dot(act.astype(BF16), wdn_bf[lo:hi, :], preferred_element_type=F32)
            acc = part if acc is None else acc + part
        y_ref[...] = acc + bdn_ref[...]

    @pl.when(b >= nu_ref[0])
    def _():
        y_ref[...] = jnp.zeros_like(y_ref)


def _experts(xs, block_expert, n_used, w_gu, b_gu, w_dn, b_dn):
    cap, d = xs.shape
    nb = cap // MOE_BLK
    ne = w_gu.shape[0]

    def xmap(b, be, nu):
        return (jnp.minimum(b, nu[0] - 1), 0)

    def wmap(b, be, nu):
        return (be[b], 0, 0)

    grid_spec = pltpu.PrefetchScalarGridSpec(
        num_scalar_prefetch=2,
        grid=(nb,),
        in_specs=[pl.BlockSpec((MOE_BLK, d), xmap),
                  pl.BlockSpec((None, d, 2 * d), wmap), pl.BlockSpec((None, 1, 2 * d), wmap),
                  pl.BlockSpec((None, d, d), wmap), pl.BlockSpec((None, 1, d), wmap)],
        out_specs=pl.BlockSpec((MOE_BLK, d), lambda b, be, nu: (b, 0)),
        scratch_shapes=[pltpu.VMEM((d, 2 * d), BF16), pltpu.VMEM((d, d), BF16)],
    )
    return pl.pallas_call(
        _expert_kernel,
        grid_spec=grid_spec,
        out_shape=jax.ShapeDtypeStruct((cap, d), F32),
        compiler_params=_cparams(("arbitrary",)),
        name="moe_experts",
    )(block_expert, n_used, xs, w_gu, b_gu.reshape(ne, 1, 2 * d), w_dn, b_dn.reshape(ne, 1, d))


def _combine_ple_kernel(n8_ref, sbase_ref, gdst_ref, y_hbm, hm_ref, gate_ref, pos_ref, p_ref, pw_ref, pn_ref,
                        gn_ref, gw_ref, o_ref, stage, rsem):
    tile = pl.program_id(0)
    slot = tile % 2

    def copy(sl, so, gd, size):
        return pltpu.make_async_copy(y_hbm.at[pl.ds(gd, size), :], stage.at[sl, pl.ds(so, size), :], rsem.at[sl])

    def fetch(t, sl):
        _for_tile_strips(n8_ref, sbase_ref, gdst_ref, t, lambda so, gd, size: copy(sl, so, gd, size).start())

    @pl.when(tile == 0)
    def _():
        stage[...] = jnp.zeros_like(stage)
        fetch(0, 0)

    @pl.when(tile + 1 < pl.num_programs(0))
    def _():
        fetch(tile + 1, 1 - slot)

    ple = _rms(jnp.dot(p_ref[...].astype(BF16), pw_ref[...], preferred_element_type=F32), pn_ref[...])
    _for_tile_strips(n8_ref, sbase_ref, gdst_ref, tile, lambda so, gd, size: copy(slot, so, gd, size).wait())
    h2 = hm_ref[...]
    for c in range(STAGE_ROWS // COMBINE_CHUNK):
        lanes = lax.broadcasted_iota(I32, (MOE_TILE, COMBINE_CHUNK), 1) + c * COMBINE_CHUNK
        weights = jnp.zeros((MOE_TILE, COMBINE_CHUNK), F32)
        for k in range(TOP_K):
            weights = jnp.where(lanes == pos_ref[:, k:k + 1], gate_ref[:, k:k + 1], weights)
        staged = stage[slot, c * COMBINE_CHUNK:(c + 1) * COMBINE_CHUNK, :].astype(BF16)
        h2 = h2 + jnp.dot(weights.astype(BF16), staged, preferred_element_type=F32)
    gate = jax.nn.sigmoid(jnp.dot(_rms(h2, gn_ref[...]).astype(BF16), gw_ref[...], preferred_element_type=F32))
    o_ref[...] = h2 + ple * gate


def _combine_ple(y, pos, n8, sbase, gdst, h_mid, gates, p, ple_w, ple_norm, gate_norm, gate_w):
    n, d = h_mid.shape
    tm = MOE_TILE
    row = lambda i, *_: (i, 0)
    full = lambda shape: pl.BlockSpec(shape, lambda i, *_: (0,) * len(shape))
    grid_spec = pltpu.PrefetchScalarGridSpec(
        num_scalar_prefetch=3,
        grid=(n // tm,),
        in_specs=[pl.BlockSpec(memory_space=pl.ANY), pl.BlockSpec((tm, d), row), pl.BlockSpec((tm, TOP_K), row),
                  pl.BlockSpec((tm, TOP_K), row), pl.BlockSpec((tm, PLE_DIM), row), full((PLE_DIM, d)),
                  full((1, d)), full((1, d)), full((d, d))],
        out_specs=pl.BlockSpec((tm, d), row),
        scratch_shapes=[pltpu.VMEM((2, STAGE_ROWS, d), F32), pltpu.SemaphoreType.DMA((2,))],
    )
    return pl.pallas_call(
        _combine_ple_kernel,
        grid_spec=grid_spec,
        out_shape=jax.ShapeDtypeStruct((n, d), F32),
        compiler_params=_cparams(("arbitrary",)),
        name="moe_combine_ple",
    )(n8, sbase, gdst, y, h_mid, gates.T, pos.T, p, ple_w.astype(BF16), ple_norm.reshape(1, d),
      gate_norm.reshape(1, d), gate_w.astype(BF16))


def _moe_ple(act, h, w_out, g_ffn, w_r, b_r, layer, w_gu, b_gu, w_dn, b_dn, p, ple_w, ple_norm, gate_norm, gate_w):
    n, d = h.shape
    ntiles = n // MOE_TILE
    h_mid, xn, gates, pos, cnt = _outproj_router(act, h, w_out, g_ffn, w_r, b_r)
    n8 = cnt[:, 0].astype(I32).reshape(ntiles, N_EXPERTS)
    sbase = jnp.cumsum(n8, axis=1) - n8
    total = jnp.sum(n8, axis=0)
    padded = (total + MOE_BLK - 1) // MOE_BLK * MOE_BLK
    pend = jnp.cumsum(padded)
    pstart = pend - padded
    gdst = pstart[None, :] + jnp.cumsum(n8, axis=0) - n8
    nb = -(-(n * TOP_K + ntiles * N_EXPERTS * (STRIP_ALIGN - 1)) // MOE_BLK) + N_EXPERTS
    cap = nb * MOE_BLK
    block_start = jnp.arange(nb, dtype=I32) * MOE_BLK
    block_expert = jnp.minimum(jnp.sum(pend[None, :] <= block_start[:, None], axis=1), N_EXPERTS - 1).astype(I32)
    n_used = (pend[-1:] // MOE_BLK).astype(I32)
    n8f, sbf, gdf = n8.reshape(-1), sbase.reshape(-1).astype(I32), gdst.reshape(-1).astype(I32)
    xs = _dispatch(xn, pos, n8f, sbf, gdf, total.astype(I32), pstart.astype(I32), cap)
    y = _experts(xs, block_expert + layer * N_EXPERTS, n_used, w_gu, b_gu, w_dn, b_dn)
    return _combine_ple(y, pos, n8f, sbf, gdf, h_mid, gates, p, ple_w, ple_norm, gate_norm, gate_w)


def kernel(x, p, norm_mix, norm_ffn, hg_w_in, hg_w_out, hg_gnorm, hg_lb_param, fox_w_in, fox_f_bias, fox_qnorm, fox_knorm, fox_w_out, rg_w_in, rg_conv_w, rg_conv_b, rg_wa, rg_ba, rg_wx, rg_bx, rg_lambda, rg_w_out, router_w, router_b, moe_w_gu, moe_b_gu, moe_w_dn, moe_b_dn, ple_w, ple_norm, ple_gate_norm, ple_gate_w):
    batch, seq, d = x.shape
    depth = p.shape[0]
    n = batch * seq
    lb_all = jnp.cumsum(jax.nn.softmax(hg_lb_param.astype(F32), axis=0), axis=0)
    lb_all = lb_all - lb_all[0]
    h = x.reshape(n, d).astype(F32)
    ne = depth * N_EXPERTS
    w_gu = moe_w_gu.astype(F32).reshape(ne, d, 2 * d)
    b_gu = moe_b_gu.astype(F32).reshape(ne, 2 * d)
    w_dn = moe_w_dn.astype(F32).reshape(ne, d, d)
    b_dn = moe_b_dn.astype(F32).reshape(ne, d)
    for i in range(depth):
        j = i // 3
        kind = i % 3
        if kind == 0:
            act = _hgrn2_mixer(h, norm_mix[i], hg_w_in[j], None, hg_gnorm[j], lb_all[i], batch, seq)
            w_out = hg_w_out[j]
        elif kind == 1:
            act = _fox_mixer(h, norm_mix[i], fox_w_in[j], fox_f_bias[j], fox_qnorm[j], fox_knorm[j], batch, seq)
            w_out = fox_w_out[j]
        else:
            act = _rglru_mixer(h, norm_mix[i], rg_w_in[j], rg_conv_w[j], rg_conv_b[j], rg_wa[j], rg_ba[j],
                               rg_wx[j], rg_bx[j], rg_lambda[j], batch, seq)
            w_out = rg_w_out[j]
        h = _moe_ple(act, h, w_out, norm_ffn[i], router_w[i], router_b[i], i, w_gu, b_gu, w_dn, b_dn,
                     p[i].reshape(n, PLE_DIM), ple_w[i], ple_norm[i], ple_gate_norm[i], ple_gate_w[i])
    return h.reshape(batch, seq, d)
```

```python
import functools

import jax
import jax.numpy as jnp
import numpy as np
from jax import lax
from jax.experimental import pallas as pl
from jax.experimental.pallas import tpu as pltpu

F32 = jnp.float32
BF16 = jnp.bfloat16
I32 = jnp.int32

D_MODEL = 1024
EPS = 1e-6
PLE_DIM = 256

HG_HEADS = 8
HG_DK = 128
HG_CHUNK = 128
HG_LEVELS = 7

FOX_HEADS = 16
FOX_HD = 64
FOX_ZERO_EXP = -105.0

RG_BLOCKS = 4
RG_BW = 256
CONV_W = 4
RG_C = 8.0

N_EXPERTS = 32
TOP_K = 4
SWIGLU_LIMIT = 7.0
SWIGLU_ALPHA = 1.702
MOE_BLK = 512
EXPERT_CHUNK = 512
MOE_TILE = 512
STRIP_ALIGN = 8
STAGE_ROWS = -(-(TOP_K * MOE_TILE + N_EXPERTS * (STRIP_ALIGN - 1)) // 256) * 256
ONEHOT_CHUNK = 256
COMBINE_CHUNK = STAGE_ROWS // 3

VMEM_LIMIT = 56 * 1024 * 1024

HI = lax.Precision.HIGHEST


def _cparams(sem):
    return pltpu.CompilerParams(dimension_semantics=sem, vmem_limit_bytes=VMEM_LIMIT)


def _bdot(a, b):
    return jnp.dot(a.astype(BF16), b.astype(BF16), preferred_element_type=F32)


def _bdot_nt(a, b):
    return lax.dot_general(a.astype(BF16), b.astype(BF16), (((1,), (1,)), ((), ())),
                           preferred_element_type=F32)


def _bdot_tn(a, b):
    return lax.dot_general(a.astype(BF16), b.astype(BF16), (((0,), (0,)), ((), ())),
                           preferred_element_type=F32)


def _rms(x, g):
    return x * lax.rsqrt(jnp.mean(x * x, axis=-1, keepdims=True) + EPS) * g


def _split2(x):
    hi = x.astype(BF16)
    lo = (x - hi.astype(F32)).astype(BF16)
    return hi, lo


def _split3(x):
    hi = x.astype(BF16)
    r = x - hi.astype(F32)
    mid = r.astype(BF16)
    lo = (r - mid.astype(F32)).astype(BF16)
    return hi, mid, lo


def _log_sigmoid(z):
    return jnp.minimum(z, 0.0) - jnp.log1p(jnp.exp(-jnp.abs(z)))


def _full(shape):
    return pl.BlockSpec(shape, lambda *_: (0,) * len(shape))


def _norm_proj_kernel(h_ref, g_ref, w_ref, o_ref, *, cn):
    xn = _rms(h_ref[...], g_ref[...]).astype(BF16)
    m = w_ref.shape[1]
    for c in range(m // cn):
        o_ref[:, c * cn:(c + 1) * cn] = jnp.dot(
            xn, w_ref[:, c * cn:(c + 1) * cn], preferred_element_type=F32).astype(o_ref.dtype)


def _norm_proj(h, g, w, tm=256, out_dtype=F32):
    n, d = h.shape
    m = w.shape[1]
    return pl.pallas_call(
        functools.partial(_norm_proj_kernel, cn=512),
        grid=(n // tm,),
        in_specs=[pl.BlockSpec((tm, d), lambda i: (i, 0)), _full((1, d)), _full((d, m))],
        out_specs=pl.BlockSpec((tm, m), lambda i: (i, 0)),
        out_shape=jax.ShapeDtypeStruct((n, m), out_dtype),
        compiler_params=_cparams(("parallel",)),
        name="norm_proj",
    )(h, g.reshape(1, d), w)


def _hgrn2_consts():
    c = HG_CHUNK
    t = np.arange(c)
    tril = (t[:, None] >= t[None, :]).astype(np.float32)
    sel = np.zeros((HG_LEVELS, c, c), np.float32)
    for l in range(HG_LEVELS):
        hs = 1 << l
        m = (t // (2 * hs)) * (2 * hs) + hs - 1
        sel[l, t, m] = 1.0
    return jnp.asarray(tril, BF16), jnp.asarray(sel.reshape(HG_LEVELS * c, c), BF16)


def _hgrn2_kernel(q_ref, z_ref, v_ref, g_ref, par_ref, tril_ref, sel_ref, o_ref, st_ref, *, nchunk):
    c = HG_CHUNK

    @pl.when(pl.program_id(2) == 0)
    def _():
        st_ref[...] = jnp.zeros_like(st_ref)

    log_lb = par_ref[0:1, :]
    log1m_lb = par_ref[1:2, :]
    one_m_lb = par_ref[2:3, :]
    gnorm = par_ref[3:4, :]
    row = lax.broadcasted_iota(I32, (c, c), 0)
    col = lax.broadcasted_iota(I32, (c, c), 1)

    cs = range(nchunk)
    rs = [slice(ci * c, (ci + 1) * c) for ci in cs]
    tril = tril_ref[...]
    q, k, vb, cum, refs, a = [], [], [], [], [], []
    for r in rs:
        qr = q_ref[r, :]
        z = z_ref[r, :]
        vb.append(v_ref[r, :].astype(BF16))
        q.append(qr * jax.nn.sigmoid(qr) * (HG_DK ** -0.5))
        b = log1m_lb + _log_sigmoid(z)
        lf = jnp.maximum(log_lb, b) + jnp.log1p(jnp.exp(-jnp.abs(log_lb - b)))
        k.append(one_m_lb * jax.nn.sigmoid(-z))
        hi, mid, lo = _split3(lf)
        cum.append(jnp.dot(tril, hi, preferred_element_type=F32)
                   + jnp.dot(tril, mid, preferred_element_type=F32)
                   + jnp.dot(tril, lo, preferred_element_type=F32))
    for i in cs:
        chi, clo = _split2(cum[i])
        refs.append(jnp.dot(sel_ref[...], chi, preferred_element_type=F32)
                    + jnp.dot(sel_ref[...], clo, preferred_element_type=F32))
        a.append(jnp.where(row == col, _bdot_nt(q[i], k[i]), 0.0))
    for l in range(HG_LEVELS):
        hs = 1 << l
        mask = (((row ^ col) >> l) == 1) & (row > col)
        for i in cs:
            ref = refs[i][l * c:(l + 1) * c, :]
            if hs >= 8:
                parts = []
                for blk in range(c // hs):
                    sl = slice(blk * hs, (blk + 1) * hs)
                    if blk % 2:
                        parts.append(q[i][sl] * jnp.exp(jnp.minimum(cum[i][sl] - ref[sl], 0.0)))
                    else:
                        parts.append(k[i][sl] * jnp.exp(jnp.minimum(ref[sl] - cum[i][sl], 0.0)))
                x = jnp.concatenate(parts, axis=0).astype(BF16)
            else:
                x = (jnp.where((row & hs) != 0, q[i], k[i]) * jnp.exp(-jnp.abs(cum[i] - ref))).astype(BF16)
            a[i] = jnp.where(mask, _bdot_nt(x, x), a[i])
    intra = [_bdot(a[i], vb[i]) for i in cs]
    last = [cum[i][c - 1:c, :] for i in cs]
    qe = [(q[i] * jnp.exp(cum[i])).astype(BF16) for i in cs]
    kd = [(k[i] * jnp.exp(last[i] - cum[i])).astype(BF16) for i in cs]
    st = st_ref[...]
    for i in cs:
        o = intra[i] + _bdot_nt(qe[i], st)
        st = st * jnp.exp(last[i]) + _bdot_tn(vb[i], kd[i])
        gt = g_ref[rs[i], :]
        y = _rms(o, gnorm) * (gt * jax.nn.sigmoid(gt))
        o_ref[rs[i], :] = y.astype(o_ref.dtype)
    st_ref[...] = st


def _hgrn2_recurrence(proj, par, batch, seq, tt=1024):
    n = batch * seq
    nt = seq // tt
    tril, sel = _hgrn2_consts()

    def part(p):
        return pl.BlockSpec((tt, HG_DK), lambda b, h, i, p=p: (b * nt + i, p * HG_HEADS + h))

    return pl.pallas_call(
        functools.partial(_hgrn2_kernel, nchunk=tt // HG_CHUNK),
        grid=(batch, HG_HEADS, nt),
        in_specs=[part(0), part(1), part(2), part(3),
                  pl.BlockSpec((8, HG_DK), lambda b, h, i: (0, h)),
                  _full(tril.shape), _full(sel.shape)],
        out_specs=pl.BlockSpec((tt, HG_DK), lambda b, h, i: (b * nt + i, h)),
        out_shape=jax.ShapeDtypeStruct((n, D_MODEL), BF16),
        scratch_shapes=[pltpu.VMEM((HG_DK, HG_DK), F32)],
        compiler_params=_cparams(("parallel", "parallel", "arbitrary")),
        name="hgrn2_recurrence",
    )(proj, proj, proj, proj, par, tril, sel)


def _hgrn2_mixer(h, g_mix, w_in, w_out_unused, g_norm, lb, batch, seq):
    del w_out_unused
    proj = _norm_proj(h, g_mix, w_in.astype(BF16))
    par = jnp.zeros((8, D_MODEL), F32)
    par = par.at[0].set(jnp.log(lb)).at[1].set(jnp.log1p(-lb)).at[2].set(1.0 - lb)
    par = par.at[3].set(jnp.tile(g_norm.astype(F32), HG_HEADS))
    return _hgrn2_recurrence(proj, par, batch, seq)


def _fox_proj_kernel(h_ref, g_ref, w_ref, wvt_ref, wf_ref, fb_ref, qg_ref, kg_ref, gs_ref, gst_ref, tril_ref,
                     place_ref, q_ref, gate_ref, ka_ref, vt_ref, cum_ref, carry_ref, *, tiles_per_seq):
    d = D_MODEL

    @pl.when(pl.program_id(0) % tiles_per_seq == 0)
    def _():
        carry_ref[...] = jnp.zeros_like(carry_ref)

    xn = _rms(h_ref[...], g_ref[...])
    xb = xn.astype(BF16)

    def headnorm(t, gain):
        shi, slo = _split2(t * t)
        ssq = (jnp.dot(shi, gs_ref[...], preferred_element_type=F32)
               + jnp.dot(slo, gs_ref[...], preferred_element_type=F32))
        inv = lax.rsqrt(ssq * (1.0 / FOX_HD) + EPS)
        ihi, ilo = _split2(inv)
        invf = (jnp.dot(ihi, gst_ref[...], preferred_element_type=F32)
                + jnp.dot(ilo, gst_ref[...], preferred_element_type=F32))
        return t * invf * gain

    q = jnp.dot(xb, w_ref[:, 0:d], preferred_element_type=F32)
    q_ref[...] = (headnorm(q, qg_ref[...]) * (FOX_HD ** -0.5)).astype(q_ref.dtype)
    k = jnp.dot(xb, w_ref[:, d:2 * d], preferred_element_type=F32)
    kn = headnorm(k, kg_ref[...]).astype(BF16)
    gate_ref[...] = jnp.dot(xb, w_ref[:, 3 * d:4 * d], preferred_element_type=F32).astype(gate_ref.dtype)
    vt_ref[...] = lax.dot_general(wvt_ref[...], xb, (((1,), (1,)), ((), ())),
                                  preferred_element_type=F32).astype(vt_ref.dtype)

    fl = jnp.dot(xn, wf_ref[...], preferred_element_type=F32, precision=HI) + fb_ref[...]
    hi, mid, lo = _split3(_log_sigmoid(fl))
    tril = tril_ref[...]
    cum = (jnp.dot(tril, hi, preferred_element_type=F32)
           + jnp.dot(tril, mid, preferred_element_type=F32)
           + jnp.dot(tril, lo, preferred_element_type=F32)) + carry_ref[0:1, :]
    cum_ref[...] = cum
    tm = cum.shape[0]
    carry_ref[...] = jnp.broadcast_to(cum[tm - 1:tm, :], carry_ref.shape)
    nhi, nmid, nlo = _split3(-cum)
    feat = (jnp.dot(nhi, place_ref[0], preferred_element_type=F32)
            + jnp.dot(nmid, place_ref[1], preferred_element_type=F32)
            + jnp.dot(nlo, place_ref[2], preferred_element_type=F32)).astype(BF16)
    for hp in range(FOX_HEADS // 2):
        ka_ref[:, hp * 256:hp * 256 + 128] = kn[:, hp * 128:(hp + 1) * 128]
        ka_ref[:, hp * 256 + 128:(hp + 1) * 256] = feat[:, hp * 128:(hp + 1) * 128]


def _fox_proj(h, g_mix, w_in, f_bias, q_norm, k_norm, seq, tm=256):
    n, d = h.shape
    w_main = w_in[:, :4 * d].astype(BF16)
    w_vt = w_in[:, 2 * d:3 * d].T.astype(BF16)
    w_f = jnp.zeros((d, 128), F32).at[:, :FOX_HEADS].set(w_in[:, 4 * d:].astype(F32))
    fb = jnp.zeros((1, 128), F32).at[0, :FOX_HEADS].set(f_bias.astype(F32))
    head_of = np.arange(d) // FOX_HD
    gs_np = (head_of[:, None] == np.arange(128)[None, :]).astype(np.float32)
    gs = jnp.asarray(gs_np, BF16)
    gst = jnp.asarray(gs_np.T, BF16)
    tril = jnp.asarray(np.tril(np.ones((tm, tm), np.float32)), BF16)
    place_np = np.zeros((3, 128, d), np.float32)
    for hd in range(FOX_HEADS):
        for c in range(3):
            place_np[c, hd, (hd // 2) * 128 + 3 * (hd % 2) + c] = 1.0
    place = jnp.asarray(place_np, BF16)
    qg = jnp.tile(q_norm.astype(F32), FOX_HEADS).reshape(1, d)
    kg = jnp.tile(k_norm.astype(F32), FOX_HEADS).reshape(1, d)
    row = lambda i: (i, 0)
    return pl.pallas_call(
        functools.partial(_fox_proj_kernel, tiles_per_seq=seq // tm),
        grid=(n // tm,),
        in_specs=[pl.BlockSpec((tm, d), row), _full((1, d)), _full((d, 4 * d)), _full((d, d)),
                  _full((d, 128)), _full((1, 128)), _full((1, d)), _full((1, d)),
                  _full((d, 128)), _full((128, d)), _full((tm, tm)), _full((3, 128, d))],
        out_specs=[pl.BlockSpec((tm, d), row), pl.BlockSpec((tm, d), row), pl.BlockSpec((tm, 2 * d), row),
                   pl.BlockSpec((d, tm), lambda i: (0, i)), pl.BlockSpec((tm, 128), row)],
        out_shape=[jax.ShapeDtypeStruct((n, d), BF16), jax.ShapeDtypeStruct((n, d), BF16),
                   jax.ShapeDtypeStruct((n, 2 * d), BF16), jax.ShapeDtypeStruct((d, n), BF16),
                   jax.ShapeDtypeStruct((n, 128), F32)],
        scratch_shapes=[pltpu.VMEM((8, 128), F32)],
        compiler_params=_cparams(("arbitrary",)),
        name="fox_proj",
    )(h, g_mix.reshape(1, d), w_main, w_vt, w_f, fb, qg, kg, gs, gst, tril, place)


def _fox_attn_kernel(jlo_ref, q_ref, k_ref, vt_ref, g_ref, o_ref, qh_ref, m_ref, l_ref, acc_ref, *, tq, nq):
    qi = pl.program_id(2)
    lane = lax.broadcasted_iota(I32, (tq, 128), 1)
    qv = q_ref[...]
    for hh in range(2):
        own = (lane < FOX_HD) if hh == 0 else (lane >= FOX_HD)
        qh_ref[hh, :, 0:128] = jnp.where(own, qv, jnp.zeros_like(qv))
        pick = jnp.where(lane < 3 * hh, 0.0, jnp.where(lane < 3 * hh + 3, 1.0, 0.0))
        qh_ref[hh, :, 128:256] = pick.astype(BF16)
    m_ref[...] = jnp.full_like(m_ref, -jnp.inf)
    l_ref[...] = jnp.zeros_like(l_ref)
    acc_ref[...] = jnp.zeros_like(acc_ref)

    def sweep(kj, masked, nblk=1):
        rows = pl.ds(pl.multiple_of(kj * tq, tq), nblk * tq)
        kb = k_ref[rows, :]
        vt = vt_ref[:, rows]
        sts = [lax.dot_general(kb, qh_ref[hh], (((1,), (1,)), ((), ())), preferred_element_type=F32)
               for hh in range(2)]
        ps, alphas = [], []
        for hh in range(2):
            st = sts[hh]
            if masked:
                key = lax.broadcasted_iota(I32, st.shape, 0)
                qry = lax.broadcasted_iota(I32, st.shape, 1)
                st = jnp.where(key <= qry, st, -jnp.inf)
            m_old = m_ref[hh]
            m_new = jnp.maximum(m_old, jnp.max(st, axis=0, keepdims=True))
            alpha = jnp.exp(m_old - m_new)
            p = jnp.exp(st - m_new[0:1, :])
            l_ref[hh] = alpha * l_ref[hh] + jnp.sum(p, axis=0, keepdims=True)
            m_ref[hh] = m_new
            ps.append(p.astype(BF16))
            alphas.append(alpha[0:1, :])
        for hh in range(2):
            acc_ref[hh] = alphas[hh] * acc_ref[hh] + jnp.dot(
                vt[hh * FOX_HD:(hh + 1) * FOX_HD, :], ps[hh], preferred_element_type=F32)

    lo = jlo_ref[(pl.program_id(0) * pl.num_programs(1) + pl.program_id(1)) * nq + qi]
    span = qi - lo

    def body(i, carry):
        sweep(lo + 2 * i, False, 2)
        return carry

    lax.fori_loop(0, span // 2, body, 0)

    @pl.when(span % 2 == 1)
    def _():
        sweep(qi - 1, False)

    sweep(qi, True)
    ot = jnp.concatenate([acc_ref[0] / l_ref[0][0:1, :], acc_ref[1] / l_ref[1][0:1, :]], axis=0)
    gt = g_ref[...].astype(F32)
    o_ref[...] = (ot.T * jax.nn.sigmoid(gt)).astype(o_ref.dtype)


def _fox_first_block(cum_t, logit_bound, tq):
    bh, _, seq = cum_t.shape
    nq = seq // tq
    blk = cum_t.reshape(bh, 2, nq, tq)
    gap = blk[:, :, :, None, 0] - blk[:, :, None, :, tq - 1]
    dead = jnp.all(2.0 * logit_bound + gap < FOX_ZERO_EXP, axis=1)
    dead = dead & (jnp.arange(nq)[None, None, :] < jnp.arange(nq)[None, :, None])
    return jnp.sum(jnp.cumprod(dead.astype(I32), axis=-1), axis=-1).astype(I32).reshape(-1)


def _fox_attention(q, gate, ka, vt, cum, logit_bound, batch, seq, tq=512):
    n = batch * seq
    nq = seq // tq
    hp = FOX_HEADS // 2
    cum_t = cum[:, :FOX_HEADS].reshape(batch, seq, hp, 2).transpose(0, 2, 3, 1).reshape(batch * hp, 2, seq)
    jlo = _fox_first_block(cum_t, logit_bound, tq)
    grid_spec = pltpu.PrefetchScalarGridSpec(
        num_scalar_prefetch=1,
        grid=(batch, hp, nq),
        in_specs=[
            pl.BlockSpec((tq, 128), lambda b, h, i, jlo: (b * nq + i, h)),
            pl.BlockSpec((seq, 256), lambda b, h, i, jlo: (b, h)),
            pl.BlockSpec((128, seq), lambda b, h, i, jlo: (h, b)),
            pl.BlockSpec((tq, 128), lambda b, h, i, jlo: (b * nq + i, h)),
        ],
        out_specs=pl.BlockSpec((tq, 128), lambda b, h, i, jlo: (b * nq + i, h)),
        scratch_shapes=[pltpu.VMEM((2, tq, 256), BF16), pltpu.VMEM((2, 8, tq), F32),
                        pltpu.VMEM((2, 8, tq), F32), pltpu.VMEM((2, FOX_HD, tq), F32)],
    )
    return pl.pallas_call(
        functools.partial(_fox_attn_kernel, tq=tq, nq=nq),
        grid_spec=grid_spec,
        out_shape=jax.ShapeDtypeStruct((n, D_MODEL), BF16),
        compiler_params=_cparams(("parallel", "parallel", "arbitrary")),
        name="fox_attention",
    )(jlo, q, ka, vt, gate)


def _fox_mixer(h, g_mix, w_in, f_bias, q_norm, k_norm, batch, seq):
    q, gate, ka, vt, cum = _fox_proj(h, g_mix, w_in, f_bias, q_norm, k_norm, seq)
    logit_bound = 1.02 * FOX_HD ** 0.5 * jnp.max(jnp.abs(q_norm.astype(F32))) * jnp.max(jnp.abs(k_norm.astype(F32)))
    return _fox_attention(q, gate, ka, vt, cum, logit_bound, batch, seq)


def _rglru_kernel(gate_ref, u_ref, par_ref, wa_ref, wx_ref, o_ref, prev_ref, hc_ref, *, tt):
    @pl.when(pl.program_id(1) == 0)
    def _():
        prev_ref[...] = jnp.zeros_like(prev_ref)
        hc_ref[...] = jnp.zeros_like(hc_ref)

    u = u_ref[...]
    ext = jnp.concatenate([prev_ref[...], u], axis=0)
    conv = par_ref[4:5, :] + u * par_ref[3:4, :]
    for shift in range(1, CONV_W):
        conv = conv + pltpu.roll(ext, shift, 0)[8:, :] * par_ref[3 - shift:4 - shift, :]
    prev_ref[...] = u[tt - 8:, :]

    cb = conv.astype(BF16)
    ra, ia = [], []
    for nb in range(RG_BLOCKS):
        blk = cb[:, nb * RG_BW:(nb + 1) * RG_BW]
        ra.append(jnp.dot(blk, wa_ref[nb], preferred_element_type=F32))
        ia.append(jnp.dot(blk, wx_ref[nb], preferred_element_type=F32))
    r = jax.nn.sigmoid(jnp.concatenate(ra, axis=1) + par_ref[5:6, :])
    ig = jax.nn.sigmoid(jnp.concatenate(ia, axis=1) + par_ref[6:7, :])
    lam = par_ref[7:8, :]
    softplus = jnp.maximum(-lam, 0.0) + jnp.log1p(jnp.exp(-jnp.abs(lam)))
    log_a = -RG_C * r * softplus
    a = jnp.exp(log_a)
    b = jnp.sqrt(1.0 - a * a) * (ig * conv)

    a = a.reshape(tt // 8, 8, a.shape[-1])
    b = b.reshape(a.shape)
    within = lax.broadcasted_iota(I32, a.shape, 1)
    for dist in (1, 2, 4):
        ok = within >= dist
        a_sh = pltpu.roll(a, dist, 1)
        b_sh = pltpu.roll(b, dist, 1)
        b = jnp.where(ok, a * b_sh + b, b)
        a = jnp.where(ok, a * a_sh, a)
    a = a.reshape(tt, a.shape[-1])
    b = b.reshape(a.shape)
    gt = gate_ref[...]
    gelu = 0.5 * gt * (1.0 + jnp.tanh(0.7978845608028654 * (gt + 0.044715 * gt * gt * gt)))
    h = hc_ref[0:1, :]
    for g in range(tt // 8):
        sl = slice(g * 8, (g + 1) * 8)
        hg = b[sl] + a[sl] * h
        o_ref[sl, :] = (hg * gelu[sl]).astype(o_ref.dtype)
        h = hg[7:8, :]
    hc_ref[...] = jnp.broadcast_to(h, hc_ref.shape)


def _rglru_mixer(h, g_mix, w_in, conv_w, conv_b, w_a, b_a, w_x, b_x, lam, batch, seq, tt=256):
    n = batch * seq
    nt = seq // tt
    w = D_MODEL
    proj = _norm_proj(h, g_mix, w_in.astype(BF16))
    par = jnp.concatenate([conv_w.astype(F32), conv_b.reshape(1, w), b_a.reshape(1, w),
                           b_x.reshape(1, w), lam.reshape(1, w)], axis=0).astype(F32)
    return pl.pallas_call(
        functools.partial(_rglru_kernel, tt=tt),
        grid=(batch, nt),
        in_specs=[pl.BlockSpec((tt, w), lambda b, i: (b * nt + i, 0)),
                  pl.BlockSpec((tt, w), lambda b, i: (b * nt + i, 1)),
                  _full((8, w)), _full((RG_BLOCKS, RG_BW, RG_BW)), _full((RG_BLOCKS, RG_BW, RG_BW))],
        out_specs=pl.BlockSpec((tt, w), lambda b, i: (b * nt + i, 0)),
        out_shape=jax.ShapeDtypeStruct((n, w), BF16),
        scratch_shapes=[pltpu.VMEM((8, w), F32), pltpu.VMEM((8, w), F32)],
        compiler_params=_cparams(("parallel", "arbitrary")),
        name="rglru",
    )(proj, proj, par, w_a.astype(BF16), w_x.astype(BF16))


def _outproj_router_kernel(a_ref, h_ref, wo_ref, g_ref, wr_ref, br_ref, tri_ref, lst_ref,
                           hm_ref, xn_ref, gate_ref, pos_ref, cnt_ref):
    hm = h_ref[...] + jnp.dot(a_ref[...], wo_ref[...], preferred_element_type=F32)
    hm_ref[...] = hm
    xn = _rms(hm, g_ref[...])
    xn_ref[...] = xn.astype(xn_ref.dtype)
    logit = lax.dot_general(wr_ref[...], xn, (((1,), (1,)), ((), ())),
                            preferred_element_type=F32, precision=HI) + br_ref[:, 0:1]
    ne, tm = logit.shape
    eidx = lax.broadcasted_iota(I32, (ne, tm), 0)
    work = logit
    vals, hots = [], []
    for _ in range(TOP_K):
        mx = jnp.max(work, axis=0, keepdims=True)
        pick = jnp.min(jnp.where(work == mx, eidx, ne), axis=0, keepdims=True)
        hot = eidx == pick
        work = jnp.where(hot, -jnp.inf, work)
        vals.append(mx)
        hots.append(hot)
    ex = [jnp.exp(v - vals[0]) for v in vals]
    den = ex[0] + ex[1] + ex[2] + ex[3]
    gate_ref[...] = jnp.concatenate([e / den for e in ex], axis=0)
    tok_hot = (hots[0] | hots[1] | hots[2] | hots[3]).astype(BF16)
    before = jnp.dot(tok_hot, tri_ref[...], preferred_element_type=F32)
    count = jnp.sum(tok_hot.astype(F32), axis=1, keepdims=True)
    units = jnp.floor((count + (STRIP_ALIGN - 1)) * (1.0 / STRIP_ALIGN))
    units_b = jnp.broadcast_to(units, (ne, 128)).astype(BF16)
    start = STRIP_ALIGN * jnp.dot(lst_ref[...], units_b, preferred_element_type=F32)[:, 0:1]
    where = before + start
    pos_ref[...] = jnp.concatenate(
        [jnp.sum(jnp.where(h, where, 0.0), axis=0, keepdims=True) for h in hots], axis=0).astype(I32)
    cnt_ref[...] = jnp.broadcast_to(STRIP_ALIGN * units, cnt_ref.shape)


def _outproj_router(act, h, w_out, g_ffn, w_r, b_r):
    n, d = h.shape
    tm = MOE_TILE
    tri = jnp.asarray(np.triu(np.ones((tm, tm), np.float32), 1), BF16)
    lst = jnp.asarray(np.tril(np.ones((N_EXPERTS, N_EXPERTS), np.float32), -1), BF16)
    row = lambda i: (i, 0)
    colb = lambda i: (0, i)
    return pl.pallas_call(
        _outproj_router_kernel,
        grid=(n // tm,),
        in_specs=[pl.BlockSpec((tm, d), row), pl.BlockSpec((tm, d), row), _full((d, d)), _full((1, d)),
                  _full((N_EXPERTS, d)), _full((N_EXPERTS, 128)), _full((tm, tm)),
                  _full((N_EXPERTS, N_EXPERTS))],
        out_specs=[pl.BlockSpec((tm, d), row), pl.BlockSpec((tm, d), row),
                   pl.BlockSpec((TOP_K, tm), colb), pl.BlockSpec((TOP_K, tm), colb),
                   pl.BlockSpec((N_EXPERTS, 128), row)],
        out_shape=[jax.ShapeDtypeStruct((n, d), F32), jax.ShapeDtypeStruct((n, d), BF16),
                   jax.ShapeDtypeStruct((TOP_K, n), F32), jax.ShapeDtypeStruct((TOP_K, n), I32),
                   jax.ShapeDtypeStruct((n // tm * N_EXPERTS, 128), F32)],
        compiler_params=_cparams(("parallel",)),
        name="outproj_router",
    )(act, h, w_out.astype(BF16), g_ffn.reshape(1, d), w_r.T.astype(F32),
      jnp.broadcast_to(b_r.astype(F32)[:, None], (N_EXPERTS, 128)), tri, lst)


def _for_strip_chunks(nrows, max_rows, fn):
    size = max_rows
    while size >= STRIP_ALIGN:
        @pl.when((nrows & size) != 0)
        def _(size=size):
            fn(pl.multiple_of(nrows & ~(2 * size - 1), STRIP_ALIGN), size)
        size //= 2


def _for_tile_strips(n8_ref, sbase_ref, gdst_ref, tile, fn):
    def per_expert(e, carry):
        i = tile * N_EXPERTS + e
        so = sbase_ref[i]
        gd = gdst_ref[i]
        _for_strip_chunks(n8_ref[i], MOE_TILE,
                          lambda o, size: fn(pl.multiple_of(so + o, STRIP_ALIGN), pl.multiple_of(gd + o, STRIP_ALIGN), size))
        return carry

    lax.fori_loop(0, N_EXPERTS, per_expert, 0)


def _dispatch_kernel(n8_ref, sbase_ref, gdst_ref, tot_ref, pst_ref, x_ref, pos_ref, xs_hbm,
                     stage, zblk, rsem, zsem):
    tile = pl.program_id(0)
    slot = tile % 2
    xb = x_ref[...]
    for c in range(STAGE_ROWS // ONEHOT_CHUNK):
        rows = lax.broadcasted_iota(I32, (ONEHOT_CHUNK, MOE_TILE), 0) + c * ONEHOT_CHUNK
        onehot = jnp.zeros((ONEHOT_CHUNK, MOE_TILE), F32)
        for k in range(TOP_K):
            onehot = jnp.where(rows == pos_ref[k:k + 1, :], 1.0, onehot)
        stage[slot, c * ONEHOT_CHUNK:(c + 1) * ONEHOT_CHUNK, :] = jnp.dot(
            onehot.astype(BF16), xb, preferred_element_type=F32)

    def copy(sl, so, gd, size):
        return pltpu.make_async_copy(stage.at[sl, pl.ds(so, size), :], xs_hbm.at[pl.ds(gd, size), :], rsem.at[sl])

    _for_tile_strips(n8_ref, sbase_ref, gdst_ref, tile, lambda so, gd, size: copy(slot, so, gd, size).start())

    @pl.when(tile > 0)
    def _():
        _for_tile_strips(n8_ref, sbase_ref, gdst_ref, tile - 1,
                         lambda so, gd, size: copy(1 - slot, so, gd, size).wait())

    @pl.when(tile == 0)
    def _():
        zblk[...] = jnp.zeros_like(zblk)

        def zero(first, size):
            cp = pltpu.make_async_copy(zblk.at[pl.ds(0, size), :], xs_hbm.at[pl.ds(first, size), :], zsem)
            cp.start()
            cp.wait()

        def per_expert(e, carry):
            tot = tot_ref[e]
            first = pst_ref[e] + tot
            npad = ((tot + (MOE_BLK - 1)) // MOE_BLK) * MOE_BLK - tot
            _for_strip_chunks(npad, MOE_BLK // 2,
                              lambda o, size: zero(pl.multiple_of(first + o, STRIP_ALIGN), size))
            return carry

        lax.fori_loop(0, N_EXPERTS, per_expert, 0)

        last = N_EXPERTS - 1
        used = (pst_ref[last] + tot_ref[last] + (MOE_BLK - 1)) // MOE_BLK

        def ztail(b, c):
            zero(pl.multiple_of(b * MOE_BLK, MOE_BLK), MOE_BLK)
            return c

        lax.fori_loop(used, xs_hbm.shape[0] // MOE_BLK, ztail, 0)

    @pl.when(tile == pl.num_programs(0) - 1)
    def _():
        _for_tile_strips(n8_ref, sbase_ref, gdst_ref, tile, lambda so, gd, size: copy(slot, so, gd, size).wait())


def _dispatch(xn, pos, n8, sbase, gdst, total, pstart, cap):
    n, d = xn.shape
    tm = MOE_TILE
    grid_spec = pltpu.PrefetchScalarGridSpec(
        num_scalar_prefetch=5,
        grid=(n // tm,),
        in_specs=[pl.BlockSpec((tm, d), lambda i, *_: (i, 0)), pl.BlockSpec((TOP_K, tm), lambda i, *_: (0, i))],
        out_specs=pl.BlockSpec(memory_space=pl.ANY),
        scratch_shapes=[pltpu.VMEM((2, STAGE_ROWS, d), F32), pltpu.VMEM((MOE_BLK, d), F32),
                        pltpu.SemaphoreType.DMA((2,)), pltpu.SemaphoreType.DMA],
    )
    return pl.pallas_call(
        _dispatch_kernel,
        grid_spec=grid_spec,
        out_shape=jax.ShapeDtypeStruct((cap, d), F32),
        compiler_params=_cparams(("arbitrary",)),
        name="moe_dispatch",
    )(n8, sbase, gdst, total, pstart, xn, pos)


def _expert_kernel(be_ref, nu_ref, x_ref, wgu_ref, bgu_ref, wdn_ref, bdn_ref, y_ref, wgu_bf, wdn_bf):
    d = D_MODEL
    b = pl.program_id(0)

    @pl.when((b == 0) | (be_ref[b] != be_ref[jnp.maximum(b - 1, 0)]))
    def _():
        wgu_bf[...] = wgu_ref[...].astype(BF16)
        wdn_bf[...] = wdn_ref[...].astype(BF16)

    @pl.when(b < nu_ref[0])
    def _():
        xb = x_ref[...].astype(BF16)
        acc = None
        for c in range(d // EXPERT_CHUNK):
            lo, hi = c * EXPERT_CHUNK, (c + 1) * EXPERT_CHUNK
            glu = jnp.dot(xb, wgu_bf[:, lo:hi], preferred_element_type=F32) + bgu_ref[:, lo:hi]
            lin = jnp.dot(xb, wgu_bf[:, d + lo:d + hi], preferred_element_type=F32) + bgu_ref[:, d + lo:d + hi]
            glu = jnp.minimum(glu, SWIGLU_LIMIT)
            lin = jnp.clip(lin, -SWIGLU_LIMIT, SWIGLU_LIMIT)
            act = glu * jax.nn.sigmoid(SWIGLU_ALPHA * glu) * (lin + 1.0)
            part = jnp.dot(act.astype(BF16), wdn_bf[lo:hi, :], preferred_element_type=F32)
            acc = part if acc is None else acc + part
        y_ref[...] = acc + bdn_ref[...]

    @pl.when(b >= nu_ref[0])
    def _():
        y_ref[...] = jnp.zeros_like(y_ref)


def _experts(xs, block_expert, n_used, w_gu, b_gu, w_dn, b_dn):
    cap, d = xs.shape
    nb = cap // MOE_BLK
    ne = w_gu.shape[0]

    def xmap(b, be, nu):
        return (jnp.minimum(b, nu[0] - 1), 0)

    def wmap(b, be, nu):
        return (be[b], 0, 0)

    grid_spec = pltpu.PrefetchScalarGridSpec(
        num_scalar_prefetch=2,
        grid=(nb,),
        in_specs=[pl.BlockSpec((MOE_BLK, d), xmap),
                  pl.BlockSpec((None, d, 2 * d), wmap), pl.BlockSpec((None, 1, 2 * d), wmap),
                  pl.BlockSpec((None, d, d), wmap), pl.BlockSpec((None, 1, d), wmap)],
        out_specs=pl.BlockSpec((MOE_BLK, d), lambda b, be, nu: (b, 0)),
        scratch_shapes=[pltpu.VMEM((d, 2 * d), BF16), pltpu.VMEM((d, d), BF16)],
    )
    return pl.pallas_call(
        _expert_kernel,
        grid_spec=grid_spec,
        out_shape=jax.ShapeDtypeStruct((cap, d), F32),
        compiler_params=_cparams(("arbitrary",)),
        name="moe_experts",
    )(block_expert, n_used, xs, w_gu, b_gu.reshape(ne, 1, 2 * d), w_dn, b_dn.reshape(ne, 1, d))


def _combine_ple_kernel(n8_ref, sbase_ref, gdst_ref, y_hbm, hm_ref, gate_ref, pos_ref, p_ref, pw_ref, pn_ref,
                        gn_ref, gw_ref, o_ref, stage, rsem):
    tile = pl.program_id(0)
    slot = tile % 2

    def copy(sl, so, gd, size):
        return pltpu.make_async_copy(y_hbm.at[pl.ds(gd, size), :], stage.at[sl, pl.ds(so, size), :], rsem.at[sl])

    def fetch(t, sl):
        _for_tile_strips(n8_ref, sbase_ref, gdst_ref, t, lambda so, gd, size: copy(sl, so, gd, size).start())

    @pl.when(tile == 0)
    def _():
        stage[...] = jnp.zeros_like(stage)
        fetch(0, 0)

    @pl.when(tile + 1 < pl.num_programs(0))
    def _():
        fetch(tile + 1, 1 - slot)

    ple = _rms(jnp.dot(p_ref[...].astype(BF16), pw_ref[...], preferred_element_type=F32), pn_ref[...])
    _for_tile_strips(n8_ref, sbase_ref, gdst_ref, tile, lambda so, gd, size: copy(slot, so, gd, size).wait())
    h2 = hm_ref[...]
    for c in range(STAGE_ROWS // COMBINE_CHUNK):
        lanes = lax.broadcasted_iota(I32, (MOE_TILE, COMBINE_CHUNK), 1) + c * COMBINE_CHUNK
        weights = jnp.zeros((MOE_TILE, COMBINE_CHUNK), F32)
        for k in range(TOP_K):
            weights = jnp.where(lanes == pos_ref[:, k:k + 1], gate_ref[:, k:k + 1], weights)
        staged = stage[slot, c * COMBINE_CHUNK:(c + 1) * COMBINE_CHUNK, :].astype(BF16)
        h2 = h2 + jnp.dot(weights.astype(BF16), staged, preferred_element_type=F32)
    gate = jax.nn.sigmoid(jnp.dot(_rms(h2, gn_ref[...]).astype(BF16), gw_ref[...], preferred_element_type=F32))
    o_ref[...] = h2 + ple * gate


def _combine_ple(y, pos, n8, sbase, gdst, h_mid, gates, p, ple_w, ple_norm, gate_norm, gate_w):
    n, d = h_mid.shape
    tm = MOE_TILE
    row = lambda i, *_: (i, 0)
    full = lambda shape: pl.BlockSpec(shape, lambda i, *_: (0,) * len(shape))
    grid_spec = pltpu.PrefetchScalarGridSpec(
        num_scalar_prefetch=3,
        grid=(n // tm,),
        in_specs=[pl.BlockSpec(memory_space=pl.ANY), pl.BlockSpec((tm, d), row), pl.BlockSpec((tm, TOP_K), row),
                  pl.BlockSpec((tm, TOP_K), row), pl.BlockSpec((tm, PLE_DIM), row), full((PLE_DIM, d)),
                  full((1, d)), full((1, d)), full((d, d))],
        out_specs=pl.BlockSpec((tm, d), row),
        scratch_shapes=[pltpu.VMEM((2, STAGE_ROWS, d), F32), pltpu.SemaphoreType.DMA((2,))],
    )
    return pl.pallas_call(
        _combine_ple_kernel,
        grid_spec=grid_spec,
        out_shape=jax.ShapeDtypeStruct((n, d), F32),
        compiler_params=_cparams(("arbitrary",)),
        name="moe_combine_ple",
    )(n8, sbase, gdst, y, h_mid, gates.T, pos.T, p, ple_w.astype(BF16), ple_norm.reshape(1, d),
      gate_norm.reshape(1, d), gate_w.astype(BF16))


def _moe_ple(act, h, w_out, g_ffn, w_r, b_r, layer, w_gu, b_gu, w_dn, b_dn, p, ple_w, ple_norm, gate_norm, gate_w):
    n, d = h.shape
    ntiles = n // MOE_TILE
    h_mid, xn, gates, pos, cnt = _outproj_router(act, h, w_out, g_ffn, w_r, b_r)
    n8 = cnt[:, 0].astype(I32).reshape(ntiles, N_EXPERTS)
    sbase = jnp.cumsum(n8, axis=1) - n8
    total = jnp.sum(n8, axis=0)
    padded = (total + MOE_BLK - 1) // MOE_BLK * MOE_BLK
    pend = jnp.cumsum(padded)
    pstart = pend - padded
    gdst = pstart[None, :] + jnp.cumsum(n8, axis=0) - n8
    nb = -(-(n * TOP_K + ntiles * N_EXPERTS * (STRIP_ALIGN - 1)) // MOE_BLK) + N_EXPERTS
    cap = nb * MOE_BLK
    block_start = jnp.arange(nb, dtype=I32) * MOE_BLK
    block_expert = jnp.minimum(jnp.sum(pend[None, :] <= block_start[:, None], axis=1), N_EXPERTS - 1).astype(I32)
    n_used = (pend[-1:] // MOE_BLK).astype(I32)
    n8f, sbf, gdf = n8.reshape(-1), sbase.reshape(-1).astype(I32), gdst.reshape(-1).astype(I32)
    xs = _dispatch(xn, pos, n8f, sbf, gdf, total.astype(I32), pstart.astype(I32), cap)
    y = _experts(xs, block_expert + layer * N_EXPERTS, n_used, w_gu, b_gu, w_dn, b_dn)
    return _combine_ple(y, pos, n8f, sbf, gdf, h_mid, gates, p, ple_w, ple_norm, gate_norm, gate_w)


def kernel(x, p, norm_mix, norm_ffn, hg_w_in, hg_w_out, hg_gnorm, hg_lb_param, fox_w_in, fox_f_bias, fox_qnorm, fox_knorm, fox_w_out, rg_w_in, rg_conv_w, rg_conv_b, rg_wa, rg_ba, rg_wx, rg_bx, rg_lambda, rg_w_out, router_w, router_b, moe_w_gu, moe_b_gu, moe_w_dn, moe_b_dn, ple_w, ple_norm, ple_gate_norm, ple_gate_w):
    batch, seq, d = x.shape
    depth = p.shape[0]
    n = batch * seq
    lb_all = jnp.cumsum(jax.nn.softmax(hg_lb_param.astype(F32), axis=0), axis=0)
    lb_all = lb_all - lb_all[0]
    h = x.reshape(n, d).astype(F32)
    ne = depth * N_EXPERTS
    w_gu = moe_w_gu.astype(F32).reshape(ne, d, 2 * d)
    b_gu = moe_b_gu.astype(F32).reshape(ne, 2 * d)
    w_dn = moe_w_dn.astype(F32).reshape(ne, d, d)
    b_dn = moe_b_dn.astype(F32).reshape(ne, d)
    for i in range(depth):
        j = i // 3
        kind = i % 3
        if kind == 0:
            act = _hgrn2_mixer(h, norm_mix[i], hg_w_in[j], None, hg_gnorm[j], lb_all[i], batch, seq)
            w_out = hg_w_out[j]
        elif kind == 1:
            act = _fox_mixer(h, norm_mix[i], fox_w_in[j], fox_f_bias[j], fox_qnorm[j], fox_knorm[j], batch, seq)
            w_out = fox_w_out[j]
        else:
            act = _rglru_mixer(h, norm_mix[i], rg_w_in[j], rg_conv_w[j], rg_conv_b[j], rg_wa[j], rg_ba[j],
                               rg_wx[j], rg_bx[j], rg_lambda[j], batch, seq)
            w_out = rg_w_out[j]
        h = _moe_ple(act, h, w_out, norm_ffn[i], router_w[i], router_b[i], i, w_gu, b_gu, w_dn, b_dn,
                     p[i].reshape(n, PLE_DIM), ple_w[i], ple_norm[i], ple_gate_norm[i], ple_gate_w[i])
    return h.reshape(batch, seq, d)
```

```python
import functools

import jax
import jax.numpy as jnp
import numpy as np
from jax import lax
from jax.experimental import pallas as pl
from jax.experimental.pallas import tpu as pltpu

F32 = jnp.float32
BF16 = jnp.bfloat16
I32 = jnp.int32

D_MODEL = 1024
EPS = 1e-6
PLE_DIM = 256

HG_HEADS = 8
HG_DK = 128
HG_CHUNK = 128
HG_LEVELS = 7

FOX_HEADS = 16
FOX_HD = 64
FOX_ZERO_EXP = -105.0

RG_BLOCKS = 4
RG_BW = 256
CONV_W = 4
RG_C = 8.0

N_EXPERTS = 32
TOP_K = 4
SWIGLU_LIMIT = 7.0
SWIGLU_ALPHA = 1.702
MOE_BLK = 512
EXPERT_CHUNK = 512
MOE_TILE = 512
STRIP_ALIGN = 8
STAGE_ROWS = -(-(TOP_K * MOE_TILE + N_EXPERTS * (STRIP_ALIGN - 1)) // 256) * 256
ONEHOT_CHUNK = 256
COMBINE_CHUNK = STAGE_ROWS // 3

VMEM_LIMIT = 56 * 1024 * 1024

HI = lax.Precision.HIGHEST


def _cparams(sem):
    return pltpu.CompilerParams(dimension_semantics=sem, vmem_limit_bytes=VMEM_LIMIT)


def _bdot(a, b):
    return jnp.dot(a.astype(BF16), b.astype(BF16), preferred_element_type=F32)


def _bdot_nt(a, b):
    return lax.dot_general(a.astype(BF16), b.astype(BF16), (((1,), (1,)), ((), ())),
                           preferred_element_type=F32)


def _bdot_tn(a, b):
    return lax.dot_general(a.astype(BF16), b.astype(BF16), (((0,), (0,)), ((), ())),
                           preferred_element_type=F32)


def _rms(x, g):
    return x * lax.rsqrt(jnp.mean(x * x, axis=-1, keepdims=True) + EPS) * g


def _split2(x):
    hi = x.astype(BF16)
    lo = (x - hi.astype(F32)).astype(BF16)
    return hi, lo


def _split3(x):
    hi = x.astype(BF16)
    r = x - hi.astype(F32)
    mid = r.astype(BF16)
    lo = (r - mid.astype(F32)).astype(BF16)
    return hi, mid, lo


def _log_sigmoid(z):
    return jnp.minimum(z, 0.0) - jnp.log1p(jnp.exp(-jnp.abs(z)))


def _full(shape):
    return pl.BlockSpec(shape, lambda *_: (0,) * len(shape))


def _norm_proj_kernel(h_ref, g_ref, w_ref, o_ref, *, cn):
    xn = _rms(h_ref[...], g_ref[...]).astype(BF16)
    m = w_ref.shape[1]
    for c in range(m // cn):
        o_ref[:, c * cn:(c + 1) * cn] = jnp.dot(
            xn, w_ref[:, c * cn:(c + 1) * cn], preferred_element_type=F32).astype(o_ref.dtype)


def _norm_proj(h, g, w, tm=256, out_dtype=F32):
    n, d = h.shape
    m = w.shape[1]
    return pl.pallas_call(
        functools.partial(_norm_proj_kernel, cn=512),
        grid=(n // tm,),
        in_specs=[pl.BlockSpec((tm, d), lambda i: (i, 0)), _full((1, d)), _full((d, m))],
        out_specs=pl.BlockSpec((tm, m), lambda i: (i, 0)),
        out_shape=jax.ShapeDtypeStruct((n, m), out_dtype),
        compiler_params=_cparams(("parallel",)),
        name="norm_proj",
    )(h, g.reshape(1, d), w)


def _hgrn2_consts():
    c = HG_CHUNK
    t = np.arange(c)
    tril = (t[:, None] >= t[None, :]).astype(np.float32)
    sel = np.zeros((HG_LEVELS, c, c), np.float32)
    for l in range(HG_LEVELS):
        hs = 1 << l
        m = (t // (2 * hs)) * (2 * hs) + hs - 1
        sel[l, t, m] = 1.0
    return jnp.asarray(tril, BF16), jnp.asarray(sel.reshape(HG_LEVELS * c, c), BF16)


def _hgrn2_kernel(q_ref, z_ref, v_ref, g_ref, par_ref, tril_ref, sel_ref, o_ref, st_ref, *, nchunk):
    c = HG_CHUNK

    @pl.when(pl.program_id(2) == 0)
    def _():
        st_ref[...] = jnp.zeros_like(st_ref)

    log_lb = par_ref[0:1, :]
    log1m_lb = par_ref[1:2, :]
    one_m_lb = par_ref[2:3, :]
    gnorm = par_ref[3:4, :]
    row = lax.broadcasted_iota(I32, (c, c), 0)
    col = lax.broadcasted_iota(I32, (c, c), 1)

    cs = range(nchunk)
    rs = [slice(ci * c, (ci + 1) * c) for ci in cs]
    tril = tril_ref[...]
    q, k, vb, cum, refs, a = [], [], [], [], [], []
    for r in rs:
        qr = q_ref[r, :]
        z = z_ref[r, :]
        vb.append(v_ref[r, :].astype(BF16))
        q.append(qr * jax.nn.sigmoid(qr) * (HG_DK ** -0.5))
        b = log1m_lb + _log_sigmoid(z)
        lf = jnp.maximum(log_lb, b) + jnp.log1p(jnp.exp(-jnp.abs(log_lb - b)))
        k.append(one_m_lb * jax.nn.sigmoid(-z))
        hi, mid, lo = _split3(lf)
        cum.append(jnp.dot(tril, hi, preferred_element_type=F32)
                   + jnp.dot(tril, mid, preferred_element_type=F32)
                   + jnp.dot(tril, lo, preferred_element_type=F32))
    for i in cs:
        chi, clo = _split2(cum[i])
        refs.append(jnp.dot(sel_ref[...], chi, preferred_element_type=F32)
                    + jnp.dot(sel_ref[...], clo, preferred_element_type=F32))
        a.append(jnp.where(row == col, _bdot_nt(q[i], k[i]), 0.0))
    for l in range(HG_LEVELS):
        hs = 1 << l
        mask = (((row ^ col) >> l) == 1) & (row > col)
        for i in cs:
            ref = refs[i][l * c:(l + 1) * c, :]
            if hs >= 8:
                parts = []
                for blk in range(c // hs):
                    sl = slice(blk * hs, (blk + 1) * hs)
                    if blk % 2:
                        parts.append(q[i][sl] * jnp.exp(jnp.minimum(cum[i][sl] - ref[sl], 0.0)))
                    else:
                        parts.append(k[i][sl] * jnp.exp(jnp.minimum(ref[sl] - cum[i][sl], 0.0)))
                x = jnp.concatenate(parts, axis=0).astype(BF16)
            else:
                x = (jnp.where((row & hs) != 0, q[i], k[i]) * jnp.exp(-jnp.abs(cum[i] - ref))).astype(BF16)
            a[i] = jnp.where(mask, _bdot_nt(x, x), a[i])
    intra = [_bdot(a[i], vb[i]) for i in cs]
    last = [cum[i][c - 1:c, :] for i in cs]
    qe = [(q[i] * jnp.exp(cum[i])).astype(BF16) for i in cs]
    kd = [(k[i] * jnp.exp(last[i] - cum[i])).astype(BF16) for i in cs]
    st = st_ref[...]
    for i in cs:
        o = intra[i] + _bdot_nt(qe[i], st)
        st = st * jnp.exp(last[i]) + _bdot_tn(vb[i], kd[i])
        gt = g_ref[rs[i], :]
        y = _rms(o, gnorm) * (gt * jax.nn.sigmoid(gt))
        o_ref[rs[i], :] = y.astype(o_ref.dtype)
    st_ref[...] = st


def _hgrn2_recurrence(proj, par, batch, seq, tt=1024):
    n = batch * seq
    nt = seq // tt
    tril, sel = _hgrn2_consts()

    def part(p):
        return pl.BlockSpec((tt, HG_DK), lambda b, h, i, p=p: (b * nt + i, p * HG_HEADS + h))

    return pl.pallas_call(
        functools.partial(_hgrn2_kernel, nchunk=tt // HG_CHUNK),
        grid=(batch, HG_HEADS, nt),
        in_specs=[part(0), part(1), part(2), part(3),
                  pl.BlockSpec((8, HG_DK), lambda b, h, i: (0, h)),
                  _full(tril.shape), _full(sel.shape)],
        out_specs=pl.BlockSpec((tt, HG_DK), lambda b, h, i: (b * nt + i, h)),
        out_shape=jax.ShapeDtypeStruct((n, D_MODEL), BF16),
        scratch_shapes=[pltpu.VMEM((HG_DK, HG_DK), F32)],
        compiler_params=_cparams(("parallel", "parallel", "arbitrary")),
        name="hgrn2_recurrence",
    )(proj, proj, proj, proj, par, tril, sel)


def _hgrn2_mixer(h, g_mix, w_in, w_out_unused, g_norm, lb, batch, seq):
    del w_out_unused
    proj = _norm_proj(h, g_mix, w_in.astype(BF16))
    par = jnp.zeros((8, D_MODEL), F32)
    par = par.at[0].set(jnp.log(lb)).at[1].set(jnp.log1p(-lb)).at[2].set(1.0 - lb)
    par = par.at[3].set(jnp.tile(g_norm.astype(F32), HG_HEADS))
    return _hgrn2_recurrence(proj, par, batch, seq)


def _fox_proj_kernel(h_ref, g_ref, w_ref, wvt_ref, wf_ref, fb_ref, qg_ref, kg_ref, gs_ref, gst_ref, tril_ref,
                     place_ref, q_ref, gate_ref, ka_ref, vt_ref, cum_ref, carry_ref, *, tiles_per_seq):
    d = D_MODEL

    @pl.when(pl.program_id(0) % tiles_per_seq == 0)
    def _():
        carry_ref[...] = jnp.zeros_like(carry_ref)

    xn = _rms(h_ref[...], g_ref[...])
    xb = xn.astype(BF16)

    def headnorm(t, gain):
        shi, slo = _split2(t * t)
        ssq = (jnp.dot(shi, gs_ref[...], preferred_element_type=F32)
               + jnp.dot(slo, gs_ref[...], preferred_element_type=F32))
        inv = lax.rsqrt(ssq * (1.0 / FOX_HD) + EPS)
        ihi, ilo = _split2(inv)
        invf = (jnp.dot(ihi, gst_ref[...], preferred_element_type=F32)
                + jnp.dot(ilo, gst_ref[...], preferred_element_type=F32))
        return t * invf * gain

    q = jnp.dot(xb, w_ref[:, 0:d], preferred_element_type=F32)
    q_ref[...] = (headnorm(q, qg_ref[...]) * (FOX_HD ** -0.5)).astype(q_ref.dtype)
    k = jnp.dot(xb, w_ref[:, d:2 * d], preferred_element_type=F32)
    kn = headnorm(k, kg_ref[...]).astype(BF16)
    gate_ref[...] = jnp.dot(xb, w_ref[:, 3 * d:4 * d], preferred_element_type=F32).astype(gate_ref.dtype)
    vt_ref[...] = lax.dot_general(wvt_ref[...], xb, (((1,), (1,)), ((), ())),
                                  preferred_element_type=F32).astype(vt_ref.dtype)

    fl = jnp.dot(xn, wf_ref[...], preferred_element_type=F32, precision=HI) + fb_ref[...]
    hi, mid, lo = _split3(_log_sigmoid(fl))
    tril = tril_ref[...]
    cum = (jnp.dot(tril, hi, preferred_element_type=F32)
           + jnp.dot(tril, mid, preferred_element_type=F32)
           + jnp.dot(tril, lo, preferred_element_type=F32)) + carry_ref[0:1, :]
    cum_ref[...] = cum
    tm = cum.shape[0]
    carry_ref[...] = jnp.broadcast_to(cum[tm - 1:tm, :], carry_ref.shape)
    nhi, nmid, nlo = _split3(-cum)
    feat = (jnp.dot(nhi, place_ref[0], preferred_element_type=F32)
            + jnp.dot(nmid, place_ref[1], preferred_element_type=F32)
            + jnp.dot(nlo, place_ref[2], preferred_element_type=F32)).astype(BF16)
    for hp in range(FOX_HEADS // 2):
        ka_ref[:, hp * 256:hp * 256 + 128] = kn[:, hp * 128:(hp + 1) * 128]
        ka_ref[:, hp * 256 + 128:(hp + 1) * 256] = feat[:, hp * 128:(hp + 1) * 128]


def _fox_proj(h, g_mix, w_in, f_bias, q_norm, k_norm, seq, tm=256):
    n, d = h.shape
    w_main = w_in[:, :4 * d].astype(BF16)
    w_vt = w_in[:, 2 * d:3 * d].T.astype(BF16)
    w_f = jnp.zeros((d, 128), F32).at[:, :FOX_HEADS].set(w_in[:, 4 * d:].astype(F32))
    fb = jnp.zeros((1, 128), F32).at[0, :FOX_HEADS].set(f_bias.astype(F32))
    head_of = np.arange(d) // FOX_HD
    gs_np = (head_of[:, None] == np.arange(128)[None, :]).astype(np.float32)
    gs = jnp.asarray(gs_np, BF16)
    gst = jnp.asarray(gs_np.T, BF16)
    tril = jnp.asarray(np.tril(np.ones((tm, tm), np.float32)), BF16)
    place_np = np.zeros((3, 128, d), np.float32)
    for hd in range(FOX_HEADS):
        for c in range(3):
            place_np[c, hd, (hd // 2) * 128 + 3 * (hd % 2) + c] = 1.0
    place = jnp.asarray(place_np, BF16)
    qg = jnp.tile(q_norm.astype(F32), FOX_HEADS).reshape(1, d)
    kg = jnp.tile(k_norm.astype(F32), FOX_HEADS).reshape(1, d)
    row = lambda i: (i, 0)
    return pl.pallas_call(
        functools.partial(_fox_proj_kernel, tiles_per_seq=seq // tm),
        grid=(n // tm,),
        in_specs=[pl.BlockSpec((tm, d), row), _full((1, d)), _full((d, 4 * d)), _full((d, d)),
                  _full((d, 128)), _full((1, 128)), _full((1, d)), _full((1, d)),
                  _full((d, 128)), _full((128, d)), _full((tm, tm)), _full((3, 128, d))],
        out_specs=[pl.BlockSpec((tm, d), row), pl.BlockSpec((tm, d), row), pl.BlockSpec((tm, 2 * d), row),
                   pl.BlockSpec((d, tm), lambda i: (0, i)), pl.BlockSpec((tm, 128), row)],
        out_shape=[jax.ShapeDtypeStruct((n, d), BF16), jax.ShapeDtypeStruct((n, d), BF16),
                   jax.ShapeDtypeStruct((n, 2 * d), BF16), jax.ShapeDtypeStruct((d, n), BF16),
                   jax.ShapeDtypeStruct((n, 128), F32)],
        scratch_shapes=[pltpu.VMEM((8, 128), F32)],
        compiler_params=_cparams(("arbitrary",)),
        name="fox_proj",
    )(h, g_mix.reshape(1, d), w_main, w_vt, w_f, fb, qg, kg, gs, gst, tril, place)


def _fox_attn_kernel(jlo_ref, q_ref, k_ref, vt_ref, g_ref, o_ref, qh_ref, m_ref, l_ref, acc_ref, *, tq, nq):
    qi = pl.program_id(2)
    lane = lax.broadcasted_iota(I32, (tq, 128), 1)
    qv = q_ref[...]
    for hh in range(2):
        own = (lane < FOX_HD) if hh == 0 else (lane >= FOX_HD)
        qh_ref[hh, :, 0:128] = jnp.where(own, qv, jnp.zeros_like(qv))
        pick = jnp.where(lane < 3 * hh, 0.0, jnp.where(lane < 3 * hh + 3, 1.0, 0.0))
        qh_ref[hh, :, 128:256] = pick.astype(BF16)
    m_ref[...] = jnp.full_like(m_ref, -jnp.inf)
    l_ref[...] = jnp.zeros_like(l_ref)
    acc_ref[...] = jnp.zeros_like(acc_ref)

    def sweep(kj, masked, nblk=1):
        rows = pl.ds(pl.multiple_of(kj * tq, tq), nblk * tq)
        kb = k_ref[rows, :]
        vt = vt_ref[:, rows]
        sts = [lax.dot_general(kb, qh_ref[hh], (((1,), (1,)), ((), ())), preferred_element_type=F32)
               for hh in range(2)]
        ps, alphas = [], []
        for hh in range(2):
            st = sts[hh]
            if masked:
                key = lax.broadcasted_iota(I32, st.shape, 0)
                qry = lax.broadcasted_iota(I32, st.shape, 1)
                st = jnp.where(key <= qry, st, -jnp.inf)
            m_old = m_ref[hh]
            m_new = jnp.maximum(m_old, jnp.max(st, axis=0, keepdims=True))
            alpha = jnp.exp(m_old - m_new)
            p = jnp.exp(st - m_new[0:1, :])
            l_ref[hh] = alpha * l_ref[hh] + jnp.sum(p, axis=0, keepdims=True)
            m_ref[hh] = m_new
            ps.append(p.astype(BF16))
            alphas.append(alpha[0:1, :])
        for hh in range(2):
            acc_ref[hh] = alphas[hh] * acc_ref[hh] + jnp.dot(
                vt[hh * FOX_HD:(hh + 1) * FOX_HD, :], ps[hh], preferred_element_type=F32)

    lo = jlo_ref[(pl.program_id(0) * pl.num_programs(1) + pl.program_id(1)) * nq + qi]
    span = qi - lo

    def body(i, carry):
        sweep(lo + 2 * i, False, 2)
        return carry

    lax.fori_loop(0, span // 2, body, 0)

    @pl.when(span % 2 == 1)
    def _():
        sweep(qi - 1, False)

    sweep(qi, True)
    ot = jnp.concatenate([acc_ref[0] / l_ref[0][0:1, :], acc_ref[1] / l_ref[1][0:1, :]], axis=0)
    gt = g_ref[...].astype(F32)
    o_ref[...] = (ot.T * jax.nn.sigmoid(gt)).astype(o_ref.dtype)


def _fox_first_block(cum_t, logit_bound, tq):
    bh, _, seq = cum_t.shape
    nq = seq // tq
    blk = cum_t.reshape(bh, 2, nq, tq)
    gap = blk[:, :, :, None, 0] - blk[:, :, None, :, tq - 1]
    dead = jnp.all(2.0 * logit_bound + gap < FOX_ZERO_EXP, axis=1)
    dead = dead & (jnp.arange(nq)[None, None, :] < jnp.arange(nq)[None, :, None])
    return jnp.sum(jnp.cumprod(dead.astype(I32), axis=-1), axis=-1).astype(I32).reshape(-1)


def _fox_attention(q, gate, ka, vt, cum, logit_bound, batch, seq, tq=512):
    n = batch * seq
    nq = seq // tq
    hp = FOX_HEADS // 2
    cum_t = cum[:, :FOX_HEADS].reshape(batch, seq, hp, 2).transpose(0, 2, 3, 1).reshape(batch * hp, 2, seq)
    jlo = _fox_first_block(cum_t, logit_bound, tq)
    grid_spec = pltpu.PrefetchScalarGridSpec(
        num_scalar_prefetch=1,
        grid=(batch, hp, nq),
        in_specs=[
            pl.BlockSpec((tq, 128), lambda b, h, i, jlo: (b * nq + i, h)),
            pl.BlockSpec((seq, 256), lambda b, h, i, jlo: (b, h)),
            pl.BlockSpec((128, seq), lambda b, h, i, jlo: (h, b)),
            pl.BlockSpec((tq, 128), lambda b, h, i, jlo: (b * nq + i, h)),
        ],
        out_specs=pl.BlockSpec((tq, 128), lambda b, h, i, jlo: (b * nq + i, h)),
        scratch_shapes=[pltpu.VMEM((2, tq, 256), BF16), pltpu.VMEM((2, 8, tq), F32),
                        pltpu.VMEM((2, 8, tq), F32), pltpu.VMEM((2, FOX_HD, tq), F32)],
    )
    return pl.pallas_call(
        functools.partial(_fox_attn_kernel, tq=tq, nq=nq),
        grid_spec=grid_spec,
        out_shape=jax.ShapeDtypeStruct((n, D_MODEL), BF16),
        compiler_params=_cparams(("parallel", "parallel", "arbitrary")),
        name="fox_attention",
    )(jlo, q, ka, vt, gate)


def _fox_mixer(h, g_mix, w_in, f_bias, q_norm, k_norm, batch, seq):
    q, gate, ka, vt, cum = _fox_proj(h, g_mix, w_in, f_bias, q_norm, k_norm, seq)
    logit_bound = 1.02 * FOX_HD ** 0.5 * jnp.max(jnp.abs(q_norm.astype(F32))) * jnp.max(jnp.abs(k_norm.astype(F32)))
    return _fox_attention(q, gate, ka, vt, cum, logit_bound, batch, seq)


def _rglru_kernel(gate_ref, u_ref, par_ref, wa_ref, wx_ref, o_ref, prev_ref, hc_ref, *, tt):
    @pl.when(pl.program_id(1) == 0)
    def _():
        prev_ref[...] = jnp.zeros_like(prev_ref)
        hc_ref[...] = jnp.zeros_like(hc_ref)

    u = u_ref[...]
    ext = jnp.concatenate([prev_ref[...], u], axis=0)
    conv = par_ref[4:5, :] + u * par_ref[3:4, :]
    for shift in range(1, CONV_W):
        conv = conv + pltpu.roll(ext, shift, 0)[8:, :] * par_ref[3 - shift:4 - shift, :]
    prev_ref[...] = u[tt - 8:, :]

    cb = conv.astype(BF16)
    ra, ia = [], []
    for nb in range(RG_BLOCKS):
        blk = cb[:, nb * RG_BW:(nb + 1) * RG_BW]
        ra.append(jnp.dot(blk, wa_ref[nb], preferred_element_type=F32))
        ia.append(jnp.dot(blk, wx_ref[nb], preferred_element_type=F32))
    r = jax.nn.sigmoid(jnp.concatenate(ra, axis=1) + par_ref[5:6, :])
    ig = jax.nn.sigmoid(jnp.concatenate(ia, axis=1) + par_ref[6:7, :])
    lam = par_ref[7:8, :]
    softplus = jnp.maximum(-lam, 0.0) + jnp.log1p(jnp.exp(-jnp.abs(lam)))
    log_a = -RG_C * r * softplus
    a = jnp.exp(log_a)
    b = jnp.sqrt(1.0 - a * a) * (ig * conv)

    a = a.reshape(tt // 8, 8, a.shape[-1])
    b = b.reshape(a.shape)
    within = lax.broadcasted_iota(I32, a.shape, 1)
    for dist in (1, 2, 4):
        ok = within >= dist
        a_sh = pltpu.roll(a, dist, 1)
        b_sh = pltpu.roll(b, dist, 1)
        b = jnp.where(ok, a * b_sh + b, b)
        a = jnp.where(ok, a * a_sh, a)
    a = a.reshape(tt, a.shape[-1])
    b = b.reshape(a.shape)
    gt = gate_ref[...]
    gelu = 0.5 * gt * (1.0 + jnp.tanh(0.7978845608028654 * (gt + 0.044715 * gt * gt * gt)))
    h = hc_ref[0:1, :]
    for g in range(tt // 8):
        sl = slice(g * 8, (g + 1) * 8)
        hg = b[sl] + a[sl] * h
        o_ref[sl, :] = (hg * gelu[sl]).astype(o_ref.dtype)
        h = hg[7:8, :]
    hc_ref[...] = jnp.broadcast_to(h, hc_ref.shape)


def _rglru_mixer(h, g_mix, w_in, conv_w, conv_b, w_a, b_a, w_x, b_x, lam, batch, seq, tt=256):
    n = batch * seq
    nt = seq // tt
    w = D_MODEL
    proj = _norm_proj(h, g_mix, w_in.astype(BF16))
    par = jnp.concatenate([conv_w.astype(F32), conv_b.reshape(1, w), b_a.reshape(1, w),
                           b_x.reshape(1, w), lam.reshape(1, w)], axis=0).astype(F32)
    return pl.pallas_call(
        functools.partial(_rglru_kernel, tt=tt),
        grid=(batch, nt),
        in_specs=[pl.BlockSpec((tt, w), lambda b, i: (b * nt + i, 0)),
                  pl.BlockSpec((tt, w), lambda b, i: (b * nt + i, 1)),
                  _full((8, w)), _full((RG_BLOCKS, RG_BW, RG_BW)), _full((RG_BLOCKS, RG_BW, RG_BW))],
        out_specs=pl.BlockSpec((tt, w), lambda b, i: (b * nt + i, 0)),
        out_shape=jax.ShapeDtypeStruct((n, w), BF16),
        scratch_shapes=[pltpu.VMEM((8, w), F32), pltpu.VMEM((8, w), F32)],
        compiler_params=_cparams(("parallel", "arbitrary")),
        name="rglru",
    )(proj, proj, par, w_a.astype(BF16), w_x.astype(BF16))


def _outproj_router_kernel(a_ref, h_ref, wo_ref, g_ref, wr_ref, br_ref, tri_ref, lst_ref,
                           hm_ref, xn_ref, gate_ref, pos_ref, cnt_ref):
    hm = h_ref[...] + jnp.dot(a_ref[...], wo_ref[...], preferred_element_type=F32)
    hm_ref[...] = hm
    xn = _rms(hm, g_ref[...])
    xn_ref[...] = xn.astype(xn_ref.dtype)
    logit = lax.dot_general(wr_ref[...], xn, (((1,), (1,)), ((), ())),
                            preferred_element_type=F32, precision=HI) + br_ref[:, 0:1]
    ne, tm = logit.shape
    eidx = lax.broadcasted_iota(I32, (ne, tm), 0)
    work = logit
    vals, hots = [], []
    for _ in range(TOP_K):
        mx = jnp.max(work, axis=0, keepdims=True)
        pick = jnp.min(jnp.where(work == mx, eidx, ne), axis=0, keepdims=True)
        hot = eidx == pick
        work = jnp.where(hot, -jnp.inf, work)
        vals.append(mx)
        hots.append(hot)
    ex = [jnp.exp(v - vals[0]) for v in vals]
    den = ex[0] + ex[1] + ex[2] + ex[3]
    gate_ref[...] = jnp.concatenate([e / den for e in ex], axis=0)
    tok_hot = (hots[0] | hots[1] | hots[2] | hots[3]).astype(BF16)
    before = jnp.dot(tok_hot, tri_ref[...], preferred_element_type=F32)
    count = jnp.sum(tok_hot.astype(F32), axis=1, keepdims=True)
    units = jnp.floor((count + (STRIP_ALIGN - 1)) * (1.0 / STRIP_ALIGN))
    units_b = jnp.broadcast_to(units, (ne, 128)).astype(BF16)
    start = STRIP_ALIGN * jnp.dot(lst_ref[...], units_b, preferred_element_type=F32)[:, 0:1]
    where = before + start
    pos_ref[...] = jnp.concatenate(
        [jnp.sum(jnp.where(h, where, 0.0), axis=0, keepdims=True) for h in hots], axis=0).astype(I32)
    cnt_ref[...] = jnp.broadcast_to(STRIP_ALIGN * units, cnt_ref.shape)


def _outproj_router(act, h, w_out, g_ffn, w_r, b_r):
    n, d = h.shape
    tm = MOE_TILE
    tri = jnp.asarray(np.triu(np.ones((tm, tm), np.float32), 1), BF16)
    lst = jnp.asarray(np.tril(np.ones((N_EXPERTS, N_EXPERTS), np.float32), -1), BF16)
    row = lambda i: (i, 0)
    colb = lambda i: (0, i)
    return pl.pallas_call(
        _outproj_router_kernel,
        grid=(n // tm,),
        in_specs=[pl.BlockSpec((tm, d), row), pl.BlockSpec((tm, d), row), _full((d, d)), _full((1, d)),
                  _full((N_EXPERTS, d)), _full((N_EXPERTS, 128)), _full((tm, tm)),
                  _full((N_EXPERTS, N_EXPERTS))],
        out_specs=[pl.BlockSpec((tm, d), row), pl.BlockSpec((tm, d), row),
                   pl.BlockSpec((TOP_K, tm), colb), pl.BlockSpec((TOP_K, tm), colb),
                   pl.BlockSpec((N_EXPERTS, 128), row)],
        out_shape=[jax.ShapeDtypeStruct((n, d), F32), jax.ShapeDtypeStruct((n, d), BF16),
                   jax.ShapeDtypeStruct((TOP_K, n), F32), jax.ShapeDtypeStruct((TOP_K, n), I32),
                   jax.ShapeDtypeStruct((n // tm * N_EXPERTS, 128), F32)],
        compiler_params=_cparams(("parallel",)),
        name="outproj_router",
    )(act, h, w_out.astype(BF16), g_ffn.reshape(1, d), w_r.T.astype(F32),
      jnp.broadcast_to(b_r.astype(F32)[:, None], (N_EXPERTS, 128)), tri, lst)


def _for_strip_chunks(nrows, max_rows, fn):
    size = max_rows
    while size >= STRIP_ALIGN:
        @pl.when((nrows & size) != 0)
        def _(size=size):
            fn(pl.multiple_of(nrows & ~(2 * size - 1), STRIP_ALIGN), size)
        size //= 2


def _for_tile_strips(n8_ref, sbase_ref, gdst_ref, tile, fn):
    def per_expert(e, carry):
        i = tile * N_EXPERTS + e
        so = sbase_ref[i]
        gd = gdst_ref[i]
        _for_strip_chunks(n8_ref[i], MOE_TILE,
                          lambda o, size: fn(pl.multiple_of(so + o, STRIP_ALIGN), pl.multiple_of(gd + o, STRIP_ALIGN), size))
        return carry

    lax.fori_loop(0, N_EXPERTS, per_expert, 0)


def _wait_tile_strips(n8_ref, sbase_ref, tile, wait_rows):
    last = tile * N_EXPERTS + (N_EXPERTS - 1)
    total = sbase_ref[last] + n8_ref[last]
    size = STAGE_ROWS.bit_length() - 1
    size = 1 << size
    while size >= STRIP_ALIGN:
        @pl.when((total & size) != 0)
        def _(size=size):
            wait_rows(size)
        size //= 2


def _dispatch_kernel(n8_ref, sbase_ref, gdst_ref, tot_ref, pst_ref, x_ref, pos_ref, xs_hbm,
                     stage, zblk, rsem, zsem):
    tile = pl.program_id(0)
    slot = tile % 2
    xb = x_ref[...]
    for c in range(STAGE_ROWS // ONEHOT_CHUNK):
        rows = lax.broadcasted_iota(I32, (ONEHOT_CHUNK, MOE_TILE), 0) + c * ONEHOT_CHUNK
        onehot = jnp.zeros((ONEHOT_CHUNK, MOE_TILE), F32)
        for k in range(TOP_K):
            onehot = jnp.where(rows == pos_ref[k:k + 1, :], 1.0, onehot)
        stage[slot, c * ONEHOT_CHUNK:(c + 1) * ONEHOT_CHUNK, :] = jnp.dot(
            onehot.astype(BF16), xb, preferred_element_type=F32)

    def copy(sl, so, gd, size):
        return pltpu.make_async_copy(stage.at[sl, pl.ds(so, size), :], xs_hbm.at[pl.ds(gd, size), :], rsem.at[sl])

    _for_tile_strips(n8_ref, sbase_ref, gdst_ref, tile, lambda so, gd, size: copy(slot, so, gd, size).start())

    @pl.when(tile > 0)
    def _():
        _wait_tile_strips(n8_ref, sbase_ref, tile - 1, lambda size: copy(1 - slot, 0, 0, size).wait())

    @pl.when(tile == 0)
    def _():
        zblk[...] = jnp.zeros_like(zblk)

        def zero(first, size):
            cp = pltpu.make_async_copy(zblk.at[pl.ds(0, size), :], xs_hbm.at[pl.ds(first, size), :], zsem)
            cp.start()
            cp.wait()

        def per_expert(e, carry):
            tot = tot_ref[e]
            first = pst_ref[e] + tot
            npad = ((tot + (MOE_BLK - 1)) // MOE_BLK) * MOE_BLK - tot
            _for_strip_chunks(npad, MOE_BLK // 2,
                              lambda o, size: zero(pl.multiple_of(first + o, STRIP_ALIGN), size))
            return carry

        lax.fori_loop(0, N_EXPERTS, per_expert, 0)

        last = N_EXPERTS - 1
        used = (pst_ref[last] + tot_ref[last] + (MOE_BLK - 1)) // MOE_BLK

        def ztail(b, c):
            zero(pl.multiple_of(b * MOE_BLK, MOE_BLK), MOE_BLK)
            return c

        lax.fori_loop(used, xs_hbm.shape[0] // MOE_BLK, ztail, 0)

    @pl.when(tile == pl.num_programs(0) - 1)
    def _():
        _wait_tile_strips(n8_ref, sbase_ref, tile, lambda size: copy(slot, 0, 0, size).wait())


def _dispatch(xn, pos, n8, sbase, gdst, total, pstart, cap):
    n, d = xn.shape
    tm = MOE_TILE
    grid_spec = pltpu.PrefetchScalarGridSpec(
        num_scalar_prefetch=5,
        grid=(n // tm,),
        in_specs=[pl.BlockSpec((tm, d), lambda i, *_: (i, 0)), pl.BlockSpec((TOP_K, tm), lambda i, *_: (0, i))],
        out_specs=pl.BlockSpec(memory_space=pl.ANY),
        scratch_shapes=[pltpu.VMEM((2, STAGE_ROWS, d), F32), pltpu.VMEM((MOE_BLK, d), F32),
                        pltpu.SemaphoreType.DMA((2,)), pltpu.SemaphoreType.DMA],
    )
    return pl.pallas_call(
        _dispatch_kernel,
        grid_spec=grid_spec,
        out_shape=jax.ShapeDtypeStruct((cap, d), F32),
        compiler_params=_cparams(("arbitrary",)),
        name="moe_dispatch",
    )(n8, sbase, gdst, total, pstart, xn, pos)


def _expert_kernel(be_ref, nu_ref, x_ref, wgu_ref, bgu_ref, wdn_ref, bdn_ref, y_ref, wgu_bf, wdn_bf):
    d = D_MODEL
    b = pl.program_id(0)

    @pl.when((b == 0) | (be_ref[b] != be_ref[jnp.maximum(b - 1, 0)]))
    def _():
        wgu_bf[...] = wgu_ref[...].astype(BF16)
        wdn_bf[...] = wdn_ref[...].astype(BF16)

    @pl.when(b < nu_ref[0])
    def _():
        xb = x_ref[...].astype(BF16)
        acc = None
        for c in range(d // EXPERT_CHUNK):
            lo, hi = c * EXPERT_CHUNK, (c + 1) * EXPERT_CHUNK
            glu = jnp.dot(xb, wgu_bf[:, lo:hi], preferred_element_type=F32) + bgu_ref[:, lo:hi]
            lin = jnp.dot(xb, wgu_bf[:, d + lo:d + hi], preferred_element_type=F32) + bgu_ref[:, d + lo:d + hi]
            glu = jnp.minimum(glu, SWIGLU_LIMIT)
            lin = jnp.clip(lin, -SWIGLU_LIMIT, SWIGLU_LIMIT)
            act = glu * jax.nn.sigmoid(SWIGLU_ALPHA * glu) * (lin + 1.0)
            part = jnp.dot(act.astype(BF16), wdn_bf[lo:hi, :], preferred_element_type=F32)
            acc = part if acc is None else acc + part
        y_ref[...] = acc + bdn_ref[...]

    @pl.when(b >= nu_ref[0])
    def _():
        y_ref[...] = jnp.zeros_like(y_ref)


def _experts(xs, block_expert, n_used, w_gu, b_gu, w_dn, b_dn):
    cap, d = xs.shape
    nb = cap // MOE_BLK
    ne = w_gu.shape[0]

    def xmap(b, be, nu):
        return (jnp.minimum(b, nu[0] - 1), 0)

    def wmap(b, be, nu):
        return (be[b], 0, 0)

    grid_spec = pltpu.PrefetchScalarGridSpec(
        num_scalar_prefetch=2,
        grid=(nb,),
        in_specs=[pl.BlockSpec((MOE_BLK, d), xmap),
                  pl.BlockSpec((None, d, 2 * d), wmap), pl.BlockSpec((None, 1, 2 * d), wmap),
                  pl.BlockSpec((None, d, d), wmap), pl.BlockSpec((None, 1, d), wmap)],
        out_specs=pl.BlockSpec((MOE_BLK, d), lambda b, be, nu: (b, 0)),
        scratch_shapes=[pltpu.VMEM((d, 2 * d), BF16), pltpu.VMEM((d, d), BF16)],
    )
    return pl.pallas_call(
        _expert_kernel,
        grid_spec=grid_spec,
        out_shape=jax.ShapeDtypeStruct((cap, d), F32),
        compiler_params=_cparams(("arbitrary",)),
        name="moe_experts",
    )(block_expert, n_used, xs, w_gu, b_gu.reshape(ne, 1, 2 * d), w_dn, b_dn.reshape(ne, 1, d))


def _combine_ple_kernel(n8_ref, sbase_ref, gdst_ref, y_hbm, hm_ref, gate_ref, pos_ref, p_ref, pw_ref, pn_ref,
                        gn_ref, gw_ref, o_ref, stage, rsem):
    tile = pl.program_id(0)
    slot = tile % 2

    def copy(sl, so, gd, size):
        return pltpu.make_async_copy(y_hbm.at[pl.ds(gd, size), :], stage.at[sl, pl.ds(so, size), :], rsem.at[sl])

    def fetch(t, sl):
        _for_tile_strips(n8_ref, sbase_ref, gdst_ref, t, lambda so, gd, size: copy(sl, so, gd, size).start())

    @pl.when(tile == 0)
    def _():
        stage[...] = jnp.zeros_like(stage)
        fetch(0, 0)

    @pl.when(tile + 1 < pl.num_programs(0))
    def _():
        fetch(tile + 1, 1 - slot)

    ple = _rms(jnp.dot(p_ref[...].astype(BF16), pw_ref[...], preferred_element_type=F32), pn_ref[...])
    _wait_tile_strips(n8_ref, sbase_ref, tile, lambda size: copy(slot, 0, 0, size).wait())
    h2 = hm_ref[...]
    for c in range(STAGE_ROWS // COMBINE_CHUNK):
        lanes = lax.broadcasted_iota(I32, (MOE_TILE, COMBINE_CHUNK), 1) + c * COMBINE_CHUNK
        weights = jnp.zeros((MOE_TILE, COMBINE_CHUNK), F32)
        for k in range(TOP_K):
            weights = jnp.where(lanes == pos_ref[:, k:k + 1], gate_ref[:, k:k + 1], weights)
        staged = stage[slot, c * COMBINE_CHUNK:(c + 1) * COMBINE_CHUNK, :].astype(BF16)
        h2 = h2 + jnp.dot(weights.astype(BF16), staged, preferred_element_type=F32)
    gate = jax.nn.sigmoid(jnp.dot(_rms(h2, gn_ref[...]).astype(BF16), gw_ref[...], preferred_element_type=F32))
    o_ref[...] = h2 + ple * gate


def _combine_ple(y, pos, n8, sbase, gdst, h_mid, gates, p, ple_w, ple_norm, gate_norm, gate_w):
    n, d = h_mid.shape
    tm = MOE_TILE
    row = lambda i, *_: (i, 0)
    full = lambda shape: pl.BlockSpec(shape, lambda i, *_: (0,) * len(shape))
    grid_spec = pltpu.PrefetchScalarGridSpec(
        num_scalar_prefetch=3,
        grid=(n // tm,),
        in_specs=[pl.BlockSpec(memory_space=pl.ANY), pl.BlockSpec((tm, d), row), pl.BlockSpec((tm, TOP_K), row),
                  pl.BlockSpec((tm, TOP_K), row), pl.BlockSpec((tm, PLE_DIM), row), full((PLE_DIM, d)),
                  full((1, d)), full((1, d)), full((d, d))],
        out_specs=pl.BlockSpec((tm, d), row),
        scratch_shapes=[pltpu.VMEM((2, STAGE_ROWS, d), F32), pltpu.SemaphoreType.DMA((2,))],
    )
    return pl.pallas_call(
        _combine_ple_kernel,
        grid_spec=grid_spec,
        out_shape=jax.ShapeDtypeStruct((n, d), F32),
        compiler_params=_cparams(("arbitrary",)),
        name="moe_combine_ple",
    )(n8, sbase, gdst, y, h_mid, gates.T, pos.T, p, ple_w.astype(BF16), ple_norm.reshape(1, d),
      gate_norm.reshape(1, d), gate_w.astype(BF16))


def _moe_ple(act, h, w_out, g_ffn, w_r, b_r, layer, w_gu, b_gu, w_dn, b_dn, p, ple_w, ple_norm, gate_norm, gate_w):
    n, d = h.shape
    ntiles = n // MOE_TILE
    h_mid, xn, gates, pos, cnt = _outproj_router(act, h, w_out, g_ffn, w_r, b_r)
    n8 = cnt[:, 0].astype(I32).reshape(ntiles, N_EXPERTS)
    sbase = jnp.cumsum(n8, axis=1) - n8
    total = jnp.sum(n8, axis=0)
    padded = (total + MOE_BLK - 1) // MOE_BLK * MOE_BLK
    pend = jnp.cumsum(padded)
    pstart = pend - padded
    gdst = pstart[None, :] + jnp.cumsum(n8, axis=0) - n8
    nb = -(-(n * TOP_K + ntiles * N_EXPERTS * (STRIP_ALIGN - 1)) // MOE_BLK) + N_EXPERTS
    cap = nb * MOE_BLK
    block_start = jnp.arange(nb, dtype=I32) * MOE_BLK
    block_expert = jnp.minimum(jnp.sum(pend[None, :] <= block_start[:, None], axis=1), N_EXPERTS - 1).astype(I32)
    n_used = (pend[-1:] // MOE_BLK).astype(I32)
    n8f, sbf, gdf = n8.reshape(-1), sbase.reshape(-1).astype(I32), gdst.reshape(-1).astype(I32)
    xs = _dispatch(xn, pos, n8f, sbf, gdf, total.astype(I32), pstart.astype(I32), cap)
    y = _experts(xs, block_expert + layer * N_EXPERTS, n_used, w_gu, b_gu, w_dn, b_dn)
    return _combine_ple(y, pos, n8f, sbf, gdf, h_mid, gates, p, ple_w, ple_norm, gate_norm, gate_w)


def kernel(x, p, norm_mix, norm_ffn, hg_w_in, hg_w_out, hg_gnorm, hg_lb_param, fox_w_in, fox_f_bias, fox_qnorm, fox_knorm, fox_w_out, rg_w_in, rg_conv_w, rg_conv_b, rg_wa, rg_ba, rg_wx, rg_bx, rg_lambda, rg_w_out, router_w, router_b, moe_w_gu, moe_b_gu, moe_w_dn, moe_b_dn, ple_w, ple_norm, ple_gate_norm, ple_gate_w):
    batch, seq, d = x.shape
    depth = p.shape[0]
    n = batch * seq
    lb_all = jnp.cumsum(jax.nn.softmax(hg_lb_param.astype(F32), axis=0), axis=0)
    lb_all = lb_all - lb_all[0]
    h = x.reshape(n, d).astype(F32)
    ne = depth * N_EXPERTS
    w_gu = moe_w_gu.astype(F32).reshape(ne, d, 2 * d)
    b_gu = moe_b_gu.astype(F32).reshape(ne, 2 * d)
    w_dn = moe_w_dn.astype(F32).reshape(ne, d, d)
    b_dn = moe_b_dn.astype(F32).reshape(ne, d)
    for i in range(depth):
        j = i // 3
        kind = i % 3
        if kind == 0:
            act = _hgrn2_mixer(h, norm_mix[i], hg_w_in[j], None, hg_gnorm[j], lb_all[i], batch, seq)
            w_out = hg_w_out[j]
        elif kind == 1:
            act = _fox_mixer(h, norm_mix[i], fox_w_in[j], fox_f_bias[j], fox_qnorm[j], fox_knorm[j], batch, seq)
            w_out = fox_w_out[j]
        else:
            act = _rglru_mixer(h, norm_mix[i], rg_w_in[j], rg_conv_w[j], rg_conv_b[j], rg_wa[j], rg_ba[j],
                               rg_wx[j], rg_bx[j], rg_lambda[j], batch, seq)
            w_out = rg_w_out[j]
        h = _moe_ple(act, h, w_out, norm_ffn[i], router_w[i], router_b[i], i, w_gu, b_gu, w_dn, b_dn,
                     p[i].reshape(n, PLE_DIM), ple_w[i], ple_norm[i], ple_gate_norm[i], ple_gate_w[i])
    return h.reshape(batch, seq, d)
```

```python
import functools

import jax
import jax.numpy as jnp
import numpy as np
from jax import lax
from jax.experimental import pallas as pl
from jax.experimental.pallas import tpu as pltpu

F32 = jnp.float32
BF16 = jnp.bfloat16
I32 = jnp.int32

D_MODEL = 1024
EPS = 1e-6
PLE_DIM = 256

HG_HEADS = 8
HG_DK = 128
HG_CHUNK = 128
HG_LEVELS = 7

FOX_HEADS = 16
FOX_HD = 64
FOX_ZERO_EXP = -105.0

RG_BLOCKS = 4
RG_BW = 256
CONV_W = 4
RG_C = 8.0

N_EXPERTS = 32
TOP_K = 4
SWIGLU_LIMIT = 7.0
SWIGLU_ALPHA = 1.702
MOE_BLK = 512
EXPERT_CHUNK = 512
MOE_TILE = 512
STRIP_ALIGN = 8
STAGE_ROWS = -(-(TOP_K * MOE_TILE + N_EXPERTS * (STRIP_ALIGN - 1)) // 256) * 256
ONEHOT_CHUNK = 256
COMBINE_CHUNK = STAGE_ROWS // 3

VMEM_LIMIT = 56 * 1024 * 1024


def _cparams(sem):
    return pltpu.CompilerParams(dimension_semantics=sem, vmem_limit_bytes=VMEM_LIMIT)


def _bdot(a, b):
    return jnp.dot(a.astype(BF16), b.astype(BF16), preferred_element_type=F32)


def _bdot_nt(a, b):
    return lax.dot_general(a.astype(BF16), b.astype(BF16), (((1,), (1,)), ((), ())),
                           preferred_element_type=F32)


def _bdot_tn(a, b):
    return lax.dot_general(a.astype(BF16), b.astype(BF16), (((0,), (0,)), ((), ())),
                           preferred_element_type=F32)


def _rms(x, g):
    return x * lax.rsqrt(jnp.mean(x * x, axis=-1, keepdims=True) + EPS) * g


def _split2(x):
    hi = x.astype(BF16)
    lo = (x - hi.astype(F32)).astype(BF16)
    return hi, lo


def _split3(x):
    hi = x.astype(BF16)
    r = x - hi.astype(F32)
    mid = r.astype(BF16)
    lo = (r - mid.astype(F32)).astype(BF16)
    return hi, mid, lo


def _log_sigmoid(z):
    return jnp.minimum(z, 0.0) - jnp.log1p(jnp.exp(-jnp.abs(z)))


def _full(shape):
    return pl.BlockSpec(shape, lambda *_: (0,) * len(shape))


def _norm_proj_kernel(h_ref, g_ref, w_ref, o_ref, *, cn):
    xn = _rms(h_ref[...], g_ref[...]).astype(BF16)
    m = w_ref.shape[1]
    for c in range(m // cn):
        o_ref[:, c * cn:(c + 1) * cn] = jnp.dot(
            xn, w_ref[:, c * cn:(c + 1) * cn], preferred_element_type=F32).astype(o_ref.dtype)


def _norm_proj(h, g, w, tm=256, out_dtype=F32):
    n, d = h.shape
    m = w.shape[1]
    return pl.pallas_call(
        functools.partial(_norm_proj_kernel, cn=512),
        grid=(n // tm,),
        in_specs=[pl.BlockSpec((tm, d), lambda i: (i, 0)), _full((1, d)), _full((d, m))],
        out_specs=pl.BlockSpec((tm, m), lambda i: (i, 0)),
        out_shape=jax.ShapeDtypeStruct((n, m), out_dtype),
        compiler_params=_cparams(("parallel",)),
        name="norm_proj",
    )(h, g.reshape(1, d), w)


def _hgrn2_consts():
    c = HG_CHUNK
    t = np.arange(c)
    tril = (t[:, None] >= t[None, :]).astype(np.float32)
    sel = np.zeros((HG_LEVELS, c, c), np.float32)
    for l in range(HG_LEVELS):
        hs = 1 << l
        m = (t // (2 * hs)) * (2 * hs) + hs - 1
        sel[l, t, m] = 1.0
    return jnp.asarray(tril, BF16), jnp.asarray(sel.reshape(HG_LEVELS * c, c), BF16)


def _hgrn2_kernel(q_ref, z_ref, v_ref, g_ref, par_ref, tril_ref, sel_ref, o_ref, st_ref, *, nchunk):
    c = HG_CHUNK

    @pl.when(pl.program_id(2) == 0)
    def _():
        st_ref[...] = jnp.zeros_like(st_ref)

    log_lb = par_ref[0:1, :]
    log1m_lb = par_ref[1:2, :]
    one_m_lb = par_ref[2:3, :]
    gnorm = par_ref[3:4, :]
    row = lax.broadcasted_iota(I32, (c, c), 0)
    col = lax.broadcasted_iota(I32, (c, c), 1)

    cs = range(nchunk)
    rs = [slice(ci * c, (ci + 1) * c) for ci in cs]
    tril = tril_ref[...]
    q, k, vb, cum, refs, a = [], [], [], [], [], []
    for r in rs:
        qr = q_ref[r, :]
        z = z_ref[r, :]
        vb.append(v_ref[r, :].astype(BF16))
        q.append(qr * jax.nn.sigmoid(qr) * (HG_DK ** -0.5))
        b = log1m_lb + _log_sigmoid(z)
        lf = jnp.maximum(log_lb, b) + jnp.log1p(jnp.exp(-jnp.abs(log_lb - b)))
        k.append(one_m_lb * jax.nn.sigmoid(-z))
        hi, mid, lo = _split3(lf)
        cum.append(jnp.dot(tril, hi, preferred_element_type=F32)
                   + jnp.dot(tril, mid, preferred_element_type=F32)
                   + jnp.dot(tril, lo, preferred_element_type=F32))
    for i in cs:
        chi, clo = _split2(cum[i])
        refs.append(jnp.dot(sel_ref[...], chi, preferred_element_type=F32)
                    + jnp.dot(sel_ref[...], clo, preferred_element_type=F32))
        a.append(jnp.where(row == col, _bdot_nt(q[i], k[i]), 0.0))
    for l in range(HG_LEVELS):
        hs = 1 << l
        mask = (((row ^ col) >> l) == 1) & (row > col)
        for i in cs:
            ref = refs[i][l * c:(l + 1) * c, :]
            if hs >= 8:
                parts = []
                for blk in range(c // hs):
                    sl = slice(blk * hs, (blk + 1) * hs)
                    if blk % 2:
                        parts.append(q[i][sl] * jnp.exp(jnp.minimum(cum[i][sl] - ref[sl], 0.0)))
                    else:
                        parts.append(k[i][sl] * jnp.exp(jnp.minimum(ref[sl] - cum[i][sl], 0.0)))
                x = jnp.concatenate(parts, axis=0).astype(BF16)
            else:
                x = (jnp.where((row & hs) != 0, q[i], k[i]) * jnp.exp(-jnp.abs(cum[i] - ref))).astype(BF16)
            a[i] = jnp.where(mask, _bdot_nt(x, x), a[i])
    intra = [_bdot(a[i], vb[i]) for i in cs]
    last = [cum[i][c - 1:c, :] for i in cs]
    qe = [(q[i] * jnp.exp(cum[i])).astype(BF16) for i in cs]
    kd = [(k[i] * jnp.exp(last[i] - cum[i])).astype(BF16) for i in cs]
    st = st_ref[...]
    for i in cs:
        o = intra[i] + _bdot_nt(qe[i], st)
        st = st * jnp.exp(last[i]) + _bdot_tn(vb[i], kd[i])
        gt = g_ref[rs[i], :]
        y = _rms(o, gnorm) * (gt * jax.nn.sigmoid(gt))
        o_ref[rs[i], :] = y.astype(o_ref.dtype)
    st_ref[...] = st


def _hgrn2_recurrence(proj, par, batch, seq, tt=1024):
    n = batch * seq
    nt = seq // tt
    tril, sel = _hgrn2_consts()

    def part(p):
        return pl.BlockSpec((tt, HG_DK), lambda b, h, i, p=p: (b * nt + i, p * HG_HEADS + h))

    return pl.pallas_call(
        functools.partial(_hgrn2_kernel, nchunk=tt // HG_CHUNK),
        grid=(batch, HG_HEADS, nt),
        in_specs=[part(0), part(1), part(2), part(3),
                  pl.BlockSpec((8, HG_DK), lambda b, h, i: (0, h)),
                  _full(tril.shape), _full(sel.shape)],
        out_specs=pl.BlockSpec((tt, HG_DK), lambda b, h, i: (b * nt + i, h)),
        out_shape=jax.ShapeDtypeStruct((n, D_MODEL), BF16),
        scratch_shapes=[pltpu.VMEM((HG_DK, HG_DK), F32)],
        compiler_params=_cparams(("parallel", "parallel", "arbitrary")),
        name="hgrn2_recurrence",
    )(proj, proj, proj, proj, par, tril, sel)


def _hgrn2_mixer(h, g_mix, w_in, w_out_unused, g_norm, lb, batch, seq):
    del w_out_unused
    proj = _norm_proj(h, g_mix, w_in.astype(BF16))
    par = jnp.zeros((8, D_MODEL), F32)
    par = par.at[0].set(jnp.log(lb)).at[1].set(jnp.log1p(-lb)).at[2].set(1.0 - lb)
    par = par.at[3].set(jnp.tile(g_norm.astype(F32), HG_HEADS))
    return _hgrn2_recurrence(proj, par, batch, seq)


def _fox_proj_kernel(h_ref, g_ref, w_ref, wvt_ref, wf_ref, fb_ref, qg_ref, kg_ref, gs_ref, gst_ref, tril_ref,
                     place_ref, q_ref, gate_ref, ka_ref, vt_ref, cum_ref, carry_ref, *, tiles_per_seq):
    d = D_MODEL

    @pl.when(pl.program_id(0) % tiles_per_seq == 0)
    def _():
        carry_ref[...] = jnp.zeros_like(carry_ref)

    xn = _rms(h_ref[...], g_ref[...])
    xb = xn.astype(BF16)

    def headnorm(t, gain):
        shi, slo = _split2(t * t)
        ssq = (jnp.dot(shi, gs_ref[...], preferred_element_type=F32)
               + jnp.dot(slo, gs_ref[...], preferred_element_type=F32))
        inv = lax.rsqrt(ssq * (1.0 / FOX_HD) + EPS)
        ihi, ilo = _split2(inv)
        invf = (jnp.dot(ihi, gst_ref[...], preferred_element_type=F32)
                + jnp.dot(ilo, gst_ref[...], preferred_element_type=F32))
        return t * invf * gain

    q = jnp.dot(xb, w_ref[:, 0:d], preferred_element_type=F32)
    q_ref[...] = (headnorm(q, qg_ref[...]) * (FOX_HD ** -0.5)).astype(q_ref.dtype)
    k = jnp.dot(xb, w_ref[:, d:2 * d], preferred_element_type=F32)
    kn = headnorm(k, kg_ref[...]).astype(BF16)
    gate_ref[...] = jnp.dot(xb, w_ref[:, 3 * d:4 * d], preferred_element_type=F32).astype(gate_ref.dtype)
    vt_ref[...] = lax.dot_general(wvt_ref[...], xb, (((1,), (1,)), ((), ())),
                                  preferred_element_type=F32).astype(vt_ref.dtype)

    x_lo = (xn - xb.astype(F32)).astype(BF16)
    wf_hi, wf_lo = _split2(wf_ref[...])
    fl = _bdot(xb, wf_hi) + (_bdot(x_lo, wf_hi) + _bdot(xb, wf_lo)) + fb_ref[...]
    hi, mid, lo = _split3(_log_sigmoid(fl))
    tril = tril_ref[...]
    cum = (jnp.dot(tril, hi, preferred_element_type=F32)
           + jnp.dot(tril, mid, preferred_element_type=F32)
           + jnp.dot(tril, lo, preferred_element_type=F32)) + carry_ref[0:1, :]
    cum_ref[...] = cum
    tm = cum.shape[0]
    carry_ref[...] = jnp.broadcast_to(cum[tm - 1:tm, :], carry_ref.shape)
    nhi, nmid, nlo = _split3(-cum)
    feat = (jnp.dot(nhi, place_ref[0], preferred_element_type=F32)
            + jnp.dot(nmid, place_ref[1], preferred_element_type=F32)
            + jnp.dot(nlo, place_ref[2], preferred_element_type=F32)).astype(BF16)
    for hp in range(FOX_HEADS // 2):
        ka_ref[:, hp * 256:hp * 256 + 128] = kn[:, hp * 128:(hp + 1) * 128]
        ka_ref[:, hp * 256 + 128:(hp + 1) * 256] = feat[:, hp * 128:(hp + 1) * 128]


def _fox_proj(h, g_mix, w_in, f_bias, q_norm, k_norm, seq, tm=256):
    n, d = h.shape
    w_main = w_in[:, :4 * d].astype(BF16)
    w_vt = w_in[:, 2 * d:3 * d].T.astype(BF16)
    w_f = jnp.zeros((d, 128), F32).at[:, :FOX_HEADS].set(w_in[:, 4 * d:].astype(F32))
    fb = jnp.zeros((1, 128), F32).at[0, :FOX_HEADS].set(f_bias.astype(F32))
    head_of = np.arange(d) // FOX_HD
    gs_np = (head_of[:, None] == np.arange(128)[None, :]).astype(np.float32)
    gs = jnp.asarray(gs_np, BF16)
    gst = jnp.asarray(gs_np.T, BF16)
    tril = jnp.asarray(np.tril(np.ones((tm, tm), np.float32)), BF16)
    place_np = np.zeros((3, 128, d), np.float32)
    for hd in range(FOX_HEADS):
        for c in range(3):
            place_np[c, hd, (hd // 2) * 128 + 3 * (hd % 2) + c] = 1.0
    place = jnp.asarray(place_np, BF16)
    qg = jnp.tile(q_norm.astype(F32), FOX_HEADS).reshape(1, d)
    kg = jnp.tile(k_norm.astype(F32), FOX_HEADS).reshape(1, d)
    row = lambda i: (i, 0)
    return pl.pallas_call(
        functools.partial(_fox_proj_kernel, tiles_per_seq=seq // tm),
        grid=(n // tm,),
        in_specs=[pl.BlockSpec((tm, d), row), _full((1, d)), _full((d, 4 * d)), _full((d, d)),
                  _full((d, 128)), _full((1, 128)), _full((1, d)), _full((1, d)),
                  _full((d, 128)), _full((128, d)), _full((tm, tm)), _full((3, 128, d))],
        out_specs=[pl.BlockSpec((tm, d), row), pl.BlockSpec((tm, d), row), pl.BlockSpec((tm, 2 * d), row),
                   pl.BlockSpec((d, tm), lambda i: (0, i)), pl.BlockSpec((tm, 128), row)],
        out_shape=[jax.ShapeDtypeStruct((n, d), BF16), jax.ShapeDtypeStruct((n, d), BF16),
                   jax.ShapeDtypeStruct((n, 2 * d), BF16), jax.ShapeDtypeStruct((d, n), BF16),
                   jax.ShapeDtypeStruct((n, 128), F32)],
        scratch_shapes=[pltpu.VMEM((8, 128), F32)],
        compiler_params=_cparams(("arbitrary",)),
        name="fox_proj",
    )(h, g_mix.reshape(1, d), w_main, w_vt, w_f, fb, qg, kg, gs, gst, tril, place)


def _fox_attn_kernel(jlo_ref, q_ref, k_ref, vt_ref, g_ref, o_ref, qh_ref, m_ref, l_ref, acc_ref, *, tq, nq):
    qi = pl.program_id(2)
    lane = lax.broadcasted_iota(I32, (tq, 128), 1)
    qv = q_ref[...]
    for hh in range(2):
        own = (lane < FOX_HD) if hh == 0 else (lane >= FOX_HD)
        qh_ref[hh, :, 0:128] = jnp.where(own, qv, jnp.zeros_like(qv))
        pick = jnp.where(lane < 3 * hh, 0.0, jnp.where(lane < 3 * hh + 3, 1.0, 0.0))
        qh_ref[hh, :, 128:256] = pick.astype(BF16)
    m_ref[...] = jnp.full_like(m_ref, -jnp.inf)
    l_ref[...] = jnp.zeros_like(l_ref)
    acc_ref[...] = jnp.zeros_like(acc_ref)

    def sweep(kj, masked, nblk=1):
        rows = pl.ds(pl.multiple_of(kj * tq, tq), nblk * tq)
        kb = k_ref[rows, :]
        vt = vt_ref[:, rows]
        sts = [lax.dot_general(kb, qh_ref[hh], (((1,), (1,)), ((), ())), preferred_element_type=F32)
               for hh in range(2)]
        ps, alphas = [], []
        for hh in range(2):
            st = sts[hh]
            if masked:
                key = lax.broadcasted_iota(I32, st.shape, 0)
                qry = lax.broadcasted_iota(I32, st.shape, 1)
                st = jnp.where(key <= qry, st, -jnp.inf)
            m_old = m_ref[hh]
            m_new = jnp.maximum(m_old, jnp.max(st, axis=0, keepdims=True))
            alpha = jnp.exp(m_old - m_new)
            p = jnp.exp(st - m_new[0:1, :])
            l_ref[hh] = alpha * l_ref[hh] + jnp.sum(p, axis=0, keepdims=True)
            m_ref[hh] = m_new
            ps.append(p.astype(BF16))
            alphas.append(alpha[0:1, :])
        for hh in range(2):
            acc_ref[hh] = alphas[hh] * acc_ref[hh] + jnp.dot(
                vt[hh * FOX_HD:(hh + 1) * FOX_HD, :], ps[hh], preferred_element_type=F32)

    lo = jlo_ref[(pl.program_id(0) * pl.num_programs(1) + pl.program_id(1)) * nq + qi]
    span = qi - lo

    def body(i, carry):
        sweep(lo + 2 * i, False, 2)
        return carry

    lax.fori_loop(0, span // 2, body, 0)

    @pl.when(span % 2 == 1)
    def _():
        sweep(qi - 1, False)

    sweep(qi, True)
    ot = jnp.concatenate([acc_ref[0] / l_ref[0][0:1, :], acc_ref[1] / l_ref[1][0:1, :]], axis=0)
    gt = g_ref[...].astype(F32)
    o_ref[...] = (ot.T * jax.nn.sigmoid(gt)).astype(o_ref.dtype)


def _fox_first_block(cum_t, logit_bound, tq):
    bh, _, seq = cum_t.shape
    nq = seq // tq
    blk = cum_t.reshape(bh, 2, nq, tq)
    gap = blk[:, :, :, None, 0] - blk[:, :, None, :, tq - 1]
    dead = jnp.all(2.0 * logit_bound + gap < FOX_ZERO_EXP, axis=1)
    dead = dead & (jnp.arange(nq)[None, None, :] < jnp.arange(nq)[None, :, None])
    return jnp.sum(jnp.cumprod(dead.astype(I32), axis=-1), axis=-1).astype(I32).reshape(-1)


def _fox_attention(q, gate, ka, vt, cum, logit_bound, batch, seq, tq=512):
    n = batch * seq
    nq = seq // tq
    hp = FOX_HEADS // 2
    cum_t = cum[:, :FOX_HEADS].reshape(batch, seq, hp, 2).transpose(0, 2, 3, 1).reshape(batch * hp, 2, seq)
    jlo = _fox_first_block(cum_t, logit_bound, tq)
    grid_spec = pltpu.PrefetchScalarGridSpec(
        num_scalar_prefetch=1,
        grid=(batch, hp, nq),
        in_specs=[
            pl.BlockSpec((tq, 128), lambda b, h, i, jlo: (b * nq + i, h)),
            pl.BlockSpec((seq, 256), lambda b, h, i, jlo: (b, h)),
            pl.BlockSpec((128, seq), lambda b, h, i, jlo: (h, b)),
            pl.BlockSpec((tq, 128), lambda b, h, i, jlo: (b * nq + i, h)),
        ],
        out_specs=pl.BlockSpec((tq, 128), lambda b, h, i, jlo: (b * nq + i, h)),
        scratch_shapes=[pltpu.VMEM((2, tq, 256), BF16), pltpu.VMEM((2, 8, tq), F32),
                        pltpu.VMEM((2, 8, tq), F32), pltpu.VMEM((2, FOX_HD, tq), F32)],
    )
    return pl.pallas_call(
        functools.partial(_fox_attn_kernel, tq=tq, nq=nq),
        grid_spec=grid_spec,
        out_shape=jax.ShapeDtypeStruct((n, D_MODEL), BF16),
        compiler_params=_cparams(("parallel", "parallel", "arbitrary")),
        name="fox_attention",
    )(jlo, q, ka, vt, gate)


def _fox_mixer(h, g_mix, w_in, f_bias, q_norm, k_norm, batch, seq):
    q, gate, ka, vt, cum = _fox_proj(h, g_mix, w_in, f_bias, q_norm, k_norm, seq)
    logit_bound = 1.02 * FOX_HD ** 0.5 * jnp.max(jnp.abs(q_norm.astype(F32))) * jnp.max(jnp.abs(k_norm.astype(F32)))
    return _fox_attention(q, gate, ka, vt, cum, logit_bound, batch, seq)


def _rglru_kernel(gate_ref, u_ref, par_ref, wa_ref, wx_ref, o_ref, prev_ref, hc_ref, *, tt):
    @pl.when(pl.program_id(1) == 0)
    def _():
        prev_ref[...] = jnp.zeros_like(prev_ref)
        hc_ref[...] = jnp.zeros_like(hc_ref)

    u = u_ref[...]
    ext = jnp.concatenate([prev_ref[...], u], axis=0)
    conv = par_ref[4:5, :] + u * par_ref[3:4, :]
    for shift in range(1, CONV_W):
        conv = conv + pltpu.roll(ext, shift, 0)[8:, :] * par_ref[3 - shift:4 - shift, :]
    prev_ref[...] = u[tt - 8:, :]

    cb = conv.astype(BF16)
    ra, ia = [], []
    for nb in range(RG_BLOCKS):
        blk = cb[:, nb * RG_BW:(nb + 1) * RG_BW]
        ra.append(jnp.dot(blk, wa_ref[nb], preferred_element_type=F32))
        ia.append(jnp.dot(blk, wx_ref[nb], preferred_element_type=F32))
    r = jax.nn.sigmoid(jnp.concatenate(ra, axis=1) + par_ref[5:6, :])
    ig = jax.nn.sigmoid(jnp.concatenate(ia, axis=1) + par_ref[6:7, :])
    lam = par_ref[7:8, :]
    softplus = jnp.maximum(-lam, 0.0) + jnp.log1p(jnp.exp(-jnp.abs(lam)))
    log_a = -RG_C * r * softplus
    a = jnp.exp(log_a)
    b = jnp.sqrt(1.0 - a * a) * (ig * conv)

    a = a.reshape(tt // 8, 8, a.shape[-1])
    b = b.reshape(a.shape)
    within = lax.broadcasted_iota(I32, a.shape, 1)
    for dist in (1, 2, 4):
        ok = within >= dist
        a_sh = pltpu.roll(a, dist, 1)
        b_sh = pltpu.roll(b, dist, 1)
        b = jnp.where(ok, a * b_sh + b, b)
        a = jnp.where(ok, a * a_sh, a)
    a = a.reshape(tt, a.shape[-1])
    b = b.reshape(a.shape)
    gt = gate_ref[...]
    gelu = 0.5 * gt * (1.0 + jnp.tanh(0.7978845608028654 * (gt + 0.044715 * gt * gt * gt)))
    h = hc_ref[0:1, :]
    for g in range(tt // 8):
        sl = slice(g * 8, (g + 1) * 8)
        hg = b[sl] + a[sl] * h
        o_ref[sl, :] = (hg * gelu[sl]).astype(o_ref.dtype)
        h = hg[7:8, :]
    hc_ref[...] = jnp.broadcast_to(h, hc_ref.shape)


def _rglru_mixer(h, g_mix, w_in, conv_w, conv_b, w_a, b_a, w_x, b_x, lam, batch, seq, tt=256):
    n = batch * seq
    nt = seq // tt
    w = D_MODEL
    proj = _norm_proj(h, g_mix, w_in.astype(BF16))
    par = jnp.concatenate([conv_w.astype(F32), conv_b.reshape(1, w), b_a.reshape(1, w),
                           b_x.reshape(1, w), lam.reshape(1, w)], axis=0).astype(F32)
    return pl.pallas_call(
        functools.partial(_rglru_kernel, tt=tt),
        grid=(batch, nt),
        in_specs=[pl.BlockSpec((tt, w), lambda b, i: (b * nt + i, 0)),
                  pl.BlockSpec((tt, w), lambda b, i: (b * nt + i, 1)),
                  _full((8, w)), _full((RG_BLOCKS, RG_BW, RG_BW)), _full((RG_BLOCKS, RG_BW, RG_BW))],
        out_specs=pl.BlockSpec((tt, w), lambda b, i: (b * nt + i, 0)),
        out_shape=jax.ShapeDtypeStruct((n, w), BF16),
        scratch_shapes=[pltpu.VMEM((8, w), F32), pltpu.VMEM((8, w), F32)],
        compiler_params=_cparams(("parallel", "arbitrary")),
        name="rglru",
    )(proj, proj, par, w_a.astype(BF16), w_x.astype(BF16))


def _outproj_router_kernel(a_ref, h_ref, wo_ref, g_ref, wr_ref, br_ref, tri_ref, lst_ref,
                           hm_ref, xn_ref, gate_ref, pos_ref, cnt_ref):
    hm = h_ref[...] + jnp.dot(a_ref[...], wo_ref[...], preferred_element_type=F32)
    hm_ref[...] = hm
    xn = _rms(hm, g_ref[...])
    x_hi, x_lo = _split2(xn)
    xn_ref[...] = x_hi
    w_hi, w_lo = _split2(wr_ref[...])
    logit = _bdot_nt(w_hi, x_hi) + (_bdot_nt(w_hi, x_lo) + _bdot_nt(w_lo, x_hi)) + br_ref[:, 0:1]
    ne, tm = logit.shape
    eidx = lax.broadcasted_iota(I32, (ne, tm), 0)
    work = logit
    vals, hots = [], []
    for _ in range(TOP_K):
        mx = jnp.max(work, axis=0, keepdims=True)
        pick = jnp.min(jnp.where(work == mx, eidx, ne), axis=0, keepdims=True)
        hot = eidx == pick
        work = jnp.where(hot, -jnp.inf, work)
        vals.append(mx)
        hots.append(hot)
    ex = [jnp.exp(v - vals[0]) for v in vals]
    den = ex[0] + ex[1] + ex[2] + ex[3]
    gate_ref[...] = jnp.concatenate([e / den for e in ex], axis=0)
    tok_hot = (hots[0] | hots[1] | hots[2] | hots[3]).astype(BF16)
    tri = tri_ref[...]
    count = jnp.zeros((ne, 1), F32)
    pieces = []
    for j in range(tm // 128):
        hb = tok_hot[:, j * 128:(j + 1) * 128]
        pieces.append(jnp.dot(hb, tri, preferred_element_type=F32) + count)
        count = count + jnp.sum(hb.astype(F32), axis=1, keepdims=True)
    before = jnp.concatenate(pieces, axis=1)
    units = jnp.floor((count + (STRIP_ALIGN - 1)) * (1.0 / STRIP_ALIGN))
    units_b = jnp.broadcast_to(units, (ne, 128)).astype(BF16)
    start = STRIP_ALIGN * jnp.dot(lst_ref[...], units_b, preferred_element_type=F32)[:, 0:1]
    where = before + start
    pos_ref[...] = jnp.concatenate(
        [jnp.sum(jnp.where(h, where, 0.0), axis=0, keepdims=True) for h in hots], axis=0).astype(I32)
    cnt_ref[...] = jnp.broadcast_to(STRIP_ALIGN * units, cnt_ref.shape)


def _outproj_router(act, h, w_out, g_ffn, w_r, b_r):
    n, d = h.shape
    tm = MOE_TILE
    tri = jnp.asarray(np.triu(np.ones((128, 128), np.float32), 1), BF16)
    lst =jnp.asarray(np.tril(np.ones((N_EXPERTS, N_EXPERTS), np.float32), -1), BF16)
    row = lambda i: (i, 0)
    colb = lambda i: (0, i)
    return pl.pallas_call(
        _outproj_router_kernel,
        grid=(n // tm,),
        in_specs=[pl.BlockSpec((tm, d), row), pl.BlockSpec((tm, d), row), _full((d, d)), _full((1, d)),
                  _full((N_EXPERTS, d)), _full((N_EXPERTS, 128)), _full((128, 128)),
                  _full((N_EXPERTS, N_EXPERTS))],
        out_specs=[pl.BlockSpec((tm, d), row), pl.BlockSpec((tm, d), row),
                   pl.BlockSpec((TOP_K, tm), colb), pl.BlockSpec((TOP_K, tm), colb),
                   pl.BlockSpec((N_EXPERTS, 128), row)],
        out_shape=[jax.ShapeDtypeStruct((n, d), F32), jax.ShapeDtypeStruct((n, d), BF16),
                   jax.ShapeDtypeStruct((TOP_K, n), F32), jax.ShapeDtypeStruct((TOP_K, n), I32),
                   jax.ShapeDtypeStruct((n // tm * N_EXPERTS, 128), F32)],
        compiler_params=_cparams(("parallel",)),
        name="outproj_router",
    )(act, h, w_out.astype(BF16), g_ffn.reshape(1, d), w_r.T.astype(F32),
      jnp.broadcast_to(b_r.astype(F32)[:, None], (N_EXPERTS, 128)), tri, lst)


def _for_strip_chunks(nrows, max_rows, fn):
    size = max_rows
    while size >= STRIP_ALIGN:
        @pl.when((nrows & size) != 0)
        def _(size=size):
            fn(pl.multiple_of(nrows & ~(2 * size - 1), STRIP_ALIGN), size)
        size //= 2


def _for_tile_strips(n8_ref, sbase_ref, gdst_ref, tile, fn):
    def per_expert(e, carry):
        i = tile * N_EXPERTS + e
        so = sbase_ref[i]
        gd = gdst_ref[i]
        _for_strip_chunks(n8_ref[i], MOE_TILE,
                          lambda o, size: fn(pl.multiple_of(so + o, STRIP_ALIGN), pl.multiple_of(gd + o, STRIP_ALIGN), size))
        return carry

    lax.fori_loop(0, N_EXPERTS, per_expert, 0)


def _wait_tile_strips(n8_ref, sbase_ref, tile, wait_rows):
    last = tile * N_EXPERTS + (N_EXPERTS - 1)
    total = sbase_ref[last] + n8_ref[last]
    size = STAGE_ROWS.bit_length() - 1
    size = 1 << size
    while size >= STRIP_ALIGN:
        @pl.when((total & size) != 0)
        def _(size=size):
            wait_rows(size)
        size //= 2


def _dispatch_kernel(n8_ref, sbase_ref, gdst_ref, tot_ref, pst_ref, x_ref, pos_ref, xs_hbm,
                     stage, zblk, rsem, zsem):
    tile = pl.program_id(0)
    slot = tile % 2
    xb = x_ref[...]
    for c in range(STAGE_ROWS // ONEHOT_CHUNK):
        rows = lax.broadcasted_iota(I32, (ONEHOT_CHUNK, MOE_TILE), 0) + c * ONEHOT_CHUNK
        onehot = jnp.zeros((ONEHOT_CHUNK, MOE_TILE), F32)
        for k in range(TOP_K):
            onehot = jnp.where(rows == pos_ref[k:k + 1, :], 1.0, onehot)
        stage[slot, c * ONEHOT_CHUNK:(c + 1) * ONEHOT_CHUNK, :] = jnp.dot(
            onehot.astype(BF16), xb, preferred_element_type=F32)

    def copy(sl, so, gd, size):
        return pltpu.make_async_copy(stage.at[sl, pl.ds(so, size), :], xs_hbm.at[pl.ds(gd, size), :], rsem.at[sl])

    _for_tile_strips(n8_ref, sbase_ref, gdst_ref, tile, lambda so, gd, size: copy(slot, so, gd, size).start())

    @pl.when(tile > 0)
    def _():
        _wait_tile_strips(n8_ref, sbase_ref, tile - 1, lambda size: copy(1 - slot, 0, 0, size).wait())

    @pl.when(tile == 0)
    def _():
        zblk[...] = jnp.zeros_like(zblk)

        def zero(first, size):
            cp = pltpu.make_async_copy(zblk.at[pl.ds(0, size), :], xs_hbm.at[pl.ds(first, size), :], zsem)
            cp.start()
            cp.wait()

        def per_expert(e, carry):
            tot = tot_ref[e]
            first = pst_ref[e] + tot
            npad = ((tot + (MOE_BLK - 1)) // MOE_BLK) * MOE_BLK - tot
            _for_strip_chunks(npad, MOE_BLK // 2,
                              lambda o, size: zero(pl.multiple_of(first + o, STRIP_ALIGN), size))
            return carry

        lax.fori_loop(0, N_EXPERTS, per_expert, 0)

        last = N_EXPERTS - 1
        used = (pst_ref[last] + tot_ref[last] + (MOE_BLK - 1)) // MOE_BLK

        def ztail(b, c):
            zero(pl.multiple_of(b * MOE_BLK, MOE_BLK), MOE_BLK)
            return c

        lax.fori_loop(used, xs_hbm.shape[0] // MOE_BLK, ztail, 0)

    @pl.when(tile == pl.num_programs(0) - 1)
    def _():
        _wait_tile_strips(n8_ref, sbase_ref, tile, lambda size: copy(slot, 0, 0, size).wait())


def _dispatch(xn, pos, n8, sbase, gdst, total, pstart, cap):
    n, d = xn.shape
    tm = MOE_TILE
    grid_spec = pltpu.PrefetchScalarGridSpec(
        num_scalar_prefetch=5,
        grid=(n // tm,),
        in_specs=[pl.BlockSpec((tm, d), lambda i, *_: (i, 0)), pl.BlockSpec((TOP_K, tm), lambda i, *_: (0, i))],
        out_specs=pl.BlockSpec(memory_space=pl.ANY),
        scratch_shapes=[pltpu.VMEM((2, STAGE_ROWS, d), F32), pltpu.VMEM((MOE_BLK, d), F32),
                        pltpu.SemaphoreType.DMA((2,)), pltpu.SemaphoreType.DMA],
    )
    return pl.pallas_call(
        _dispatch_kernel,
        grid_spec=grid_spec,
        out_shape=jax.ShapeDtypeStruct((cap, d), F32),
        compiler_params=_cparams(("arbitrary",)),
        name="moe_dispatch",
    )(n8, sbase, gdst, total, pstart, xn, pos)


def _expert_kernel(be_ref, nu_ref, x_ref, wgu_ref, bgu_ref, wdn_ref, bdn_ref, y_ref, wgu_bf, wdn_bf):
    d = D_MODEL
    b = pl.program_id(0)

    @pl.when((b == 0) | (be_ref[b] != be_ref[jnp.maximum(b - 1, 0)]))
    def _():
        wgu_bf[...] = wgu_ref[...].astype(BF16)
        wdn_bf[...] = wdn_ref[...].astype(BF16)

    @pl.when(b < nu_ref[0])
    def _():
        xb = x_ref[...].astype(BF16)
        acc = None
        for c in range(d // EXPERT_CHUNK):
            lo, hi = c * EXPERT_CHUNK, (c + 1) * EXPERT_CHUNK
            glu = jnp.dot(xb, wgu_bf[:, lo:hi], preferred_element_type=F32) + bgu_ref[:, lo:hi]
            lin = jnp.dot(xb, wgu_bf[:, d + lo:d + hi], preferred_element_type=F32) + bgu_ref[:, d + lo:d + hi]
            glu = jnp.minimum(glu, SWIGLU_LIMIT)
            lin = jnp.clip(lin, -SWIGLU_LIMIT, SWIGLU_LIMIT)
            act = glu * jax.nn.sigmoid(SWIGLU_ALPHA * glu) * (lin + 1.0)
            part = jnp.dot(act.astype(BF16), wdn_bf[lo:hi, :], preferred_element_type=F32)
            acc = part if acc is None else acc + part
        y_ref[...] = acc + bdn_ref[...]

    @pl.when(b >= nu_ref[0])
    def _():
        y_ref[...] = jnp.zeros_like(y_ref)


def _experts(xs, block_expert, n_used, w_gu, b_gu, w_dn, b_dn):
    cap, d = xs.shape
    nb = cap // MOE_BLK
    ne = w_gu.shape[0]

    def xmap(b, be, nu):
        return (jnp.minimum(b, nu[0] - 1), 0)

    def wmap(b, be, nu):
        return (be[b], 0, 0)

    grid_spec = pltpu.PrefetchScalarGridSpec(
        num_scalar_prefetch=2,
        grid=(nb,),
        in_specs=[pl.BlockSpec((MOE_BLK, d), xmap),
                  pl.BlockSpec((None, d, 2 * d), wmap), pl.BlockSpec((None, 1, 2 * d), wmap),
                  pl.BlockSpec((None, d, d), wmap), pl.BlockSpec((None, 1, d), wmap)],
        out_specs=pl.BlockSpec((MOE_BLK, d), lambda b, be, nu: (b, 0)),
        scratch_shapes=[pltpu.VMEM((d, 2 * d), BF16), pltpu.VMEM((d, d), BF16)],
    )
    return pl.pallas_call(
        _expert_kernel,
        grid_spec=grid_spec,
        out_shape=jax.ShapeDtypeStruct((cap, d), F32),
        compiler_params=_cparams(("arbitrary",)),
        name="moe_experts",
    )(block_expert, n_used, xs, w_gu, b_gu.reshape(ne, 1, 2 * d), w_dn, b_dn.reshape(ne, 1, d))


def _combine_ple_kernel(n8_ref, sbase_ref, gdst_ref, y_hbm, hm_ref, gate_ref, pos_ref, p_ref, pw_ref, pn_ref,
                        gn_ref, gw_ref, o_ref, stage, rsem):
    tile = pl.program_id(0)
    slot = tile % 2

    def copy(sl, so, gd, size):
        return pltpu.make_async_copy(y_hbm.at[pl.ds(gd, size), :], stage.at[sl, pl.ds(so, size), :], rsem.at[sl])

    def fetch(t, sl):
        _for_tile_strips(n8_ref, sbase_ref, gdst_ref, t, lambda so, gd, size: copy(sl, so, gd, size).start())

    @pl.when(tile == 0)
    def _():
        stage[...] = jnp.zeros_like(stage)
        fetch(0, 0)

    @pl.when(tile + 1 < pl.num_programs(0))
    def _():
        fetch(tile + 1, 1 - slot)

    ple = _rms(jnp.dot(p_ref[...].astype(BF16), pw_ref[...], preferred_element_type=F32), pn_ref[...])
    _wait_tile_strips(n8_ref, sbase_ref, tile, lambda size: copy(slot, 0, 0, size).wait())
    h2 = hm_ref[...]
    for c in range(STAGE_ROWS // COMBINE_CHUNK):
        lanes = lax.broadcasted_iota(I32, (MOE_TILE, COMBINE_CHUNK), 1) + c * COMBINE_CHUNK
        weights = jnp.zeros((MOE_TILE, COMBINE_CHUNK), F32)
        for k in range(TOP_K):
            weights = jnp.where(lanes == pos_ref[:, k:k + 1], gate_ref[:, k:k + 1], weights)
        staged = stage[slot, c * COMBINE_CHUNK:(c + 1) * COMBINE_CHUNK, :].astype(BF16)
        h2 = h2 + jnp.dot(weights.astype(BF16), staged, preferred_element_type=F32)
    gate = jax.nn.sigmoid(jnp.dot(_rms(h2, gn_ref[...]).astype(BF16), gw_ref[...], preferred_element_type=F32))
    o_ref[...] = h2 + ple * gate


def _combine_ple(y, pos, n8, sbase, gdst, h_mid, gates, p, ple_w, ple_norm, gate_norm, gate_w):
    n, d = h_mid.shape
    tm = MOE_TILE
    row = lambda i, *_: (i, 0)
    full = lambda shape: pl.BlockSpec(shape, lambda i, *_: (0,) * len(shape))
    grid_spec = pltpu.PrefetchScalarGridSpec(
        num_scalar_prefetch=3,
        grid=(n // tm,),
        in_specs=[pl.BlockSpec(memory_space=pl.ANY), pl.BlockSpec((tm, d), row), pl.BlockSpec((tm, TOP_K), row),
                  pl.BlockSpec((tm, TOP_K), row), pl.BlockSpec((tm, PLE_DIM), row), full((PLE_DIM, d)),
                  full((1, d)), full((1, d)), full((d, d))],
        out_specs=pl.BlockSpec((tm, d), row),
        scratch_shapes=[pltpu.VMEM((2, STAGE_ROWS, d), F32), pltpu.SemaphoreType.DMA((2,))],
    )
    return pl.pallas_call(
        _combine_ple_kernel,
        grid_spec=grid_spec,
        out_shape=jax.ShapeDtypeStruct((n, d), F32),
        compiler_params=_cparams(("arbitrary",)),
        name="moe_combine_ple",
    )(n8, sbase, gdst, y, h_mid, gates.T, pos.T, p, ple_w.astype(BF16), ple_norm.reshape(1, d),
      gate_norm.reshape(1, d), gate_w.astype(BF16))


def _moe_ple(act, h, w_out, g_ffn, w_r, b_r, layer, w_gu, b_gu, w_dn, b_dn, p, ple_w, ple_norm, gate_norm, gate_w):
    n, d = h.shape
    ntiles = n // MOE_TILE
    h_mid, xn, gates, pos, cnt = _outproj_router(act, h, w_out, g_ffn, w_r, b_r)
    n8 = cnt[:, 0].astype(I32).reshape(ntiles, N_EXPERTS)
    sbase = jnp.cumsum(n8, axis=1) - n8
    total = jnp.sum(n8, axis=0)
    padded = (total + MOE_BLK - 1) // MOE_BLK * MOE_BLK
    pend = jnp.cumsum(padded)
    pstart = pend - padded
    gdst = pstart[None, :] + jnp.cumsum(n8, axis=0) - n8
    nb = -(-(n * TOP_K + ntiles * N_EXPERTS * (STRIP_ALIGN - 1)) // MOE_BLK) + N_EXPERTS
    cap = nb * MOE_BLK
    block_start = jnp.arange(nb, dtype=I32) * MOE_BLK
    block_expert = jnp.minimum(jnp.sum(pend[None, :] <= block_start[:, None], axis=1), N_EXPERTS - 1).astype(I32)
    n_used = (pend[-1:] // MOE_BLK).astype(I32)
    n8f, sbf, gdf = n8.reshape(-1), sbase.reshape(-1).astype(I32), gdst.reshape(-1).astype(I32)
    xs = _dispatch(xn, pos, n8f, sbf, gdf, total.astype(I32), pstart.astype(I32), cap)
    y = _experts(xs, block_expert + layer * N_EXPERTS, n_used, w_gu, b_gu, w_dn, b_dn)
    return _combine_ple(y, pos, n8f, sbf, gdf, h_mid, gates, p, ple_w, ple_norm, gate_norm, gate_w)


def kernel(x, p, norm_mix, norm_ffn, hg_w_in, hg_w_out, hg_gnorm, hg_lb_param, fox_w_in, fox_f_bias, fox_qnorm, fox_knorm, fox_w_out, rg_w_in, rg_conv_w, rg_conv_b, rg_wa, rg_ba, rg_wx, rg_bx, rg_lambda, rg_w_out, router_w, router_b, moe_w_gu, moe_b_gu, moe_w_dn, moe_b_dn, ple_w, ple_norm, ple_gate_norm, ple_gate_w):
    batch, seq, d = x.shape
    depth = p.shape[0]
    n = batch * seq
    lb_all = jnp.cumsum(jax.nn.softmax(hg_lb_param.astype(F32), axis=0), axis=0)
    lb_all = lb_all - lb_all[0]
    h = x.reshape(n, d).astype(F32)
    ne = depth * N_EXPERTS
    w_gu = moe_w_gu.astype(F32).reshape(ne, d, 2 * d)
    b_gu = moe_b_gu.astype(F32).reshape(ne, 2 * d)
    w_dn = moe_w_dn.astype(F32).reshape(ne, d, d)
    b_dn = moe_b_dn.astype(F32).reshape(ne, d)
    for i in range(depth):
        j = i // 3
        kind = i % 3
        if kind == 0:
            act = _hgrn2_mixer(h, norm_mix[i], hg_w_in[j], None, hg_gnorm[j], lb_all[i], batch, seq)
            w_out = hg_w_out[j]
        elif kind == 1:
            act = _fox_mixer(h, norm_mix[i], fox_w_in[j], fox_f_bias[j], fox_qnorm[j], fox_knorm[j], batch, seq)
            w_out = fox_w_out[j]
        else:
            act = _rglru_mixer(h, norm_mix[i], rg_w_in[j], rg_conv_w[j], rg_conv_b[j], rg_wa[j], rg_ba[j],
                               rg_wx[j], rg_bx[j], rg_lambda[j], batch, seq)
            w_out = rg_w_out[j]
        h = _moe_ple(act, h, w_out, norm_ffn[i], router_w[i], router_b[i], i, w_gu, b_gu, w_dn, b_dn,
                     p[i].reshape(n, PLE_DIM), ple_w[i], ple_norm[i], ple_gate_norm[i], ple_gate_w[i])
    return h.reshape(batch, seq, d)
```

```python
import functools

import jax
import jax.numpy as jnp
import numpy as np
from jax import lax
from jax.experimental import pallas as pl
from jax.experimental.pallas import tpu as pltpu

F32 = jnp.float32
BF16 = jnp.bfloat16
I32 = jnp.int32

D_MODEL = 1024
EPS = 1e-6
PLE_DIM = 256

HG_HEADS = 8
HG_DK = 128
HG_CHUNK = 128
HG_LEVELS = 7

FOX_HEADS = 16
FOX_HD = 64
FOX_ZERO_EXP = -105.0

RG_BLOCKS = 4
RG_BW = 256
CONV_W = 4
RG_C = 8.0

N_EXPERTS = 32
TOP_K = 4
SWIGLU_LIMIT = 7.0
SWIGLU_ALPHA = 1.702
MOE_BLK = 512
EXPERT_CHUNK = 512
MOE_TILE = 512
STRIP_ALIGN = 8
STAGE_ROWS = -(-(TOP_K * MOE_TILE + N_EXPERTS * (STRIP_ALIGN - 1)) // 256) * 256
SHORT_STRIP = 64
ONEHOT_CHUNK = 256
COMBINE_CHUNK = STAGE_ROWS // 3

VMEM_LIMIT = 56 * 1024 * 1024


def _cparams(sem):
    return pltpu.CompilerParams(dimension_semantics=sem, vmem_limit_bytes=VMEM_LIMIT)


def _bdot(a, b):
    return jnp.dot(a.astype(BF16), b.astype(BF16), preferred_element_type=F32)


def _bdot_nt(a, b):
    return lax.dot_general(a.astype(BF16), b.astype(BF16), (((1,), (1,)), ((), ())),
                           preferred_element_type=F32)


def _bdot_tn(a, b):
    return lax.dot_general(a.astype(BF16), b.astype(BF16), (((0,), (0,)), ((), ())),
                           preferred_element_type=F32)


def _rms(x, g):
    return x * lax.rsqrt(jnp.mean(x * x, axis=-1, keepdims=True) + EPS) * g


def _split2(x):
    hi = x.astype(BF16)
    lo = (x - hi.astype(F32)).astype(BF16)
    return hi, lo


def _split3(x):
    hi = x.astype(BF16)
    r = x - hi.astype(F32)
    mid = r.astype(BF16)
    lo = (r - mid.astype(F32)).astype(BF16)
    return hi, mid, lo


def _log_sigmoid(z):
    return jnp.minimum(z, 0.0) - jnp.log1p(jnp.exp(-jnp.abs(z)))


def _full(shape):
    return pl.BlockSpec(shape, lambda *_: (0,) * len(shape))


def _norm_proj_kernel(h_ref, g_ref, w_ref, o_ref, *, cn):
    xn = _rms(h_ref[...], g_ref[...]).astype(BF16)
    m = w_ref.shape[1]
    for c in range(m // cn):
        o_ref[:, c * cn:(c + 1) * cn] = jnp.dot(
            xn, w_ref[:, c * cn:(c + 1) * cn], preferred_element_type=F32).astype(o_ref.dtype)


def _norm_proj(h, g, w, tm=512, out_dtype=F32):
    n, d = h.shape
    m = w.shape[1]
    return pl.pallas_call(
        functools.partial(_norm_proj_kernel, cn=512),
        grid=(n // tm,),
        in_specs=[pl.BlockSpec((tm, d), lambda i: (i, 0)), _full((1, d)), _full((d, m))],
        out_specs=pl.BlockSpec((tm, m), lambda i: (i, 0)),
        out_shape=jax.ShapeDtypeStruct((n, m), out_dtype),
        compiler_params=_cparams(("parallel",)),
        name="norm_proj",
    )(h, g.reshape(1, d), w)


def _hgrn2_consts():
    c = HG_CHUNK
    t = np.arange(c)
    tril = (t[:, None] >= t[None, :]).astype(np.float32)
    sel = np.zeros((HG_LEVELS, c, c), np.float32)
    for l in range(HG_LEVELS):
        hs = 1 << l
        m = (t // (2 * hs)) * (2 * hs) + hs - 1
        sel[l, t, m] = 1.0
    return jnp.asarray(tril, BF16), jnp.asarray(sel.reshape(HG_LEVELS * c, c), BF16)


def _hgrn2_kernel(q_ref, z_ref, v_ref, g_ref, par_ref, tril_ref, sel_ref, o_ref, st_ref, *, nchunk):
    c = HG_CHUNK

    @pl.when(pl.program_id(2) == 0)
    def _():
        st_ref[...] = jnp.zeros_like(st_ref)

    log_lb = par_ref[0:1, :]
    log1m_lb = par_ref[1:2, :]
    one_m_lb = par_ref[2:3, :]
    gnorm = par_ref[3:4, :]
    row = lax.broadcasted_iota(I32, (c, c), 0)
    col = lax.broadcasted_iota(I32, (c, c), 1)

    cs = range(nchunk)
    rs = [slice(ci * c, (ci + 1) * c) for ci in cs]
    tril = tril_ref[...]
    q, k, vb, cum, refs, a = [], [], [], [], [], []
    for r in rs:
        qr = q_ref[r, :]
        z = z_ref[r, :]
        vb.append(v_ref[r, :].astype(BF16))
        q.append(qr * jax.nn.sigmoid(qr) * (HG_DK ** -0.5))
        b = log1m_lb + _log_sigmoid(z)
        lf = jnp.maximum(log_lb, b) + jnp.log1p(jnp.exp(-jnp.abs(log_lb - b)))
        k.append(one_m_lb * jax.nn.sigmoid(-z))
        hi, mid, lo = _split3(lf)
        cum.append(jnp.dot(tril, hi, preferred_element_type=F32)
                   + jnp.dot(tril, mid, preferred_element_type=F32)
                   + jnp.dot(tril, lo, preferred_element_type=F32))
    for i in cs:
        chi, clo = _split2(cum[i])
        refs.append(jnp.dot(sel_ref[...], chi, preferred_element_type=F32)
                    + jnp.dot(sel_ref[...], clo, preferred_element_type=F32))
        a.append(jnp.where(row == col, _bdot_nt(q[i], k[i]), 0.0))
    for l in range(HG_LEVELS):
        hs = 1 << l
        mask = (((row ^ col) >> l) == 1) & (row > col)
        for i in cs:
            ref = refs[i][l * c:(l + 1) * c, :]
            if hs >= 8:
                parts = []
                for blk in range(c // hs):
                    sl = slice(blk * hs, (blk + 1) * hs)
                    if blk % 2:
                        parts.append(q[i][sl] * jnp.exp(jnp.minimum(cum[i][sl] - ref[sl], 0.0)))
                    else:
                        parts.append(k[i][sl] * jnp.exp(jnp.minimum(ref[sl] - cum[i][sl], 0.0)))
                x = jnp.concatenate(parts, axis=0).astype(BF16)
            else:
                x = (jnp.where((row & hs) != 0, q[i], k[i]) * jnp.exp(-jnp.abs(cum[i] - ref))).astype(BF16)
            a[i] = jnp.where(mask, _bdot_nt(x, x), a[i])
    intra = [_bdot(a[i], vb[i]) for i in cs]
    last = [cum[i][c - 1:c, :] for i in cs]
    qe = [(q[i] * jnp.exp(cum[i])).astype(BF16) for i in cs]
    kd = [(k[i] * jnp.exp(last[i] - cum[i])).astype(BF16) for i in cs]
    st = st_ref[...]
    for i in cs:
        o = intra[i] + _bdot_nt(qe[i], st)
        st = st * jnp.exp(last[i]) + _bdot_tn(vb[i], kd[i])
        gt = g_ref[rs[i], :]
        y = _rms(o, gnorm) * (gt * jax.nn.sigmoid(gt))
        o_ref[rs[i], :] = y.astype(o_ref.dtype)
    st_ref[...] = st


def _hgrn2_recurrence(proj, par, batch, seq, tt=1024):
    n = batch * seq
    nt = seq // tt
    tril, sel = _hgrn2_consts()

    def part(p):
        return pl.BlockSpec((tt, HG_DK), lambda b, h, i, p=p: (b * nt + i, p * HG_HEADS + h))

    return pl.pallas_call(
        functools.partial(_hgrn2_kernel, nchunk=tt // HG_CHUNK),
        grid=(batch, HG_HEADS, nt),
        in_specs=[part(0), part(1), part(2), part(3),
                  pl.BlockSpec((8, HG_DK), lambda b, h, i: (0, h)),
                  _full(tril.shape), _full(sel.shape)],
        out_specs=pl.BlockSpec((tt, HG_DK), lambda b, h, i: (b * nt + i, h)),
        out_shape=jax.ShapeDtypeStruct((n, D_MODEL), BF16),
        scratch_shapes=[pltpu.VMEM((HG_DK, HG_DK), F32)],
        compiler_params=_cparams(("parallel", "parallel", "arbitrary")),
        name="hgrn2_recurrence",
    )(proj, proj, proj, proj, par, tril, sel)


def _hgrn2_mixer(h, g_mix, w_in, w_out_unused, g_norm, lb, batch, seq):
    del w_out_unused
    proj = _norm_proj(h, g_mix, w_in.astype(BF16))
    par = jnp.zeros((8, D_MODEL), F32)
    par = par.at[0].set(jnp.log(lb)).at[1].set(jnp.log1p(-lb)).at[2].set(1.0 - lb)
    par = par.at[3].set(jnp.tile(g_norm.astype(F32), HG_HEADS))
    return _hgrn2_recurrence(proj, par, batch, seq)


def _fox_proj_kernel(h_ref, g_ref, w_ref, wvt_ref, wf_ref, fb_ref, qg_ref, kg_ref, gs_ref, gst_ref, tril_ref,
                     place_ref, q_ref, gate_ref, ka_ref, vt_ref, cum_ref, carry_ref, *, tiles_per_seq):
    d = D_MODEL

    @pl.when(pl.program_id(0) % tiles_per_seq == 0)
    def _():
        carry_ref[...] = jnp.zeros_like(carry_ref)

    xn = _rms(h_ref[...], g_ref[...])
    xb = xn.astype(BF16)

    def headnorm(t, gain):
        shi, slo = _split2(t * t)
        ssq = (jnp.dot(shi, gs_ref[...], preferred_element_type=F32)
               + jnp.dot(slo, gs_ref[...], preferred_element_type=F32))
        inv = lax.rsqrt(ssq * (1.0 / FOX_HD) + EPS)
        ihi, ilo = _split2(inv)
        invf = (jnp.dot(ihi, gst_ref[...], preferred_element_type=F32)
                + jnp.dot(ilo, gst_ref[...], preferred_element_type=F32))
        return t * invf * gain

    q = jnp.dot(xb, w_ref[:, 0:d], preferred_element_type=F32)
    q_ref[...] = (headnorm(q, qg_ref[...]) * (FOX_HD ** -0.5)).astype(q_ref.dtype)
    k = jnp.dot(xb, w_ref[:, d:2 * d], preferred_element_type=F32)
    kn = headnorm(k, kg_ref[...]).astype(BF16)
    gate_ref[...] = jnp.dot(xb, w_ref[:, 3 * d:4 * d], preferred_element_type=F32).astype(gate_ref.dtype)
    vt_ref[...] = lax.dot_general(wvt_ref[...], xb, (((1,), (1,)), ((), ())),
                                  preferred_element_type=F32).astype(vt_ref.dtype)

    x_lo = (xn - xb.astype(F32)).astype(BF16)
    wf_hi, wf_lo = _split2(wf_ref[...])
    fl = _bdot(xb, wf_hi) + (_bdot(x_lo, wf_hi) + _bdot(xb, wf_lo)) + fb_ref[...]
    hi, mid, lo = _split3(_log_sigmoid(fl))
    tril = tril_ref[...]
    cum = (jnp.dot(tril, hi, preferred_element_type=F32)
           + jnp.dot(tril, mid, preferred_element_type=F32)
           + jnp.dot(tril, lo, preferred_element_type=F32)) + carry_ref[0:1, :]
    cum_ref[...] = cum
    tm = cum.shape[0]
    carry_ref[...] = jnp.broadcast_to(cum[tm - 1:tm, :], carry_ref.shape)
    nhi, nmid, nlo = _split3(-cum)
    feat = (jnp.dot(nhi, place_ref[0], preferred_element_type=F32)
            + jnp.dot(nmid, place_ref[1], preferred_element_type=F32)
            + jnp.dot(nlo, place_ref[2], preferred_element_type=F32)).astype(BF16)
    for hp in range(FOX_HEADS // 2):
        ka_ref[:, hp * 256:hp * 256 + 128] = kn[:, hp * 128:(hp + 1) * 128]
        ka_ref[:, hp * 256 + 128:(hp + 1) * 256] = feat[:, hp * 128:(hp + 1) * 128]


def _fox_proj(h, g_mix, w_in, f_bias, q_norm, k_norm, seq, tm=256):
    n, d = h.shape
    w_main = w_in[:, :4 * d].astype(BF16)
    w_vt = w_in[:, 2 * d:3 * d].T.astype(BF16)
    w_f = jnp.zeros((d, 128), F32).at[:, :FOX_HEADS].set(w_in[:, 4 * d:].astype(F32))
    fb = jnp.zeros((1, 128), F32).at[0, :FOX_HEADS].set(f_bias.astype(F32))
    head_of = np.arange(d) // FOX_HD
    gs_np = (head_of[:, None] == np.arange(128)[None, :]).astype(np.float32)
    gs = jnp.asarray(gs_np, BF16)
    gst = jnp.asarray(gs_np.T, BF16)
    tril = jnp.asarray(np.tril(np.ones((tm, tm), np.float32)), BF16)
    place_np = np.zeros((3, 128, d), np.float32)
    for hd in range(FOX_HEADS):
        for c in range(3):
            place_np[c, hd, (hd // 2) * 128 + 3 * (hd % 2) + c] = 1.0
    place = jnp.asarray(place_np, BF16)
    qg = jnp.tile(q_norm.astype(F32), FOX_HEADS).reshape(1, d)
    kg = jnp.tile(k_norm.astype(F32), FOX_HEADS).reshape(1, d)
    row = lambda i: (i, 0)
    return pl.pallas_call(
        functools.partial(_fox_proj_kernel, tiles_per_seq=seq // tm),
        grid=(n // tm,),
        in_specs=[pl.BlockSpec((tm, d), row), _full((1, d)), _full((d, 4 * d)), _full((d, d)),
                  _full((d, 128)), _full((1, 128)), _full((1, d)), _full((1, d)),
                  _full((d, 128)), _full((128, d)), _full((tm, tm)), _full((3, 128, d))],
        out_specs=[pl.BlockSpec((tm, d), row), pl.BlockSpec((tm, d), row), pl.BlockSpec((tm, 2 * d), row),
                   pl.BlockSpec((d, tm), lambda i: (0, i)), pl.BlockSpec((tm, 128), row)],
        out_shape=[jax.ShapeDtypeStruct((n, d), BF16), jax.ShapeDtypeStruct((n, d), BF16),
                   jax.ShapeDtypeStruct((n, 2 * d), BF16), jax.ShapeDtypeStruct((d, n), BF16),
                   jax.ShapeDtypeStruct((n, 128), F32)],
        scratch_shapes=[pltpu.VMEM((8, 128), F32)],
        compiler_params=_cparams(("arbitrary",)),
        name="fox_proj",
    )(h, g_mix.reshape(1, d), w_main, w_vt, w_f, fb, qg, kg, gs, gst, tril, place)


def _fox_attn_kernel(jlo_ref, q_ref, k_ref, vt_ref, g_ref, o_ref, qh_ref, m_ref, l_ref, acc_ref, *, tq, nq):
    qi = pl.program_id(2)
    lane = lax.broadcasted_iota(I32, (tq, 128), 1)
    qv = q_ref[...]
    for hh in range(2):
        own = (lane < FOX_HD) if hh == 0 else (lane >= FOX_HD)
        qh_ref[hh, :, 0:128] = jnp.where(own, qv, jnp.zeros_like(qv))
        pick = jnp.where(lane < 3 * hh, 0.0, jnp.where(lane < 3 * hh + 3, 1.0, 0.0))
        qh_ref[hh, :, 128:256] = pick.astype(BF16)
    m_ref[...] = jnp.full_like(m_ref, -jnp.inf)
    l_ref[...] = jnp.zeros_like(l_ref)
    acc_ref[...] = jnp.zeros_like(acc_ref)

    def sweep(kj, masked, nblk=1):
        rows = pl.ds(pl.multiple_of(kj * tq, tq), nblk * tq)
        kb = k_ref[rows, :]
        vt = vt_ref[:, rows]
        sts = [lax.dot_general(kb, qh_ref[hh], (((1,), (1,)), ((), ())), preferred_element_type=F32)
               for hh in range(2)]
        ps, alphas = [], []
        for hh in range(2):
            st = sts[hh]
            if masked:
                key = lax.broadcasted_iota(I32, st.shape, 0)
                qry = lax.broadcasted_iota(I32, st.shape, 1)
                st = jnp.where(key <= qry, st, -jnp.inf)
            m_old = m_ref[hh]
            m_new = jnp.maximum(m_old, jnp.max(st, axis=0, keepdims=True))
            alpha = jnp.exp(m_old - m_new)
            p = jnp.exp(st - m_new[0:1, :])
            l_ref[hh] = alpha * l_ref[hh] + jnp.sum(p, axis=0, keepdims=True)
            m_ref[hh] = m_new
            ps.append(p.astype(BF16))
            alphas.append(alpha[0:1, :])
        for hh in range(2):
            acc_ref[hh] = alphas[hh] * acc_ref[hh] + jnp.dot(
                vt[hh * FOX_HD:(hh + 1) * FOX_HD, :], ps[hh], preferred_element_type=F32)

    lo = jlo_ref[(pl.program_id(0) * pl.num_programs(1) + pl.program_id(1)) * nq + qi]
    span = qi - lo

    def body(i, carry):
        sweep(lo + 2 * i, False, 2)
        return carry

    lax.fori_loop(0, span // 2, body, 0)

    @pl.when(span % 2 == 1)
    def _():
        sweep(qi - 1, False)

    sweep(qi, True)
    ot = jnp.concatenate([acc_ref[0] / l_ref[0][0:1, :], acc_ref[1] / l_ref[1][0:1, :]], axis=0)
    gt = g_ref[...].astype(F32)
    o_ref[...] = (ot.T * jax.nn.sigmoid(gt)).astype(o_ref.dtype)


def _fox_first_block(cum_t, logit_bound, tq):
    bh, _, seq = cum_t.shape
    nq = seq // tq
    blk = cum_t.reshape(bh, 2, nq, tq)
    gap = blk[:, :, :, None, 0] - blk[:, :, None, :, tq - 1]
    dead = jnp.all(2.0 * logit_bound + gap < FOX_ZERO_EXP, axis=1)
    dead = dead & (jnp.arange(nq)[None, None, :] < jnp.arange(nq)[None, :, None])
    return jnp.sum(jnp.cumprod(dead.astype(I32), axis=-1), axis=-1).astype(I32).reshape(-1)


def _fox_attention(q, gate, ka, vt, cum, logit_bound, batch, seq, tq=512):
    n = batch * seq
    nq = seq // tq
    hp = FOX_HEADS // 2
    cum_t = cum[:, :FOX_HEADS].reshape(batch, seq, hp, 2).transpose(0, 2, 3, 1).reshape(batch * hp, 2, seq)
    jlo = _fox_first_block(cum_t, logit_bound, tq)
    grid_spec = pltpu.PrefetchScalarGridSpec(
        num_scalar_prefetch=1,
        grid=(batch, hp, nq),
        in_specs=[
            pl.BlockSpec((tq, 128), lambda b, h, i, jlo: (b * nq + i, h)),
            pl.BlockSpec((seq, 256), lambda b, h, i, jlo: (b, h)),
            pl.BlockSpec((128, seq), lambda b, h, i, jlo: (h, b)),
            pl.BlockSpec((tq, 128), lambda b, h, i, jlo: (b * nq + i, h)),
        ],
        out_specs=pl.BlockSpec((tq, 128), lambda b, h, i, jlo: (b * nq + i, h)),
        scratch_shapes=[pltpu.VMEM((2, tq, 256), BF16), pltpu.VMEM((2, 8, tq), F32),
                        pltpu.VMEM((2, 8, tq), F32), pltpu.VMEM((2, FOX_HD, tq), F32)],
    )
    return pl.pallas_call(
        functools.partial(_fox_attn_kernel, tq=tq, nq=nq),
        grid_spec=grid_spec,
        out_shape=jax.ShapeDtypeStruct((n, D_MODEL), BF16),
        compiler_params=_cparams(("parallel", "parallel", "arbitrary")),
        name="fox_attention",
    )(jlo, q, ka, vt, gate)


def _fox_mixer(h, g_mix, w_in, f_bias, q_norm, k_norm, batch, seq):
    q, gate, ka, vt, cum = _fox_proj(h, g_mix, w_in, f_bias, q_norm, k_norm, seq)
    logit_bound = 1.02 * FOX_HD ** 0.5 * jnp.max(jnp.abs(q_norm.astype(F32))) * jnp.max(jnp.abs(k_norm.astype(F32)))
    return _fox_attention(q, gate, ka, vt, cum, logit_bound, batch, seq)


def _rglru_kernel(gate_ref, u_ref, par_ref, wa_ref, wx_ref, o_ref, prev_ref, hc_ref, *, tt):
    @pl.when(pl.program_id(1) == 0)
    def _():
        prev_ref[...] = jnp.zeros_like(prev_ref)
        hc_ref[...] = jnp.zeros_like(hc_ref)

    u = u_ref[...]
    ext = jnp.concatenate([prev_ref[...], u], axis=0)
    conv = par_ref[4:5, :] + u * par_ref[3:4, :]
    for shift in range(1, CONV_W):
        conv = conv + pltpu.roll(ext, shift, 0)[8:, :] * par_ref[3 - shift:4 - shift, :]
    prev_ref[...] = u[tt - 8:, :]

    cb = conv.astype(BF16)
    ra, ia = [], []
    for nb in range(RG_BLOCKS):
        blk = cb[:, nb * RG_BW:(nb + 1) * RG_BW]
        ra.append(jnp.dot(blk, wa_ref[nb], preferred_element_type=F32))
        ia.append(jnp.dot(blk, wx_ref[nb], preferred_element_type=F32))
    r = jax.nn.sigmoid(jnp.concatenate(ra, axis=1) + par_ref[5:6, :])
    ig = jax.nn.sigmoid(jnp.concatenate(ia, axis=1) + par_ref[6:7, :])
    lam = par_ref[7:8, :]
    softplus = jnp.maximum(-lam, 0.0) + jnp.log1p(jnp.exp(-jnp.abs(lam)))
    log_a = -RG_C * r * softplus
    a = jnp.exp(log_a)
    b = jnp.sqrt(1.0 - a * a) * (ig * conv)

    a = a.reshape(tt // 8, 8, a.shape[-1])
    b = b.reshape(a.shape)
    within = lax.broadcasted_iota(I32, a.shape, 1)
    for dist in (1, 2, 4):
        ok = within >= dist
        a_sh = pltpu.roll(a, dist, 1)
        b_sh = pltpu.roll(b, dist, 1)
        b = jnp.where(ok, a * b_sh + b, b)
        a = jnp.where(ok, a * a_sh, a)
    a = a.reshape(tt, a.shape[-1])
    b = b.reshape(a.shape)
    gt = gate_ref[...]
    gelu = 0.5 * gt * (1.0 + jnp.tanh(0.7978845608028654 * (gt + 0.044715 * gt * gt * gt)))
    h = hc_ref[0:1, :]
    for g in range(tt // 8):
        sl = slice(g * 8, (g + 1) * 8)
        hg = b[sl] + a[sl] * h
        o_ref[sl, :] = (hg * gelu[sl]).astype(o_ref.dtype)
        h = hg[7:8, :]
    hc_ref[...] = jnp.broadcast_to(h, hc_ref.shape)


def _rglru_mixer(h, g_mix, w_in, conv_w, conv_b, w_a, b_a, w_x, b_x, lam, batch, seq, tt=256):
    n = batch * seq
    nt = seq // tt
    w = D_MODEL
    proj = _norm_proj(h, g_mix, w_in.astype(BF16))
    par = jnp.concatenate([conv_w.astype(F32), conv_b.reshape(1, w), b_a.reshape(1, w),
                           b_x.reshape(1, w), lam.reshape(1, w)], axis=0).astype(F32)
    return pl.pallas_call(
        functools.partial(_rglru_kernel, tt=tt),
        grid=(batch, nt),
        in_specs=[pl.BlockSpec((tt, w), lambda b, i: (b * nt + i, 0)),
                  pl.BlockSpec((tt, w), lambda b, i: (b * nt + i, 1)),
                  _full((8, w)), _full((RG_BLOCKS, RG_BW, RG_BW)), _full((RG_BLOCKS, RG_BW, RG_BW))],
        out_specs=pl.BlockSpec((tt, w), lambda b, i: (b * nt + i, 0)),
        out_shape=jax.ShapeDtypeStruct((n, w), BF16),
        scratch_shapes=[pltpu.VMEM((8, w), F32), pltpu.VMEM((8, w), F32)],
        compiler_params=_cparams(("parallel", "arbitrary")),
        name="rglru",
    )(proj, proj, par, w_a.astype(BF16), w_x.astype(BF16))


def _outproj_router_kernel(a_ref, h_ref, wo_ref, g_ref, wr_ref, br_ref, tri_ref, lst_ref,
                           hm_ref, xn_ref, gate_ref, pos_ref, cnt_ref):
    hm = h_ref[...] + jnp.dot(a_ref[...], wo_ref[...], preferred_element_type=F32)
    hm_ref[...] = hm
    xn = _rms(hm, g_ref[...])
    x_hi, x_lo = _split2(xn)
    xn_ref[...] = x_hi
    w_hi, w_lo = _split2(wr_ref[...])
    logit = _bdot_nt(w_hi, x_hi) + (_bdot_nt(w_hi, x_lo) + _bdot_nt(w_lo, x_hi)) + br_ref[:, 0:1]
    ne, tm = logit.shape
    eidx = lax.broadcasted_iota(I32, (ne, tm), 0)
    work = logit
    vals, hots = [], []
    for _ in range(TOP_K):
        mx = jnp.max(work, axis=0, keepdims=True)
        pick = jnp.min(jnp.where(work == mx, eidx, ne), axis=0, keepdims=True)
        hot = eidx == pick
        work = jnp.where(hot, -jnp.inf, work)
        vals.append(mx)
        hots.append(hot)
    ex = [jnp.exp(v - vals[0]) for v in vals]
    den = ex[0] + ex[1] + ex[2] + ex[3]
    gate_ref[...] = jnp.concatenate([e / den for e in ex], axis=0)
    tok_hot = (hots[0] | hots[1] | hots[2] | hots[3]).astype(BF16)
    tri = tri_ref[...]
    count = jnp.zeros((ne, 1), F32)
    pieces = []
    for j in range(tm // 128):
        hb = tok_hot[:, j * 128:(j + 1) * 128]
        pieces.append(jnp.dot(hb, tri, preferred_element_type=F32) + count)
        count = count + jnp.sum(hb.astype(F32), axis=1, keepdims=True)
    before = jnp.concatenate(pieces, axis=1)
    units = jnp.floor((count + (STRIP_ALIGN - 1)) * (1.0 / STRIP_ALIGN))
    units_b = jnp.broadcast_to(units, (ne, 128)).astype(BF16)
    start = STRIP_ALIGN * jnp.dot(lst_ref[...], units_b, preferred_element_type=F32)[:, 0:1]
    where = before + start
    pos_ref[...] = jnp.concatenate(
        [jnp.sum(jnp.where(h, where, 0.0), axis=0, keepdims=True) for h in hots], axis=0).astype(I32)
    cnt_ref[...] = jnp.broadcast_to(STRIP_ALIGN * units, cnt_ref.shape)


def _outproj_router(act, h, w_out, g_ffn, w_r, b_r):
    n, d = h.shape
    tm = MOE_TILE
    tri = jnp.asarray(np.triu(np.ones((128, 128), np.float32), 1), BF16)
    lst =jnp.asarray(np.tril(np.ones((N_EXPERTS, N_EXPERTS), np.float32), -1), BF16)
    row = lambda i: (i, 0)
    colb = lambda i: (0, i)
    return pl.pallas_call(
        _outproj_router_kernel,
        grid=(n // tm,),
        in_specs=[pl.BlockSpec((tm, d), row), pl.BlockSpec((tm, d), row), _full((d, d)), _full((1, d)),
                  _full((N_EXPERTS, d)), _full((N_EXPERTS, 128)), _full((128, 128)),
                  _full((N_EXPERTS, N_EXPERTS))],
        out_specs=[pl.BlockSpec((tm, d), row), pl.BlockSpec((tm, d), row),
                   pl.BlockSpec((TOP_K, tm), colb), pl.BlockSpec((TOP_K, tm), colb),
                   pl.BlockSpec((N_EXPERTS, 128), row)],
        out_shape=[jax.ShapeDtypeStruct((n, d), F32), jax.ShapeDtypeStruct((n, d), BF16),
                   jax.ShapeDtypeStruct((TOP_K, n), F32), jax.ShapeDtypeStruct((TOP_K, n), I32),
                   jax.ShapeDtypeStruct((n // tm * N_EXPERTS, 128), F32)],
        compiler_params=_cparams(("parallel",)),
        name="outproj_router",
    )(act, h, w_out.astype(BF16), g_ffn.reshape(1, d), w_r.T.astype(F32),
      jnp.broadcast_to(b_r.astype(F32)[:, None], (N_EXPERTS, 128)), tri, lst)


def _for_strip_chunks(nrows, max_rows, fn):
    def pieces(top):
        size = top
        while size >= STRIP_ALIGN:
            @pl.when((nrows & size) != 0)
            def _(size=size):
                fn(pl.multiple_of(nrows & ~(2 * size - 1), STRIP_ALIGN), size)
            size //= 2

    if max_rows > SHORT_STRIP:
        @pl.when(nrows < 2 * SHORT_STRIP)
        def _():
            pieces(SHORT_STRIP)

        @pl.when(nrows >= 2 * SHORT_STRIP)
        def _():
            pieces(max_rows)
    else:
        pieces(max_rows)


def _for_tile_strips(n8_ref, sbase_ref, gdst_ref, tile, fn):
    def per_expert(e, carry):
        i = tile * N_EXPERTS + e
        so = sbase_ref[i]
        gd = gdst_ref[i]
        _for_strip_chunks(n8_ref[i], MOE_TILE,
                          lambda o, size: fn(pl.multiple_of(so + o, STRIP_ALIGN), pl.multiple_of(gd + o, STRIP_ALIGN), size))
        return carry

    lax.fori_loop(0, N_EXPERTS, per_expert, 0)


def _wait_tile_strips(n8_ref, sbase_ref, tile, wait_rows):
    last = tile * N_EXPERTS + (N_EXPERTS - 1)
    total = sbase_ref[last] + n8_ref[last]
    size = STAGE_ROWS.bit_length() - 1
    size = 1 << size
    while size >= STRIP_ALIGN:
        @pl.when((total & size) != 0)
        def _(size=size):
            wait_rows(size)
        size //= 2


def _dispatch_kernel(n8_ref, sbase_ref, gdst_ref, tot_ref, pst_ref, x_ref, pos_ref, xs_hbm,
                     stage, zblk, rsem, zsem):
    tile = pl.program_id(0)
    slot = tile % 2
    xb = x_ref[...]
    for c in range(STAGE_ROWS // ONEHOT_CHUNK):
        rows = lax.broadcasted_iota(I32, (ONEHOT_CHUNK, MOE_TILE), 0) + c * ONEHOT_CHUNK
        onehot = jnp.zeros((ONEHOT_CHUNK, MOE_TILE), F32)
        for k in range(TOP_K):
            onehot = jnp.where(rows == pos_ref[k:k + 1, :], 1.0, onehot)
        stage[slot, c * ONEHOT_CHUNK:(c + 1) * ONEHOT_CHUNK, :] = jnp.dot(
            onehot.astype(BF16), xb, preferred_element_type=F32)

    def copy(sl, so, gd, size):
        return pltpu.make_async_copy(stage.at[sl, pl.ds(so, size), :], xs_hbm.at[pl.ds(gd, size), :], rsem.at[sl])

    _for_tile_strips(n8_ref, sbase_ref, gdst_ref, tile, lambda so, gd, size: copy(slot, so, gd, size).start())

    @pl.when(tile > 0)
    def _():
        _wait_tile_strips(n8_ref, sbase_ref, tile - 1, lambda size: copy(1 - slot, 0, 0, size).wait())

    @pl.when(tile == 0)
    def _():
        zblk[...] = jnp.zeros_like(zblk)

        def zero(first, size):
            cp = pltpu.make_async_copy(zblk.at[pl.ds(0, size), :], xs_hbm.at[pl.ds(first, size), :], zsem)
            cp.start()
            cp.wait()

        def per_expert(e, carry):
            tot = tot_ref[e]
            first = pst_ref[e] + tot
            npad = ((tot + (MOE_BLK - 1)) // MOE_BLK) * MOE_BLK - tot
            _for_strip_chunks(npad, MOE_BLK // 2,
                              lambda o, size: zero(pl.multiple_of(first + o, STRIP_ALIGN), size))
            return carry

        lax.fori_loop(0, N_EXPERTS, per_expert, 0)

        last = N_EXPERTS - 1
        used = (pst_ref[last] + tot_ref[last] + (MOE_BLK - 1)) // MOE_BLK

        def ztail(b, c):
            zero(pl.multiple_of(b * MOE_BLK, MOE_BLK), MOE_BLK)
            return c

        lax.fori_loop(used, xs_hbm.shape[0] // MOE_BLK, ztail, 0)

    @pl.when(tile == pl.num_programs(0) - 1)
    def _():
        _wait_tile_strips(n8_ref, sbase_ref, tile, lambda size: copy(slot, 0, 0, size).wait())


def _dispatch(xn, pos, n8, sbase, gdst, total, pstart, cap):
    n, d = xn.shape
    tm = MOE_TILE
    grid_spec = pltpu.PrefetchScalarGridSpec(
        num_scalar_prefetch=5,
        grid=(n // tm,),
        in_specs=[pl.BlockSpec((tm, d), lambda i, *_: (i, 0)), pl.BlockSpec((TOP_K, tm), lambda i, *_: (0, i))],
        out_specs=pl.BlockSpec(memory_space=pl.ANY),
        scratch_shapes=[pltpu.VMEM((2, STAGE_ROWS, d), F32), pltpu.VMEM((MOE_BLK, d), F32),
                        pltpu.SemaphoreType.DMA((2,)), pltpu.SemaphoreType.DMA],
    )
    return pl.pallas_call(
        _dispatch_kernel,
        grid_spec=grid_spec,
        out_shape=jax.ShapeDtypeStruct((cap, d), F32),
        compiler_params=_cparams(("arbitrary",)),
        name="moe_dispatch",
    )(n8, sbase, gdst, total, pstart, xn, pos)


def _expert_kernel(be_ref, nu_ref, x_ref, wgu_ref, bgu_ref, wdn_ref, bdn_ref, y_ref, wgu_bf, wdn_bf):
    d = D_MODEL
    b = pl.program_id(0)

    @pl.when((b == 0) | (be_ref[b] != be_ref[jnp.maximum(b - 1, 0)]))
    def _():
        wgu_bf[...] = wgu_ref[...].astype(BF16)
        wdn_bf[...] = wdn_ref[...].astype(BF16)

    @pl.when(b < nu_ref[0])
    def _():
        xb = x_ref[...].astype(BF16)
        acc = None
        for c in range(d // EXPERT_CHUNK):
            lo, hi = c * EXPERT_CHUNK, (c + 1) * EXPERT_CHUNK
            glu = jnp.dot(xb, wgu_bf[:, lo:hi], preferred_element_type=F32) + bgu_ref[:, lo:hi]
            lin = jnp.dot(xb, wgu_bf[:, d + lo:d + hi], preferred_element_type=F32) + bgu_ref[:, d + lo:d + hi]
            glu = jnp.minimum(glu, SWIGLU_LIMIT)
            lin = jnp.clip(lin, -SWIGLU_LIMIT, SWIGLU_LIMIT)
            act = glu * jax.nn.sigmoid(SWIGLU_ALPHA * glu) * (lin + 1.0)
            part = jnp.dot(act.astype(BF16), wdn_bf[lo:hi, :], preferred_element_type=F32)
            acc = part if acc is None else acc + part
        y_ref[...] = acc + bdn_ref[...]

    @pl.when(b >= nu_ref[0])
    def _():
        y_ref[...] = jnp.zeros_like(y_ref)


def _experts(xs, block_expert, n_used, w_gu, b_gu, w_dn, b_dn):
    cap, d = xs.shape
    nb = cap // MOE_BLK
    ne = w_gu.shape[0]

    def xmap(b, be, nu):
        return (jnp.minimum(b, nu[0] - 1), 0)

    def wmap(b, be, nu):
        return (be[b], 0, 0)

    grid_spec = pltpu.PrefetchScalarGridSpec(
        num_scalar_prefetch=2,
        grid=(nb,),
        in_specs=[pl.BlockSpec((MOE_BLK, d), xmap),
                  pl.BlockSpec((None, d, 2 * d), wmap), pl.BlockSpec((None, 1, 2 * d), wmap),
                  pl.BlockSpec((None, d, d), wmap), pl.BlockSpec((None, 1, d), wmap)],
        out_specs=pl.BlockSpec((MOE_BLK, d), lambda b, be, nu: (b, 0)),
        scratch_shapes=[pltpu.VMEM((d, 2 * d), BF16), pltpu.VMEM((d, d), BF16)],
    )
    return pl.pallas_call(
        _expert_kernel,
        grid_spec=grid_spec,
        out_shape=jax.ShapeDtypeStruct((cap, d), F32),
        compiler_params=_cparams(("arbitrary",)),
        name="moe_experts",
    )(block_expert, n_used, xs, w_gu, b_gu.reshape(ne, 1, 2 * d), w_dn, b_dn.reshape(ne, 1, d))


def _combine_ple_kernel(n8_ref, sbase_ref, gdst_ref, y_hbm, hm_ref, gate_ref, pos_ref, p_ref, pw_ref, pn_ref,
                        gn_ref, gw_ref, o_ref, stage, rsem):
    tile = pl.program_id(0)
    slot = tile % 2

    def copy(sl, so, gd, size):
        return pltpu.make_async_copy(y_hbm.at[pl.ds(gd, size), :], stage.at[sl, pl.ds(so, size), :], rsem.at[sl])

    def fetch(t, sl):
        _for_tile_strips(n8_ref, sbase_ref, gdst_ref, t, lambda so, gd, size: copy(sl, so, gd, size).start())

    @pl.when(tile == 0)
    def _():
        stage[...] = jnp.zeros_like(stage)
        fetch(0, 0)

    @pl.when(tile + 1 < pl.num_programs(0))
    def _():
        fetch(tile + 1, 1 - slot)

    ple = _rms(jnp.dot(p_ref[...].astype(BF16), pw_ref[...], preferred_element_type=F32), pn_ref[...])
    _wait_tile_strips(n8_ref, sbase_ref, tile, lambda size: copy(slot, 0, 0, size).wait())
    h2 = hm_ref[...]
    for c in range(STAGE_ROWS // COMBINE_CHUNK):
        lanes = lax.broadcasted_iota(I32, (MOE_TILE, COMBINE_CHUNK), 1) + c * COMBINE_CHUNK
        weights = jnp.zeros((MOE_TILE, COMBINE_CHUNK), F32)
        for k in range(TOP_K):
            weights = jnp.where(lanes == pos_ref[:, k:k + 1], gate_ref[:, k:k + 1], weights)
        staged = stage[slot, c * COMBINE_CHUNK:(c + 1) * COMBINE_CHUNK, :].astype(BF16)
        h2 = h2 + jnp.dot(weights.astype(BF16), staged, preferred_element_type=F32)
    gate = jax.nn.sigmoid(jnp.dot(_rms(h2, gn_ref[...]).astype(BF16), gw_ref[...], preferred_element_type=F32))
    o_ref[...] = h2 + ple * gate


def _combine_ple(y, pos, n8, sbase, gdst, h_mid, gates, p, ple_w, ple_norm, gate_norm, gate_w):
    n, d = h_mid.shape
    tm = MOE_TILE
    row = lambda i, *_: (i, 0)
    full = lambda shape: pl.BlockSpec(shape, lambda i, *_: (0,) * len(shape))
    grid_spec = pltpu.PrefetchScalarGridSpec(
        num_scalar_prefetch=3,
        grid=(n // tm,),
        in_specs=[pl.BlockSpec(memory_space=pl.ANY), pl.BlockSpec((tm, d), row), pl.BlockSpec((tm, TOP_K), row),
                  pl.BlockSpec((tm, TOP_K), row), pl.BlockSpec((tm, PLE_DIM), row), full((PLE_DIM, d)),
                  full((1, d)), full((1, d)), full((d, d))],
        out_specs=pl.BlockSpec((tm, d), row),
        scratch_shapes=[pltpu.VMEM((2, STAGE_ROWS, d), F32), pltpu.SemaphoreType.DMA((2,))],
    )
    return pl.pallas_call(
        _combine_ple_kernel,
        grid_spec=grid_spec,
        out_shape=jax.ShapeDtypeStruct((n, d), F32),
        compiler_params=_cparams(("arbitrary",)),
        name="moe_combine_ple",
    )(n8, sbase, gdst, y, h_mid, gates.T, pos.T, p, ple_w.astype(BF16), ple_norm.reshape(1, d),
      gate_norm.reshape(1, d), gate_w.astype(BF16))


def _moe_ple(act, h, w_out, g_ffn, w_r, b_r, layer, w_gu, b_gu, w_dn, b_dn, p, ple_w, ple_norm, gate_norm, gate_w):
    n, d = h.shape
    ntiles = n // MOE_TILE
    h_mid, xn, gates, pos, cnt = _outproj_router(act, h, w_out, g_ffn, w_r, b_r)
    n8 = cnt[:, 0].astype(I32).reshape(ntiles, N_EXPERTS)
    sbase = jnp.cumsum(n8, axis=1) - n8
    total = jnp.sum(n8, axis=0)
    padded = (total + MOE_BLK - 1) // MOE_BLK * MOE_BLK
    pend = jnp.cumsum(padded)
    pstart = pend - padded
    gdst = pstart[None, :] + jnp.cumsum(n8, axis=0) - n8
    nb = -(-(n * TOP_K + ntiles * N_EXPERTS * (STRIP_ALIGN - 1)) // MOE_BLK) + N_EXPERTS
    cap = nb * MOE_BLK
    block_start = jnp.arange(nb, dtype=I32) * MOE_BLK
    block_expert = jnp.minimum(jnp.sum(pend[None, :] <= block_start[:, None], axis=1), N_EXPERTS - 1).astype(I32)
    n_used = (pend[-1:] // MOE_BLK).astype(I32)
    n8f, sbf, gdf = n8.reshape(-1), sbase.reshape(-1).astype(I32), gdst.reshape(-1).astype(I32)
    xs = _dispatch(xn, pos, n8f, sbf, gdf, total.astype(I32), pstart.astype(I32), cap)
    y = _experts(xs, block_expert + layer * N_EXPERTS, n_used, w_gu, b_gu, w_dn, b_dn)
    return _combine_ple(y, pos, n8f, sbf, gdf, h_mid, gates, p, ple_w, ple_norm, gate_norm, gate_w)


def kernel(x, p, norm_mix, norm_ffn, hg_w_in, hg_w_out, hg_gnorm, hg_lb_param, fox_w_in, fox_f_bias, fox_qnorm, fox_knorm, fox_w_out, rg_w_in, rg_conv_w, rg_conv_b, rg_wa, rg_ba, rg_wx, rg_bx, rg_lambda, rg_w_out, router_w, router_b, moe_w_gu, moe_b_gu, moe_w_dn, moe_b_dn, ple_w, ple_norm, ple_gate_norm, ple_gate_w):
    batch, seq, d = x.shape
    depth = p.shape[0]
    n = batch * seq
    lb_all = jnp.cumsum(jax.nn.softmax(hg_lb_param.astype(F32), axis=0), axis=0)
    lb_all = lb_all - lb_all[0]
    h = x.reshape(n, d).astype(F32)
    ne = depth * N_EXPERTS
    w_gu = moe_w_gu.astype(F32).reshape(ne, d, 2 * d)
    b_gu = moe_b_gu.astype(F32).reshape(ne, 2 * d)
    w_dn = moe_w_dn.astype(F32).reshape(ne, d, d)
    b_dn = moe_b_dn.astype(F32).reshape(ne, d)
    for i in range(depth):
        j = i // 3
        kind = i % 3
        if kind == 0:
            act = _hgrn2_mixer(h, norm_mix[i], hg_w_in[j], None, hg_gnorm[j], lb_all[i], batch, seq)
            w_out = hg_w_out[j]
        elif kind == 1:
            act = _fox_mixer(h, norm_mix[i], fox_w_in[j], fox_f_bias[j], fox_qnorm[j], fox_knorm[j], batch, seq)
            w_out = fox_w_out[j]
        else:
            act = _rglru_mixer(h, norm_mix[i], rg_w_in[j], rg_conv_w[j], rg_conv_b[j], rg_wa[j], rg_ba[j],
                               rg_wx[j], rg_bx[j], rg_lambda[j], batch, seq)
            w_out = rg_w_out[j]
        h = _moe_ple(act, h, w_out, norm_ffn[i], router_w[i], router_b[i], i, w_gu, b_gu, w_dn, b_dn,
                     p[i].reshape(n, PLE_DIM), ple_w[i], ple_norm[i], ple_gate_norm[i], ple_gate_w[i])
    return h.reshape(batch, seq, d)
```

```python
import functools

import jax
import jax.numpy as jnp
import numpy as np
from jax import lax
from jax.experimental import pallas as pl
from jax.experimental.pallas import tpu as pltpu

F32 = jnp.float32
BF16 = jnp.bfloat16
I32 = jnp.int32

D_MODEL = 1024
EPS = 1e-6
PLE_DIM = 256

HG_HEADS = 8
HG_DK = 128
HG_CHUNK = 128
HG_LEVELS = 7

FOX_HEADS = 16
FOX_HD = 64
FOX_ZERO_EXP = -105.0

RG_BLOCKS = 4
RG_BW = 256
CONV_W = 4
RG_C = 8.0

N_EXPERTS = 32
TOP_K = 4
SWIGLU_LIMIT = 7.0
SWIGLU_ALPHA = 1.702
MOE_BLK = 512
EXPERT_CHUNK = 512
MOE_TILE = 512
STRIP_ALIGN = 8
STAGE_ROWS = -(-(TOP_K * MOE_TILE + N_EXPERTS * (STRIP_ALIGN - 1)) // 256) * 256
ONEHOT_CHUNK = 256
COMBINE_CHUNK = STAGE_ROWS // 3

VMEM_LIMIT = 56 * 1024 * 1024


def _cparams(sem):
    return pltpu.CompilerParams(dimension_semantics=sem, vmem_limit_bytes=VMEM_LIMIT)


def _bdot(a, b):
    return jnp.dot(a.astype(BF16), b.astype(BF16), preferred_element_type=F32)


def _bdot_nt(a, b):
    return lax.dot_general(a.astype(BF16), b.astype(BF16), (((1,), (1,)), ((), ())),
                           preferred_element_type=F32)


def _bdot_tn(a, b):
    return lax.dot_general(a.astype(BF16), b.astype(BF16), (((0,), (0,)), ((), ())),
                           preferred_element_type=F32)


def _rms(x, g):
    return x * lax.rsqrt(jnp.mean(x * x, axis=-1, keepdims=True) + EPS) * g


def _split2(x):
    hi = x.astype(BF16)
    lo = (x - hi.astype(F32)).astype(BF16)
    return hi, lo


def _split3(x):
    hi = x.astype(BF16)
    r = x - hi.astype(F32)
    mid = r.astype(BF16)
    lo = (r - mid.astype(F32)).astype(BF16)
    return hi, mid, lo


def _log_sigmoid(z):
    return jnp.minimum(z, 0.0) - jnp.log1p(jnp.exp(-jnp.abs(z)))


def _full(shape):
    return pl.BlockSpec(shape, lambda *_: (0,) * len(shape))


def _norm_proj_kernel(h_ref, g_ref, w_ref, o_ref, *, cn):
    xn = _rms(h_ref[...], g_ref[...]).astype(BF16)
    m = w_ref.shape[1]
    for c in range(m // cn):
        o_ref[:, c * cn:(c + 1) * cn] = jnp.dot(
            xn, w_ref[:, c * cn:(c + 1) * cn], preferred_element_type=F32).astype(o_ref.dtype)


def _norm_proj(h, g, w, tm=512, out_dtype=F32):
    n, d = h.shape
    m = w.shape[1]
    return pl.pallas_call(
        functools.partial(_norm_proj_kernel, cn=512),
        grid=(n // tm,),
        in_specs=[pl.BlockSpec((tm, d), lambda i: (i, 0)), _full((1, d)), _full((d, m))],
        out_specs=pl.BlockSpec((tm, m), lambda i: (i, 0)),
        out_shape=jax.ShapeDtypeStruct((n, m), out_dtype),
        compiler_params=_cparams(("parallel",)),
        name="norm_proj",
    )(h, g.reshape(1, d), w)


def _hgrn2_consts():
    c = HG_CHUNK
    t = np.arange(c)
    tril = (t[:, None] >= t[None, :]).astype(np.float32)
    sel = np.zeros((HG_LEVELS, c, c), np.float32)
    for l in range(HG_LEVELS):
        hs = 1 << l
        m = (t // (2 * hs)) * (2 * hs) + hs - 1
        sel[l, t, m] = 1.0
    return jnp.asarray(tril, BF16), jnp.asarray(sel.reshape(HG_LEVELS * c, c), BF16)


def _hgrn2_kernel(q_ref, z_ref, v_ref, g_ref, par_ref, tril_ref, sel_ref, o_ref, st_ref, *, nchunk):
    c = HG_CHUNK

    @pl.when(pl.program_id(2) == 0)
    def _():
        st_ref[...] = jnp.zeros_like(st_ref)

    log_lb = par_ref[0:1, :]
    log1m_lb = par_ref[1:2, :]
    one_m_lb = par_ref[2:3, :]
    gnorm = par_ref[3:4, :]
    row = lax.broadcasted_iota(I32, (c, c), 0)
    col = lax.broadcasted_iota(I32, (c, c), 1)

    cs = range(nchunk)
    rs = [slice(ci * c, (ci + 1) * c) for ci in cs]
    tril = tril_ref[...]
    q, k, vb, cum, refs, a = [], [], [], [], [], []
    for r in rs:
        qr = q_ref[r, :]
        z = z_ref[r, :]
        vb.append(v_ref[r, :].astype(BF16))
        q.append(qr * jax.nn.sigmoid(qr) * (HG_DK ** -0.5))
        b = log1m_lb + _log_sigmoid(z)
        lf = jnp.maximum(log_lb, b) + jnp.log1p(jnp.exp(-jnp.abs(log_lb - b)))
        k.append(one_m_lb * jax.nn.sigmoid(-z))
        hi, mid, lo = _split3(lf)
        cum.append(jnp.dot(tril, hi, preferred_element_type=F32)
                   + jnp.dot(tril, mid, preferred_element_type=F32)
                   + jnp.dot(tril, lo, preferred_element_type=F32))
    for i in cs:
        chi, clo = _split2(cum[i])
        refs.append(jnp.dot(sel_ref[...], chi, preferred_element_type=F32)
                    + jnp.dot(sel_ref[...], clo, preferred_element_type=F32))
        a.append(jnp.where(row == col, _bdot_nt(q[i], k[i]), 0.0))
    for l in range(HG_LEVELS):
        hs = 1 << l
        mask = (((row ^ col) >> l) == 1) & (row > col)
        for i in cs:
            ref = refs[i][l * c:(l + 1) * c, :]
            if hs >= 8:
                parts = []
                for blk in range(c // hs):
                    sl = slice(blk * hs, (blk + 1) * hs)
                    if blk % 2:
                        parts.append(q[i][sl] * jnp.exp(jnp.minimum(cum[i][sl] - ref[sl], 0.0)))
                    else:
                        parts.append(k[i][sl] * jnp.exp(jnp.minimum(ref[sl] - cum[i][sl], 0.0)))
                x = jnp.concatenate(parts, axis=0).astype(BF16)
            else:
                x = (jnp.where((row & hs) != 0, q[i], k[i]) * jnp.exp(-jnp.abs(cum[i] - ref))).astype(BF16)
            a[i] = jnp.where(mask, _bdot_nt(x, x), a[i])
    intra = [_bdot(a[i], vb[i]) for i in cs]
    last = [cum[i][c - 1:c, :] for i in cs]
    qe = [(q[i] * jnp.exp(cum[i])).astype(BF16) for i in cs]
    kd = [(k[i] * jnp.exp(last[i] - cum[i])).astype(BF16) for i in cs]
    st = st_ref[...]
    for i in cs:
        o = intra[i] + _bdot_nt(qe[i], st)
        st = st * jnp.exp(last[i]) + _bdot_tn(vb[i], kd[i])
        gt = g_ref[rs[i], :]
        y = _rms(o, gnorm) * (gt * jax.nn.sigmoid(gt))
        o_ref[rs[i], :] = y.astype(o_ref.dtype)
    st_ref[...] = st


def _hgrn2_recurrence(proj, par, batch, seq, tt=1024):
    n = batch * seq
    nt = seq // tt
    tril, sel = _hgrn2_consts()

    def part(p):
        return pl.BlockSpec((tt, HG_DK), lambda b, h, i, p=p: (b * nt + i, p * HG_HEADS + h))

    return pl.pallas_call(
        functools.partial(_hgrn2_kernel, nchunk=tt // HG_CHUNK),
        grid=(batch, HG_HEADS, nt),
        in_specs=[part(0), part(1), part(2), part(3),
                  pl.BlockSpec((8, HG_DK), lambda b, h, i: (0, h)),
                  _full(tril.shape), _full(sel.shape)],
        out_specs=pl.BlockSpec((tt, HG_DK), lambda b, h, i: (b * nt + i, h)),
        out_shape=jax.ShapeDtypeStruct((n, D_MODEL), BF16),
        scratch_shapes=[pltpu.VMEM((HG_DK, HG_DK), F32)],
        compiler_params=_cparams(("parallel", "parallel", "arbitrary")),
        name="hgrn2_recurrence",
    )(proj, proj, proj, proj, par, tril, sel)


def _hgrn2_mixer(h, g_mix, w_in, w_out_unused, g_norm, lb, batch, seq):
    del w_out_unused
    proj = _norm_proj(h, g_mix, w_in.astype(BF16))
    par = jnp.zeros((8, D_MODEL), F32)
    par = par.at[0].set(jnp.log(lb)).at[1].set(jnp.log1p(-lb)).at[2].set(1.0 - lb)
    par = par.at[3].set(jnp.tile(g_norm.astype(F32), HG_HEADS))
    return _hgrn2_recurrence(proj, par, batch, seq)


def _fox_proj_kernel(h_ref, g_ref, w_ref, wvt_ref, wf_ref, fb_ref, qg_ref, kg_ref, gs_ref, gst_ref, tril_ref,
                     place_ref, q_ref, gate_ref, ka_ref, vt_ref, cum_ref, carry_ref, *, tiles_per_seq):
    d = D_MODEL

    @pl.when(pl.program_id(0) % tiles_per_seq == 0)
    def _():
        carry_ref[...] = jnp.zeros_like(carry_ref)

    xn = _rms(h_ref[...], g_ref[...])
    xb = xn.astype(BF16)

    def headnorm(t, gain):
        shi, slo = _split2(t * t)
        ssq = (jnp.dot(shi, gs_ref[...], preferred_element_type=F32)
               + jnp.dot(slo, gs_ref[...], preferred_element_type=F32))
        inv = lax.rsqrt(ssq * (1.0 / FOX_HD) + EPS)
        ihi, ilo = _split2(inv)
        invf = (jnp.dot(ihi, gst_ref[...], preferred_element_type=F32)
                + jnp.dot(ilo, gst_ref[...], preferred_element_type=F32))
        return t * invf * gain

    q = jnp.dot(xb, w_ref[:, 0:d], preferred_element_type=F32)
    q_ref[...] = (headnorm(q, qg_ref[...]) * (FOX_HD ** -0.5)).astype(q_ref.dtype)
    k = jnp.dot(xb, w_ref[:, d:2 * d], preferred_element_type=F32)
    kn = headnorm(k, kg_ref[...]).astype(BF16)
    gate_ref[...] = jnp.dot(xb, w_ref[:, 3 * d:4 * d], preferred_element_type=F32).astype(gate_ref.dtype)
    vt_ref[...] = lax.dot_general(wvt_ref[...], xb, (((1,), (1,)), ((), ())),
                                  preferred_element_type=F32).astype(vt_ref.dtype)

    x_lo = (xn - xb.astype(F32)).astype(BF16)
    wf_hi, wf_lo = _split2(wf_ref[...])
    fl = _bdot(xb, wf_hi) + (_bdot(x_lo, wf_hi) + _bdot(xb, wf_lo)) + fb_ref[...]
    hi, mid, lo = _split3(_log_sigmoid(fl))
    tril = tril_ref[...]
    cum = (jnp.dot(tril, hi, preferred_element_type=F32)
           + jnp.dot(tril, mid, preferred_element_type=F32)
           + jnp.dot(tril, lo, preferred_element_type=F32)) + carry_ref[0:1, :]
    cum_ref[...] = cum
    tm = cum.shape[0]
    carry_ref[...] = jnp.broadcast_to(cum[tm - 1:tm, :], carry_ref.shape)
    nhi, nmid, nlo = _split3(-cum)
    feat = (jnp.dot(nhi, place_ref[0], preferred_element_type=F32)
            + jnp.dot(nmid, place_ref[1], preferred_element_type=F32)
            + jnp.dot(nlo, place_ref[2], preferred_element_type=F32)).astype(BF16)
    for hp in range(FOX_HEADS // 2):
        ka_ref[:, hp * 256:hp * 256 + 128] = kn[:, hp * 128:(hp + 1) * 128]
        ka_ref[:, hp * 256 + 128:(hp + 1) * 256] = feat[:, hp * 128:(hp + 1) * 128]


def _fox_proj(h, g_mix, w_in, f_bias, q_norm, k_norm, seq, tm=256):
    n, d = h.shape
    w_main = w_in[:, :4 * d].astype(BF16)
    w_vt = w_in[:, 2 * d:3 * d].T.astype(BF16)
    w_f = jnp.zeros((d, 128), F32).at[:, :FOX_HEADS].set(w_in[:, 4 * d:].astype(F32))
    fb = jnp.zeros((1, 128), F32).at[0, :FOX_HEADS].set(f_bias.astype(F32))
    head_of = np.arange(d) // FOX_HD
    gs_np = (head_of[:, None] == np.arange(128)[None, :]).astype(np.float32)
    gs = jnp.asarray(gs_np, BF16)
    gst = jnp.asarray(gs_np.T, BF16)
    tril = jnp.asarray(np.tril(np.ones((tm, tm), np.float32)), BF16)
    place_np = np.zeros((3, 128, d), np.float32)
    for hd in range(FOX_HEADS):
        for c in range(3):
            place_np[c, hd, (hd // 2) * 128 + 3 * (hd % 2) + c] = 1.0
    place = jnp.asarray(place_np, BF16)
    qg = jnp.tile(q_norm.astype(F32), FOX_HEADS).reshape(1, d)
    kg = jnp.tile(k_norm.astype(F32), FOX_HEADS).reshape(1, d)
    row = lambda i: (i, 0)
    return pl.pallas_call(
        functools.partial(_fox_proj_kernel, tiles_per_seq=seq // tm),
        grid=(n // tm,),
        in_specs=[pl.BlockSpec((tm, d), row), _full((1, d)), _full((d, 4 * d)), _full((d, d)),
                  _full((d, 128)), _full((1, 128)), _full((1, d)), _full((1, d)),
                  _full((d, 128)), _full((128, d)), _full((tm, tm)), _full((3, 128, d))],
        out_specs=[pl.BlockSpec((tm, d), row), pl.BlockSpec((tm, d), row), pl.BlockSpec((tm, 2 * d), row),
                   pl.BlockSpec((d, tm), lambda i: (0, i)), pl.BlockSpec((tm, 128), row)],
        out_shape=[jax.ShapeDtypeStruct((n, d), BF16), jax.ShapeDtypeStruct((n, d), BF16),
                   jax.ShapeDtypeStruct((n, 2 * d), BF16), jax.ShapeDtypeStruct((d, n), BF16),
                   jax.ShapeDtypeStruct((n, 128), F32)],
        scratch_shapes=[pltpu.VMEM((8, 128), F32)],
        compiler_params=_cparams(("arbitrary",)),
        name="fox_proj",
    )(h, g_mix.reshape(1, d), w_main, w_vt, w_f, fb, qg, kg, gs, gst, tril, place)


def _fox_attn_kernel(jlo_ref, q_ref, k_ref, vt_ref, g_ref, o_ref, qh_ref, m_ref, l_ref, acc_ref, *, tq, nq):
    qi = pl.program_id(2)
    lane = lax.broadcasted_iota(I32, (tq, 128), 1)
    qv = q_ref[...]
    for hh in range(2):
        own = (lane < FOX_HD) if hh == 0 else (lane >= FOX_HD)
        qh_ref[hh, :, 0:128] = jnp.where(own, qv, jnp.zeros_like(qv))
        pick = jnp.where(lane < 3 * hh, 0.0, jnp.where(lane < 3 * hh + 3, 1.0, 0.0))
        qh_ref[hh, :, 128:256] = pick.astype(BF16)
    m_ref[...] = jnp.full_like(m_ref, -jnp.inf)
    l_ref[...] = jnp.zeros_like(l_ref)
    acc_ref[...] = jnp.zeros_like(acc_ref)

    def sweep(kj, masked, nblk=1):
        rows = pl.ds(pl.multiple_of(kj * tq, tq), nblk * tq)
        kb = k_ref[rows, :]
        vt = vt_ref[:, rows]
        sts = [lax.dot_general(kb, qh_ref[hh], (((1,), (1,)), ((), ())), preferred_element_type=F32)
               for hh in range(2)]
        ps, alphas = [], []
        for hh in range(2):
            st = sts[hh]
            if masked:
                key = lax.broadcasted_iota(I32, st.shape, 0)
                qry = lax.broadcasted_iota(I32, st.shape, 1)
                st = jnp.where(key <= qry, st, -jnp.inf)
            m_old = m_ref[hh]
            m_new = jnp.maximum(m_old, jnp.max(st, axis=0, keepdims=True))
            alpha = jnp.exp(m_old - m_new)
            p = jnp.exp(st - m_new[0:1, :])
            l_ref[hh] = alpha * l_ref[hh] + jnp.sum(p, axis=0, keepdims=True)
            m_ref[hh] = m_new
            ps.append(p.astype(BF16))
            alphas.append(alpha[0:1, :])
        for hh in range(2):
            acc_ref[hh] = alphas[hh] * acc_ref[hh] + jnp.dot(
                vt[hh * FOX_HD:(hh + 1) * FOX_HD, :], ps[hh], preferred_element_type=F32)

    lo = jlo_ref[(pl.program_id(0) * pl.num_programs(1) + pl.program_id(1)) * nq + qi]
    span = qi - lo

    def body(i, carry):
        sweep(lo + 2 * i, False, 2)
        return carry

    lax.fori_loop(0, span // 2, body, 0)

    @pl.when(span % 2 == 1)
    def _():
        sweep(qi - 1, False)

    sweep(qi, True)
    ot = jnp.concatenate([acc_ref[0] / l_ref[0][0:1, :], acc_ref[1] / l_ref[1][0:1, :]], axis=0)
    gt = g_ref[...].astype(F32)
    o_ref[...] = (ot.T * jax.nn.sigmoid(gt)).astype(o_ref.dtype)


def _fox_first_block(cum_t, logit_bound, tq):
    bh, _, seq = cum_t.shape
    nq = seq // tq
    blk = cum_t.reshape(bh, 2, nq, tq)
    gap = blk[:, :, :, None, 0] - blk[:, :, None, :, tq - 1]
    dead = jnp.all(2.0 * logit_bound + gap < FOX_ZERO_EXP, axis=1)
    dead = dead & (jnp.arange(nq)[None, None, :] < jnp.arange(nq)[None, :, None])
    return jnp.sum(jnp.cumprod(dead.astype(I32), axis=-1), axis=-1).astype(I32).reshape(-1)


def _fox_attention(q, gate, ka, vt, cum, logit_bound, batch, seq, tq=512):
    n = batch * seq
    nq = seq // tq
    hp = FOX_HEADS // 2
    cum_t = cum[:, :FOX_HEADS].reshape(batch, seq, hp, 2).transpose(0, 2, 3, 1).reshape(batch * hp, 2, seq)
    jlo = _fox_first_block(cum_t, logit_bound, tq)
    grid_spec = pltpu.PrefetchScalarGridSpec(
        num_scalar_prefetch=1,
        grid=(batch, hp, nq),
        in_specs=[
            pl.BlockSpec((tq, 128), lambda b, h, i, jlo: (b * nq + i, h)),
            pl.BlockSpec((seq, 256), lambda b, h, i, jlo: (b, h)),
            pl.BlockSpec((128, seq), lambda b, h, i, jlo: (h, b)),
            pl.BlockSpec((tq, 128), lambda b, h, i, jlo: (b * nq + i, h)),
        ],
        out_specs=pl.BlockSpec((tq, 128), lambda b, h, i, jlo: (b * nq + i, h)),
        scratch_shapes=[pltpu.VMEM((2, tq, 256), BF16), pltpu.VMEM((2, 8, tq), F32),
                        pltpu.VMEM((2, 8, tq), F32), pltpu.VMEM((2, FOX_HD, tq), F32)],
    )
    return pl.pallas_call(
        functools.partial(_fox_attn_kernel, tq=tq, nq=nq),
        grid_spec=grid_spec,
        out_shape=jax.ShapeDtypeStruct((n, D_MODEL), BF16),
        compiler_params=_cparams(("parallel", "parallel", "arbitrary")),
        name="fox_attention",
    )(jlo, q, ka, vt, gate)


def _fox_mixer(h, g_mix, w_in, f_bias, q_norm, k_norm, batch, seq):
    q, gate, ka, vt, cum = _fox_proj(h, g_mix, w_in, f_bias, q_norm, k_norm, seq)
    logit_bound = 1.02 * FOX_HD ** 0.5 * jnp.max(jnp.abs(q_norm.astype(F32))) * jnp.max(jnp.abs(k_norm.astype(F32)))
    return _fox_attention(q, gate, ka, vt, cum, logit_bound, batch, seq)


def _rglru_kernel(gate_ref, u_ref, par_ref, wa_ref, wx_ref, o_ref, prev_ref, hc_ref, *, tt):
    @pl.when(pl.program_id(1) == 0)
    def _():
        prev_ref[...] = jnp.zeros_like(prev_ref)
        hc_ref[...] = jnp.zeros_like(hc_ref)

    u = u_ref[...]
    ext = jnp.concatenate([prev_ref[...], u], axis=0)
    conv = par_ref[4:5, :] + u * par_ref[3:4, :]
    for shift in range(1, CONV_W):
        conv = conv + pltpu.roll(ext, shift, 0)[8:, :] * par_ref[3 - shift:4 - shift, :]
    prev_ref[...] = u[tt - 8:, :]

    cb = conv.astype(BF16)
    ra, ia = [], []
    for nb in range(RG_BLOCKS):
        blk = cb[:, nb * RG_BW:(nb + 1) * RG_BW]
        ra.append(jnp.dot(blk, wa_ref[nb], preferred_element_type=F32))
        ia.append(jnp.dot(blk, wx_ref[nb], preferred_element_type=F32))
    r = jax.nn.sigmoid(jnp.concatenate(ra, axis=1) + par_ref[5:6, :])
    ig = jax.nn.sigmoid(jnp.concatenate(ia, axis=1) + par_ref[6:7, :])
    lam = par_ref[7:8, :]
    softplus = jnp.maximum(-lam, 0.0) + jnp.log1p(jnp.exp(-jnp.abs(lam)))
    log_a = -RG_C * r * softplus
    a = jnp.exp(log_a)
    b = jnp.sqrt(1.0 - a * a) * (ig * conv)

    a = a.reshape(tt // 8, 8, a.shape[-1])
    b = b.reshape(a.shape)
    within = lax.broadcasted_iota(I32, a.shape, 1)
    for dist in (1, 2, 4):
        ok = within >= dist
        a_sh = pltpu.roll(a, dist, 1)
        b_sh = pltpu.roll(b, dist, 1)
        b = jnp.where(ok, a * b_sh + b, b)
        a = jnp.where(ok, a * a_sh, a)
    a = a.reshape(tt, a.shape[-1])
    b = b.reshape(a.shape)
    gt = gate_ref[...]
    gelu = 0.5 * gt * (1.0 + jnp.tanh(0.7978845608028654 * (gt + 0.044715 * gt * gt * gt)))
    h = hc_ref[0:1, :]
    for g in range(tt // 8):
        sl = slice(g * 8, (g + 1) * 8)
        hg = b[sl] + a[sl] * h
        o_ref[sl, :] = (hg * gelu[sl]).astype(o_ref.dtype)
        h = hg[7:8, :]
    hc_ref[...] = jnp.broadcast_to(h, hc_ref.shape)


def _rglru_mixer(h, g_mix, w_in, conv_w, conv_b, w_a, b_a, w_x, b_x, lam, batch, seq, tt=256):
    n = batch * seq
    nt = seq // tt
    w = D_MODEL
    proj = _norm_proj(h, g_mix, w_in.astype(BF16))
    par = jnp.concatenate([conv_w.astype(F32), conv_b.reshape(1, w), b_a.reshape(1, w),
                           b_x.reshape(1, w), lam.reshape(1, w)], axis=0).astype(F32)
    return pl.pallas_call(
        functools.partial(_rglru_kernel, tt=tt),
        grid=(batch, nt),
        in_specs=[pl.BlockSpec((tt, w), lambda b, i: (b * nt + i, 0)),
                  pl.BlockSpec((tt, w), lambda b, i: (b * nt + i, 1)),
                  _full((8, w)), _full((RG_BLOCKS, RG_BW, RG_BW)), _full((RG_BLOCKS, RG_BW, RG_BW))],
        out_specs=pl.BlockSpec((tt, w), lambda b, i: (b * nt + i, 0)),
        out_shape=jax.ShapeDtypeStruct((n, w), BF16),
        scratch_shapes=[pltpu.VMEM((8, w), F32), pltpu.VMEM((8, w), F32)],
        compiler_params=_cparams(("parallel", "arbitrary")),
        name="rglru",
    )(proj, proj, par, w_a.astype(BF16), w_x.astype(BF16))


def _outproj_router_kernel(a_ref, h_ref, wo_ref, g_ref, wr_ref, br_ref, tri_ref, lst_ref,
                           hm_ref, xn_ref, gate_ref, pos_ref, cnt_ref):
    hm = h_ref[...] + jnp.dot(a_ref[...], wo_ref[...], preferred_element_type=F32)
    hm_ref[...] = hm
    xn = _rms(hm, g_ref[...])
    x_hi, x_lo = _split2(xn)
    xn_ref[...] = x_hi
    w_hi, w_lo = _split2(wr_ref[...])
    logit = _bdot_nt(w_hi, x_hi) + (_bdot_nt(w_hi, x_lo) + _bdot_nt(w_lo, x_hi)) + br_ref[:, 0:1]
    ne, tm = logit.shape
    eidx = lax.broadcasted_iota(I32, (ne, tm), 0)
    work = logit
    vals, hots = [], []
    for _ in range(TOP_K):
        mx = jnp.max(work, axis=0, keepdims=True)
        pick = jnp.min(jnp.where(work == mx, eidx, ne), axis=0, keepdims=True)
        hot = eidx == pick
        work = jnp.where(hot, -jnp.inf, work)
        vals.append(mx)
        hots.append(hot)
    ex = [jnp.exp(v - vals[0]) for v in vals]
    den = ex[0] + ex[1] + ex[2] + ex[3]
    gate_ref[...] = jnp.concatenate([e / den for e in ex], axis=0)
    tok_hot = (hots[0] | hots[1] | hots[2] | hots[3]).astype(BF16)
    tri = tri_ref[...]
    count = jnp.zeros((ne, 1), F32)
    pieces = []
    for j in range(tm // 128):
        hb = tok_hot[:, j * 128:(j + 1) * 128]
        pieces.append(jnp.dot(hb, tri, preferred_element_type=F32) + count)
        count = count + jnp.sum(hb.astype(F32), axis=1, keepdims=True)
    before = jnp.concatenate(pieces, axis=1)
    units = jnp.floor((count + (STRIP_ALIGN - 1)) * (1.0 / STRIP_ALIGN))
    units_b = jnp.broadcast_to(units, (ne, 128)).astype(BF16)
    start = STRIP_ALIGN * jnp.dot(lst_ref[...], units_b, preferred_element_type=F32)[:, 0:1]
    where = before + start
    pos_ref[...] = jnp.concatenate(
        [jnp.sum(jnp.where(h, where, 0.0), axis=0, keepdims=True) for h in hots], axis=0).astype(I32)
    cnt_ref[...] = jnp.broadcast_to(STRIP_ALIGN * units, cnt_ref.shape)


def _outproj_router(act, h, w_out, g_ffn, w_r, b_r):
    n, d = h.shape
    tm = MOE_TILE
    tri = jnp.asarray(np.triu(np.ones((128, 128), np.float32), 1), BF16)
    lst =jnp.asarray(np.tril(np.ones((N_EXPERTS, N_EXPERTS), np.float32), -1), BF16)
    row = lambda i: (i, 0)
    colb = lambda i: (0, i)
    return pl.pallas_call(
        _outproj_router_kernel,
        grid=(n // tm,),
        in_specs=[pl.BlockSpec((tm, d), row), pl.BlockSpec((tm, d), row), _full((d, d)), _full((1, d)),
                  _full((N_EXPERTS, d)), _full((N_EXPERTS, 128)), _full((128, 128)),
                  _full((N_EXPERTS, N_EXPERTS))],
        out_specs=[pl.BlockSpec((tm, d), row), pl.BlockSpec((tm, d), row),
                   pl.BlockSpec((TOP_K, tm), colb), pl.BlockSpec((TOP_K, tm), colb),
                   pl.BlockSpec((N_EXPERTS, 128), row)],
        out_shape=[jax.ShapeDtypeStruct((n, d), F32), jax.ShapeDtypeStruct((n, d), BF16),
                   jax.ShapeDtypeStruct((TOP_K, n), F32), jax.ShapeDtypeStruct((TOP_K, n), I32),
                   jax.ShapeDtypeStruct((n // tm * N_EXPERTS, 128), F32)],
        compiler_params=_cparams(("parallel",)),
        name="outproj_router",
    )(act, h, w_out.astype(BF16), g_ffn.reshape(1, d), w_r.T.astype(F32),
      jnp.broadcast_to(b_r.astype(F32)[:, None], (N_EXPERTS, 128)), tri, lst)


def _for_strip_chunks(nrows, max_rows, fn):
    size = max_rows
    while size >= STRIP_ALIGN:
        @pl.when((nrows & size) != 0)
        def _(size=size):
            fn(pl.multiple_of(nrows & ~(2 * size - 1), STRIP_ALIGN), size)
        size //= 2


def _for_tile_strips(n8_ref, sbase_ref, gdst_ref, tile, fn):
    def per_expert(e, carry):
        i = tile * N_EXPERTS + e
        so = sbase_ref[i]
        gd = gdst_ref[i]
        _for_strip_chunks(n8_ref[i], MOE_TILE,
                          lambda o, size: fn(pl.multiple_of(so + o, STRIP_ALIGN), pl.multiple_of(gd + o, STRIP_ALIGN), size))
        return carry

    lax.fori_loop(0, N_EXPERTS, per_expert, 0)


def _wait_tile_strips(n8_ref, sbase_ref, tile, wait_rows):
    last = tile * N_EXPERTS + (N_EXPERTS - 1)
    total = sbase_ref[last] + n8_ref[last]
    size = STAGE_ROWS.bit_length() - 1
    size = 1 << size
    while size >= STRIP_ALIGN:
        @pl.when((total & size) != 0)
        def _(size=size):
            wait_rows(size)
        size //= 2


def _dispatch_kernel(n8_ref, sbase_ref, gdst_ref, tot_ref, pst_ref, x_ref, pos_ref, xs_hbm,
                     stage, zblk, rsem, zsem):
    tile = pl.program_id(0)
    slot = tile % 2
    xb = x_ref[...]
    for c in range(STAGE_ROWS // ONEHOT_CHUNK):
        rows = lax.broadcasted_iota(I32, (ONEHOT_CHUNK, MOE_TILE), 0) + c * ONEHOT_CHUNK
        onehot = jnp.zeros((ONEHOT_CHUNK, MOE_TILE), F32)
        for k in range(TOP_K):
            onehot = jnp.where(rows == pos_ref[k:k + 1, :], 1.0, onehot)
        stage[slot, c * ONEHOT_CHUNK:(c + 1) * ONEHOT_CHUNK, :] = jnp.dot(
            onehot.astype(BF16), xb, preferred_element_type=F32)

    def copy(sl, so, gd, size):
        return pltpu.make_async_copy(stage.at[sl, pl.ds(so, size), :], xs_hbm.at[pl.ds(gd, size), :], rsem.at[sl])

    _for_tile_strips(n8_ref, sbase_ref, gdst_ref, tile, lambda so, gd, size: copy(slot, so, gd, size).start())

    @pl.when(tile > 0)
    def _():
        _wait_tile_strips(n8_ref, sbase_ref, tile - 1, lambda size: copy(1 - slot, 0, 0, size).wait())

    @pl.when(tile == 0)
    def _():
        zblk[...] = jnp.zeros_like(zblk)

        def zero(first, size):
            cp = pltpu.make_async_copy(zblk.at[pl.ds(0, size), :], xs_hbm.at[pl.ds(first, size), :], zsem)
            cp.start()
            cp.wait()

        def per_expert(e, carry):
            tot = tot_ref[e]
            first = pst_ref[e] + tot
            npad = ((tot + (MOE_BLK - 1)) // MOE_BLK) * MOE_BLK - tot
            _for_strip_chunks(npad, MOE_BLK // 2,
                              lambda o, size: zero(pl.multiple_of(first + o, STRIP_ALIGN), size))
            return carry

        lax.fori_loop(0, N_EXPERTS, per_expert, 0)

        last = N_EXPERTS - 1
        used = (pst_ref[last] + tot_ref[last] + (MOE_BLK - 1)) // MOE_BLK

        def ztail(b, c):
            zero(pl.multiple_of(b * MOE_BLK, MOE_BLK), MOE_BLK)
            return c

        lax.fori_loop(used, xs_hbm.shape[0] // MOE_BLK, ztail, 0)

    @pl.when(tile == pl.num_programs(0) - 1)
    def _():
        _wait_tile_strips(n8_ref, sbase_ref, tile, lambda size: copy(slot, 0, 0, size).wait())


def _dispatch(xn, pos, n8, sbase, gdst, total, pstart, cap):
    n, d = xn.shape
    tm = MOE_TILE
    grid_spec = pltpu.PrefetchScalarGridSpec(
        num_scalar_prefetch=5,
        grid=(n // tm,),
        in_specs=[pl.BlockSpec((tm, d), lambda i, *_: (i, 0)), pl.BlockSpec((TOP_K, tm), lambda i, *_: (0, i))],
        out_specs=pl.BlockSpec(memory_space=pl.ANY),
        scratch_shapes=[pltpu.VMEM((2, STAGE_ROWS, d), F32), pltpu.VMEM((MOE_BLK, d), F32),
                        pltpu.SemaphoreType.DMA((2,)), pltpu.SemaphoreType.DMA],
    )
    return pl.pallas_call(
        _dispatch_kernel,
        grid_spec=grid_spec,
        out_shape=jax.ShapeDtypeStruct((cap, d), F32),
        compiler_params=_cparams(("arbitrary",)),
        name="moe_dispatch",
    )(n8, sbase, gdst, total, pstart, xn, pos)


def _expert_kernel(be_ref, nu_ref, x_ref, wgu_ref, bgu_ref, wdn_ref, bdn_ref, y_ref, wgu_bf, wdn_bf):
    d = D_MODEL
    b = pl.program_id(0)

    @pl.when((b == 0) | (be_ref[b] != be_ref[jnp.maximum(b - 1, 0)]))
    def _():
        wgu_bf[...] = wgu_ref[...].astype(BF16)
        wdn_bf[...] = wdn_ref[...].astype(BF16)

    @pl.when(b < nu_ref[0])
    def _():
        xb = x_ref[...].astype(BF16)
        acc = None
        for c in range(d // EXPERT_CHUNK):
            lo, hi = c * EXPERT_CHUNK, (c + 1) * EXPERT_CHUNK
            glu = jnp.dot(xb, wgu_bf[:, lo:hi], preferred_element_type=F32) + bgu_ref[:, lo:hi]
            lin = jnp.dot(xb, wgu_bf[:, d + lo:d + hi], preferred_element_type=F32) + bgu_ref[:, d + lo:d + hi]
            glu = jnp.minimum(glu, SWIGLU_LIMIT)
            lin = jnp.clip(lin, -SWIGLU_LIMIT, SWIGLU_LIMIT)
            act = glu * jax.nn.sigmoid(SWIGLU_ALPHA * glu) * (lin + 1.0)
            part = jnp.dot(act.astype(BF16), wdn_bf[lo:hi, :], preferred_element_type=F32)
            acc = part if acc is None else acc + part
        y_ref[...] = acc + bdn_ref[...]

    @pl.when(b >= nu_ref[0])
    def _():
        y_ref[...] = jnp.zeros_like(y_ref)


def _experts(xs, block_expert, n_used, w_gu, b_gu, w_dn, b_dn):
    cap, d = xs.shape
    nb = cap // MOE_BLK
    ne = w_gu.shape[0]

    def xmap(b, be, nu):
        return (jnp.minimum(b, nu[0] - 1), 0)

    def wmap(b, be, nu):
        return (be[b], 0, 0)

    grid_spec = pltpu.PrefetchScalarGridSpec(
        num_scalar_prefetch=2,
        grid=(nb,),
        in_specs=[pl.BlockSpec((MOE_BLK, d), xmap),
                  pl.BlockSpec((None, d, 2 * d), wmap), pl.BlockSpec((None, 1, 2 * d), wmap),
                  pl.BlockSpec((None, d, d), wmap), pl.BlockSpec((None, 1, d), wmap)],
        out_specs=pl.BlockSpec((MOE_BLK, d), lambda b, be, nu: (b, 0)),
        scratch_shapes=[pltpu.VMEM((d, 2 * d), BF16), pltpu.VMEM((d, d), BF16)],
    )
    return pl.pallas_call(
        _expert_kernel,
        grid_spec=grid_spec,
        out_shape=jax.ShapeDtypeStruct((cap, d), F32),
        compiler_params=_cparams(("arbitrary",)),
        name="moe_experts",
    )(block_expert, n_used, xs, w_gu, b_gu.reshape(ne, 1, 2 * d), w_dn, b_dn.reshape(ne, 1, d))


def _combine_ple_kernel(n8_ref, sbase_ref, gdst_ref, y_hbm, hm_ref, gate_ref, pos_ref, p_ref, pw_ref, pn_ref,
                        gn_ref, gw_ref, o_ref, stage, rsem):
    tile = pl.program_id(0)
    slot = tile % 2

    def copy(sl, so, gd, size):
        return pltpu.make_async_copy(y_hbm.at[pl.ds(gd, size), :], stage.at[sl, pl.ds(so, size), :], rsem.at[sl])

    def fetch(t, sl):
        _for_tile_strips(n8_ref, sbase_ref, gdst_ref, t, lambda so, gd, size: copy(sl, so, gd, size).start())

    @pl.when(tile == 0)
    def _():
        stage[...] = jnp.zeros_like(stage)
        fetch(0, 0)

    @pl.when(tile + 1 < pl.num_programs(0))
    def _():
        fetch(tile + 1, 1 - slot)

    ple = _rms(jnp.dot(p_ref[...].astype(BF16), pw_ref[...], preferred_element_type=F32), pn_ref[...])
    _wait_tile_strips(n8_ref, sbase_ref, tile, lambda size: copy(slot, 0, 0, size).wait())
    h2 = hm_ref[...]
    for c in range(STAGE_ROWS // COMBINE_CHUNK):
        lanes = lax.broadcasted_iota(I32, (MOE_TILE, COMBINE_CHUNK), 1) + c * COMBINE_CHUNK
        weights = jnp.zeros((MOE_TILE, COMBINE_CHUNK), F32)
        for k in range(TOP_K):
            weights = jnp.where(lanes == pos_ref[:, k:k + 1], gate_ref[:, k:k + 1], weights)
        staged = stage[slot, c * COMBINE_CHUNK:(c + 1) * COMBINE_CHUNK, :].astype(BF16)
        h2 = h2 + jnp.dot(weights.astype(BF16), staged, preferred_element_type=F32)
    gate = jax.nn.sigmoid(jnp.dot(_rms(h2, gn_ref[...]).astype(BF16), gw_ref[...], preferred_element_type=F32))
    o_ref[...] = h2 + ple * gate


def _combine_ple(y, pos, n8, sbase, gdst, h_mid, gates, p, ple_w, ple_norm, gate_norm, gate_w):
    n, d = h_mid.shape
    tm = MOE_TILE
    row = lambda i, *_: (i, 0)
    full = lambda shape: pl.BlockSpec(shape, lambda i, *_: (0,) * len(shape))
    grid_spec = pltpu.PrefetchScalarGridSpec(
        num_scalar_prefetch=3,
        grid=(n // tm,),
        in_specs=[pl.BlockSpec(memory_space=pl.ANY), pl.BlockSpec((tm, d), row), pl.BlockSpec((tm, TOP_K), row),
                  pl.BlockSpec((tm, TOP_K), row), pl.BlockSpec((tm, PLE_DIM), row), full((PLE_DIM, d)),
                  full((1, d)), full((1, d)), full((d, d))],
        out_specs=pl.BlockSpec((tm, d), row),
        scratch_shapes=[pltpu.VMEM((2, STAGE_ROWS, d), F32), pltpu.SemaphoreType.DMA((2,))],
    )
    return pl.pallas_call(
        _combine_ple_kernel,
        grid_spec=grid_spec,
        out_shape=jax.ShapeDtypeStruct((n, d), F32),
        compiler_params=_cparams(("arbitrary",)),
        name="moe_combine_ple",
    )(n8, sbase, gdst, y, h_mid, gates.T, pos.T, p, ple_w.astype(BF16), ple_norm.reshape(1, d),
      gate_norm.reshape(1, d), gate_w.astype(BF16))


def _moe_ple(act, h, w_out, g_ffn, w_r, b_r, layer, w_gu, b_gu, w_dn, b_dn, p, ple_w, ple_norm, gate_norm, gate_w):
    n, d = h.shape
    ntiles = n // MOE_TILE
    h_mid, xn, gates, pos, cnt = _outproj_router(act, h, w_out, g_ffn, w_r, b_r)
    n8 = cnt[:, 0].astype(I32).reshape(ntiles, N_EXPERTS)
    sbase = jnp.cumsum(n8, axis=1) - n8
    total = jnp.sum(n8, axis=0)
    padded = (total + MOE_BLK - 1) // MOE_BLK * MOE_BLK
    pend = jnp.cumsum(padded)
    pstart = pend - padded
    gdst = pstart[None, :] + jnp.cumsum(n8, axis=0) - n8
    nb = -(-(n * TOP_K + ntiles * N_EXPERTS * (STRIP_ALIGN - 1)) // MOE_BLK) + N_EXPERTS
    cap = nb * MOE_BLK
    block_start = jnp.arange(nb, dtype=I32) * MOE_BLK
    block_expert = jnp.minimum(jnp.sum(pend[None, :] <= block_start[:, None], axis=1), N_EXPERTS - 1).astype(I32)
    n_used = (pend[-1:] // MOE_BLK).astype(I32)
    n8f, sbf, gdf = n8.reshape(-1), sbase.reshape(-1).astype(I32), gdst.reshape(-1).astype(I32)
    xs = _dispatch(xn, pos, n8f, sbf, gdf, total.astype(I32), pstart.astype(I32), cap)
    y = _experts(xs, block_expert + layer * N_EXPERTS, n_used, w_gu, b_gu, w_dn, b_dn)
    return _combine_ple(y, pos, n8f, sbf, gdf, h_mid, gates, p, ple_w, ple_norm, gate_norm, gate_w)


def kernel(x, p, norm_mix, norm_ffn, hg_w_in, hg_w_out, hg_gnorm, hg_lb_param, fox_w_in, fox_f_bias, fox_qnorm, fox_knorm, fox_w_out, rg_w_in, rg_conv_w, rg_conv_b, rg_wa, rg_ba, rg_wx, rg_bx, rg_lambda, rg_w_out, router_w, router_b, moe_w_gu, moe_b_gu, moe_w_dn, moe_b_dn, ple_w, ple_norm, ple_gate_norm, ple_gate_w):
    batch, seq, d = x.shape
    depth = p.shape[0]
    n = batch * seq
    lb_all = jnp.cumsum(jax.nn.softmax(hg_lb_param.astype(F32), axis=0), axis=0)
    lb_all = lb_all - lb_all[0]
    h = x.reshape(n, d).astype(F32)
    ne = depth * N_EXPERTS
    w_gu = moe_w_gu.astype(F32).reshape(ne, d, 2 * d)
    b_gu = moe_b_gu.astype(F32).reshape(ne, 2 * d)
    w_dn = moe_w_dn.astype(F32).reshape(ne, d, d)
    b_dn = moe_b_dn.astype(F32).reshape(ne, d)
    for i in range(depth):
        j = i // 3
        kind = i % 3
        if kind == 0:
            act = _hgrn2_mixer(h, norm_mix[i], hg_w_in[j], None, hg_gnorm[j], lb_all[i], batch, seq)
            w_out = hg_w_out[j]
        elif kind == 1:
            act = _fox_mixer(h, norm_mix[i], fox_w_in[j], fox_f_bias[j], fox_qnorm[j], fox_knorm[j], batch, seq)
            w_out = fox_w_out[j]
        else:
            act = _rglru_mixer(h, norm_mix[i], rg_w_in[j], rg_conv_w[j], rg_conv_b[j], rg_wa[j], rg_ba[j],
                               rg_wx[j], rg_bx[j], rg_lambda[j], batch, seq)
            w_out = rg_w_out[j]
        h = _moe_ple(act, h, w_out, norm_ffn[i], router_w[i], router_b[i], i, w_gu, b_gu, w_dn, b_dn,
                     p[i].reshape(n, PLE_DIM), ple_w[i], ple_norm[i], ple_gate_norm[i], ple_gate_w[i])
    return h.reshape(batch, seq, d)
```

```python
import functools

import jax
import jax.numpy as jnp
import numpy as np
from jax import lax
from jax.experimental import pallas as pl
from jax.experimental.pallas import tpu as pltpu

F32 = jnp.float32
BF16 = jnp.bfloat16
I32 = jnp.int32

D_MODEL = 1024
EPS = 1e-6
PLE_DIM = 256

HG_HEADS = 8
HG_DK = 128
HG_CHUNK = 128
HG_LEVELS = 7

FOX_HEADS = 16
FOX_HD = 64
FOX_ZERO_EXP = -105.0

RG_BLOCKS = 4
RG_BW = 256
CONV_W = 4
RG_C = 8.0

N_EXPERTS = 32
TOP_K = 4
SWIGLU_LIMIT = 7.0
SWIGLU_ALPHA = 1.702
MOE_BLK = 512
EXPERT_CHUNK = 512
MOE_TILE = 512
STRIP_ALIGN = 8
STAGE_ROWS = -(-(TOP_K * MOE_TILE + N_EXPERTS * (STRIP_ALIGN - 1)) // 256) * 256
ONEHOT_CHUNK = 256
COMBINE_CHUNK = STAGE_ROWS // 3

VMEM_LIMIT = 56 * 1024 * 1024


def _cparams(sem):
    return pltpu.CompilerParams(dimension_semantics=sem, vmem_limit_bytes=VMEM_LIMIT)


def _bdot(a, b):
    return jnp.dot(a.astype(BF16), b.astype(BF16), preferred_element_type=F32)


def _bdot_nt(a, b):
    return lax.dot_general(a.astype(BF16), b.astype(BF16), (((1,), (1,)), ((), ())),
                           preferred_element_type=F32)


def _bdot_tn(a, b):
    return lax.dot_general(a.astype(BF16), b.astype(BF16), (((0,), (0,)), ((), ())),
                           preferred_element_type=F32)


def _rms(x, g):
    return x * lax.rsqrt(jnp.mean(x * x, axis=-1, keepdims=True) + EPS) * g


def _split2(x):
    hi = x.astype(BF16)
    lo = (x - hi.astype(F32)).astype(BF16)
    return hi, lo


def _split3(x):
    hi = x.astype(BF16)
    r = x - hi.astype(F32)
    mid = r.astype(BF16)
    lo = (r - mid.astype(F32)).astype(BF16)
    return hi, mid, lo


def _log_sigmoid(z):
    return jnp.minimum(z, 0.0) - jnp.log1p(jnp.exp(-jnp.abs(z)))


def _full(shape):
    return pl.BlockSpec(shape, lambda *_: (0,) * len(shape))


def _norm_proj_kernel(h_ref, g_ref, w_ref, o_ref, *, cn):
    xn = _rms(h_ref[...], g_ref[...]).astype(BF16)
    m = w_ref.shape[1]
    for c in range(m // cn):
        o_ref[:, c * cn:(c + 1) * cn] = jnp.dot(
            xn, w_ref[:, c * cn:(c + 1) * cn], preferred_element_type=F32).astype(o_ref.dtype)


def _norm_proj(h, g, w, tm=512, out_dtype=F32):
    n, d = h.shape
    m = w.shape[1]
    return pl.pallas_call(
        functools.partial(_norm_proj_kernel, cn=512),
        grid=(n // tm,),
        in_specs=[pl.BlockSpec((tm, d), lambda i: (i, 0)), _full((1, d)), _full((d, m))],
        out_specs=pl.BlockSpec((tm, m), lambda i: (i, 0)),
        out_shape=jax.ShapeDtypeStruct((n, m), out_dtype),
        compiler_params=_cparams(("parallel",)),
        name="norm_proj",
    )(h, g.reshape(1, d), w)


def _hgrn2_consts():
    c = HG_CHUNK
    t = np.arange(c)
    tril = (t[:, None] >= t[None, :]).astype(np.float32)
    sel = np.zeros((HG_LEVELS, c, c), np.float32)
    for l in range(HG_LEVELS):
        hs = 1 << l
        m = (t // (2 * hs)) * (2 * hs) + hs - 1
        sel[l, t, m] = 1.0
    return jnp.asarray(tril, BF16), jnp.asarray(sel.reshape(HG_LEVELS * c, c), BF16)


def _hgrn2_kernel(q_ref, z_ref, v_ref, g_ref, par_ref, tril_ref, sel_ref, o_ref, st_ref, *, nchunk):
    c = HG_CHUNK

    @pl.when(pl.program_id(2) == 0)
    def _():
        st_ref[...] = jnp.zeros_like(st_ref)

    log_lb = par_ref[0:1, :]
    log1m_lb = par_ref[1:2, :]
    one_m_lb = par_ref[2:3, :]
    gnorm = par_ref[3:4, :]
    row = lax.broadcasted_iota(I32, (c, c), 0)
    col = lax.broadcasted_iota(I32, (c, c), 1)

    cs = range(nchunk)
    rs = [slice(ci * c, (ci + 1) * c) for ci in cs]
    tril = tril_ref[...]
    q, k, vb, cum, refs, a = [], [], [], [], [], []
    for r in rs:
        qr = q_ref[r, :]
        z = z_ref[r, :]
        vb.append(v_ref[r, :].astype(BF16))
        q.append(qr * jax.nn.sigmoid(qr) * (HG_DK ** -0.5))
        b = log1m_lb + _log_sigmoid(z)
        lf = jnp.maximum(log_lb, b) + jnp.log1p(jnp.exp(-jnp.abs(log_lb - b)))
        k.append(one_m_lb * jax.nn.sigmoid(-z))
        hi, mid, lo = _split3(lf)
        cum.append(jnp.dot(tril, hi, preferred_element_type=F32)
                   + jnp.dot(tril, mid, preferred_element_type=F32)
                   + jnp.dot(tril, lo, preferred_element_type=F32))
    for i in cs:
        chi, clo = _split2(cum[i])
        refs.append(jnp.dot(sel_ref[...], chi, preferred_element_type=F32)
                    + jnp.dot(sel_ref[...], clo, preferred_element_type=F32))
        a.append(jnp.where(row == col, _bdot_nt(q[i], k[i]), 0.0))
    for l in range(HG_LEVELS):
        hs = 1 << l
        mask = (((row ^ col) >> l) == 1) & (row > col)
        for i in cs:
            ref = refs[i][l * c:(l + 1) * c, :]
            if hs >= 8:
                parts = []
                for blk in range(c // hs):
                    sl = slice(blk * hs, (blk + 1) * hs)
                    if blk % 2:
                        parts.append(q[i][sl] * jnp.exp(jnp.minimum(cum[i][sl] - ref[sl], 0.0)))
                    else:
                        parts.append(k[i][sl] * jnp.exp(jnp.minimum(ref[sl] - cum[i][sl], 0.0)))
                x = jnp.concatenate(parts, axis=0).astype(BF16)
            else:
                x = (jnp.where((row & hs) != 0, q[i], k[i]) * jnp.exp(-jnp.abs(cum[i] - ref))).astype(BF16)
            a[i] = jnp.where(mask, _bdot_nt(x, x), a[i])
    intra = [_bdot(a[i], vb[i]) for i in cs]
    last = [cum[i][c - 1:c, :] for i in cs]
    qe = [(q[i] * jnp.exp(cum[i])).astype(BF16) for i in cs]
    kd = [(k[i] * jnp.exp(last[i] - cum[i])).astype(BF16) for i in cs]
    st = st_ref[...]
    for i in cs:
        o = intra[i] + _bdot_nt(qe[i], st)
        st = st * jnp.exp(last[i]) + _bdot_tn(vb[i], kd[i])
        gt = g_ref[rs[i], :]
        y = _rms(o, gnorm) * (gt * jax.nn.sigmoid(gt))
        o_ref[rs[i], :] = y.astype(o_ref.dtype)
    st_ref[...] = st


def _hgrn2_recurrence(proj, par, batch, seq, tt=2048):
    n = batch * seq
    nt = seq // tt
    tril, sel = _hgrn2_consts()

    def part(p):
        return pl.BlockSpec((tt, HG_DK), lambda b, h, i, p=p: (b * nt + i, p * HG_HEADS + h))

    return pl.pallas_call(
        functools.partial(_hgrn2_kernel, nchunk=tt // HG_CHUNK),
        grid=(batch, HG_HEADS, nt),
        in_specs=[part(0), part(1), part(2), part(3),
                  pl.BlockSpec((8, HG_DK), lambda b, h, i: (0, h)),
                  _full(tril.shape), _full(sel.shape)],
        out_specs=pl.BlockSpec((tt, HG_DK), lambda b, h, i: (b * nt + i, h)),
        out_shape=jax.ShapeDtypeStruct((n, D_MODEL), BF16),
        scratch_shapes=[pltpu.VMEM((HG_DK, HG_DK), F32)],
        compiler_params=_cparams(("parallel", "parallel", "arbitrary")),
        name="hgrn2_recurrence",
    )(proj, proj, proj, proj, par, tril, sel)


def _hgrn2_mixer(h, g_mix, w_in, w_out_unused, g_norm, lb, batch, seq):
    del w_out_unused
    proj = _norm_proj(h, g_mix, w_in.astype(BF16))
    par = jnp.zeros((8, D_MODEL), F32)
    par = par.at[0].set(jnp.log(lb)).at[1].set(jnp.log1p(-lb)).at[2].set(1.0 - lb)
    par = par.at[3].set(jnp.tile(g_norm.astype(F32), HG_HEADS))
    return _hgrn2_recurrence(proj, par, batch, seq)


def _fox_proj_kernel(h_ref, g_ref, w_ref, wvt_ref, wf_ref, fb_ref, qg_ref, kg_ref, gs_ref, gst_ref, tril_ref,
                     place_ref, q_ref, gate_ref, ka_ref, vt_ref, cum_ref, carry_ref, *, tiles_per_seq):
    d = D_MODEL

    @pl.when(pl.program_id(0) % tiles_per_seq == 0)
    def _():
        carry_ref[...] = jnp.zeros_like(carry_ref)

    xn = _rms(h_ref[...], g_ref[...])
    xb = xn.astype(BF16)

    def headnorm(t, gain):
        shi, slo = _split2(t * t)
        ssq = (jnp.dot(shi, gs_ref[...], preferred_element_type=F32)
               + jnp.dot(slo, gs_ref[...], preferred_element_type=F32))
        inv = lax.rsqrt(ssq * (1.0 / FOX_HD) + EPS)
        ihi, ilo = _split2(inv)
        invf = (jnp.dot(ihi, gst_ref[...], preferred_element_type=F32)
                + jnp.dot(ilo, gst_ref[...], preferred_element_type=F32))
        return t * invf * gain

    q = jnp.dot(xb, w_ref[:, 0:d], preferred_element_type=F32)
    q_ref[...] = (headnorm(q, qg_ref[...]) * (FOX_HD ** -0.5)).astype(q_ref.dtype)
    k = jnp.dot(xb, w_ref[:, d:2 * d], preferred_element_type=F32)
    kn = headnorm(k, kg_ref[...]).astype(BF16)
    gate_ref[...] = jnp.dot(xb, w_ref[:, 3 * d:4 * d], preferred_element_type=F32).astype(gate_ref.dtype)
    vt_ref[...] = lax.dot_general(wvt_ref[...], xb, (((1,), (1,)), ((), ())),
                                  preferred_element_type=F32).astype(vt_ref.dtype)

    x_lo = (xn - xb.astype(F32)).astype(BF16)
    wf_hi, wf_lo = _split2(wf_ref[...])
    fl = _bdot(xb, wf_hi) + (_bdot(x_lo, wf_hi) + _bdot(xb, wf_lo)) + fb_ref[...]
    hi, mid, lo = _split3(_log_sigmoid(fl))
    tril = tril_ref[...]
    cum = (jnp.dot(tril, hi, preferred_element_type=F32)
           + jnp.dot(tril, mid, preferred_element_type=F32)
           + jnp.dot(tril, lo, preferred_element_type=F32)) + carry_ref[0:1, :]
    cum_ref[...] = cum
    tm = cum.shape[0]
    carry_ref[...] = jnp.broadcast_to(cum[tm - 1:tm, :], carry_ref.shape)
    nhi, nmid, nlo = _split3(-cum)
    feat = (jnp.dot(nhi, place_ref[0], preferred_element_type=F32)
            + jnp.dot(nmid, place_ref[1], preferred_element_type=F32)
            + jnp.dot(nlo, place_ref[2], preferred_element_type=F32)).astype(BF16)
    for hp in range(FOX_HEADS // 2):
        ka_ref[:, hp * 256:hp * 256 + 128] = kn[:, hp * 128:(hp + 1) * 128]
        ka_ref[:, hp * 256 + 128:(hp + 1) * 256] = feat[:, hp * 128:(hp + 1) * 128]


def _fox_proj(h, g_mix, w_in, f_bias, q_norm, k_norm, seq, tm=256):
    n, d = h.shape
    w_main = w_in[:, :4 * d].astype(BF16)
    w_vt = w_in[:, 2 * d:3 * d].T.astype(BF16)
    w_f = jnp.zeros((d, 128), F32).at[:, :FOX_HEADS].set(w_in[:, 4 * d:].astype(F32))
    fb = jnp.zeros((1, 128), F32).at[0, :FOX_HEADS].set(f_bias.astype(F32))
    head_of = np.arange(d) // FOX_HD
    gs_np = (head_of[:, None] == np.arange(128)[None, :]).astype(np.float32)
    gs = jnp.asarray(gs_np, BF16)
    gst = jnp.asarray(gs_np.T, BF16)
    tril = jnp.asarray(np.tril(np.ones((tm, tm), np.float32)), BF16)
    place_np = np.zeros((3, 128, d), np.float32)
    for hd in range(FOX_HEADS):
        for c in range(3):
            place_np[c, hd, (hd // 2) * 128 + 3 * (hd % 2) + c] = 1.0
    place = jnp.asarray(place_np, BF16)
    qg = jnp.tile(q_norm.astype(F32), FOX_HEADS).reshape(1, d)
    kg = jnp.tile(k_norm.astype(F32), FOX_HEADS).reshape(1, d)
    row = lambda i: (i, 0)
    return pl.pallas_call(
        functools.partial(_fox_proj_kernel, tiles_per_seq=seq // tm),
        grid=(n // tm,),
        in_specs=[pl.BlockSpec((tm, d), row), _full((1, d)), _full((d, 4 * d)), _full((d, d)),
                  _full((d, 128)), _full((1, 128)), _full((1, d)), _full((1, d)),
                  _full((d, 128)), _full((128, d)), _full((tm, tm)), _full((3, 128, d))],
        out_specs=[pl.BlockSpec((tm, d), row), pl.BlockSpec((tm, d), row), pl.BlockSpec((tm, 2 * d), row),
                   pl.BlockSpec((d, tm), lambda i: (0, i)), pl.BlockSpec((tm, 128), row)],
        out_shape=[jax.ShapeDtypeStruct((n, d), BF16), jax.ShapeDtypeStruct((n, d), BF16),
                   jax.ShapeDtypeStruct((n, 2 * d), BF16), jax.ShapeDtypeStruct((d, n), BF16),
                   jax.ShapeDtypeStruct((n, 128), F32)],
        scratch_shapes=[pltpu.VMEM((8, 128), F32)],
        compiler_params=_cparams(("arbitrary",)),
        name="fox_proj",
    )(h, g_mix.reshape(1, d), w_main, w_vt, w_f, fb, qg, kg, gs, gst, tril, place)


def _fox_attn_kernel(jlo_ref, q_ref, k_ref, vt_ref, g_ref, o_ref, qh_ref, m_ref, l_ref, acc_ref, *, tq, nq):
    qi = pl.program_id(2)
    lane = lax.broadcasted_iota(I32, (tq, 128), 1)
    qv = q_ref[...]
    for hh in range(2):
        own = (lane < FOX_HD) if hh == 0 else (lane >= FOX_HD)
        qh_ref[hh, :, 0:128] = jnp.where(own, qv, jnp.zeros_like(qv))
        pick = jnp.where(lane < 3 * hh, 0.0, jnp.where(lane < 3 * hh + 3, 1.0, 0.0))
        qh_ref[hh, :, 128:256] = pick.astype(BF16)
    m_ref[...] = jnp.full_like(m_ref, -jnp.inf)
    l_ref[...] = jnp.zeros_like(l_ref)
    acc_ref[...] = jnp.zeros_like(acc_ref)

    def sweep(kj, masked, nblk=1):
        rows = pl.ds(pl.multiple_of(kj * tq, tq), nblk * tq)
        kb = k_ref[rows, :]
        vt = vt_ref[:, rows]
        sts = [lax.dot_general(kb, qh_ref[hh], (((1,), (1,)), ((), ())), preferred_element_type=F32)
               for hh in range(2)]
        ps, alphas = [], []
        for hh in range(2):
            st = sts[hh]
            if masked:
                key = lax.broadcasted_iota(I32, st.shape, 0)
                qry = lax.broadcasted_iota(I32, st.shape, 1)
                st = jnp.where(key <= qry, st, -jnp.inf)
            m_old = m_ref[hh]
            m_new = jnp.maximum(m_old, jnp.max(st, axis=0, keepdims=True))
            alpha = jnp.exp(m_old - m_new)
            p = jnp.exp(st - m_new[0:1, :])
            l_ref[hh] = alpha * l_ref[hh] + jnp.sum(p, axis=0, keepdims=True)
            m_ref[hh] = m_new
            ps.append(p.astype(BF16))
            alphas.append(alpha[0:1, :])
        for hh in range(2):
            acc_ref[hh] = alphas[hh] * acc_ref[hh] + jnp.dot(
                vt[hh * FOX_HD:(hh + 1) * FOX_HD, :], ps[hh], preferred_element_type=F32)

    lo = jlo_ref[(pl.program_id(0) * pl.num_programs(1) + pl.program_id(1)) * nq + qi]
    span = qi - lo

    def body(i, carry):
        sweep(lo + 2 * i, False, 2)
        return carry

    lax.fori_loop(0, span // 2, body, 0)

    @pl.when(span % 2 == 1)
    def _():
        sweep(qi - 1, False)

    sweep(qi, True)
    ot = jnp.concatenate([acc_ref[0] / l_ref[0][0:1, :], acc_ref[1] / l_ref[1][0:1, :]], axis=0)
    gt = g_ref[...].astype(F32)
    o_ref[...] = (ot.T * jax.nn.sigmoid(gt)).astype(o_ref.dtype)


def _fox_first_block(cum_t, logit_bound, tq):
    bh, _, seq = cum_t.shape
    nq = seq // tq
    blk = cum_t.reshape(bh, 2, nq, tq)
    gap = blk[:, :, :, None, 0] - blk[:, :, None, :, tq - 1]
    dead = jnp.all(2.0 * logit_bound + gap < FOX_ZERO_EXP, axis=1)
    dead = dead & (jnp.arange(nq)[None, None, :] < jnp.arange(nq)[None, :, None])
    return jnp.sum(jnp.cumprod(dead.astype(I32), axis=-1), axis=-1).astype(I32).reshape(-1)


def _fox_attention(q, gate, ka, vt, cum, logit_bound, batch, seq, tq=512):
    n = batch * seq
    nq = seq // tq
    hp = FOX_HEADS // 2
    cum_t = cum[:, :FOX_HEADS].reshape(batch, seq, hp, 2).transpose(0, 2, 3, 1).reshape(batch * hp, 2, seq)
    jlo = _fox_first_block(cum_t, logit_bound, tq)
    grid_spec = pltpu.PrefetchScalarGridSpec(
        num_scalar_prefetch=1,
        grid=(batch, hp, nq),
        in_specs=[
            pl.BlockSpec((tq, 128), lambda b, h, i, jlo: (b * nq + i, h)),
            pl.BlockSpec((seq, 256), lambda b, h, i, jlo: (b, h)),
            pl.BlockSpec((128, seq), lambda b, h, i, jlo: (h, b)),
            pl.BlockSpec((tq, 128), lambda b, h, i, jlo: (b * nq + i, h)),
        ],
        out_specs=pl.BlockSpec((tq, 128), lambda b, h, i, jlo: (b * nq + i, h)),
        scratch_shapes=[pltpu.VMEM((2, tq, 256), BF16), pltpu.VMEM((2, 8, tq), F32),
                        pltpu.VMEM((2, 8, tq), F32), pltpu.VMEM((2, FOX_HD, tq), F32)],
    )
    return pl.pallas_call(
        functools.partial(_fox_attn_kernel, tq=tq, nq=nq),
        grid_spec=grid_spec,
        out_shape=jax.ShapeDtypeStruct((n, D_MODEL), BF16),
        compiler_params=_cparams(("parallel", "parallel", "arbitrary")),
        name="fox_attention",
    )(jlo, q, ka, vt, gate)


def _fox_mixer(h, g_mix, w_in, f_bias, q_norm, k_norm, batch, seq):
    q, gate, ka, vt, cum = _fox_proj(h, g_mix, w_in, f_bias, q_norm, k_norm, seq)
    logit_bound = 1.02 * FOX_HD ** 0.5 * jnp.max(jnp.abs(q_norm.astype(F32))) * jnp.max(jnp.abs(k_norm.astype(F32)))
    return _fox_attention(q, gate, ka, vt, cum, logit_bound, batch, seq)


def _rglru_kernel(gate_ref, u_ref, par_ref, wa_ref, wx_ref, o_ref, prev_ref, hc_ref, *, tt):
    @pl.when(pl.program_id(1) == 0)
    def _():
        prev_ref[...] = jnp.zeros_like(prev_ref)
        hc_ref[...] = jnp.zeros_like(hc_ref)

    u = u_ref[...]
    ext = jnp.concatenate([prev_ref[...], u], axis=0)
    conv = par_ref[4:5, :] + u * par_ref[3:4, :]
    for shift in range(1, CONV_W):
        conv = conv + pltpu.roll(ext, shift, 0)[8:, :] * par_ref[3 - shift:4 - shift, :]
    prev_ref[...] = u[tt - 8:, :]

    cb = conv.astype(BF16)
    ra, ia = [], []
    for nb in range(RG_BLOCKS):
        blk = cb[:, nb * RG_BW:(nb + 1) * RG_BW]
        ra.append(jnp.dot(blk, wa_ref[nb], preferred_element_type=F32))
        ia.append(jnp.dot(blk, wx_ref[nb], preferred_element_type=F32))
    r = jax.nn.sigmoid(jnp.concatenate(ra, axis=1) + par_ref[5:6, :])
    ig = jax.nn.sigmoid(jnp.concatenate(ia, axis=1) + par_ref[6:7, :])
    lam = par_ref[7:8, :]
    softplus = jnp.maximum(-lam, 0.0) + jnp.log1p(jnp.exp(-jnp.abs(lam)))
    log_a = -RG_C * r * softplus
    a = jnp.exp(log_a)
    b = jnp.sqrt(1.0 - a * a) * (ig * conv)

    a = a.reshape(tt // 8, 8, a.shape[-1])
    b = b.reshape(a.shape)
    within = lax.broadcasted_iota(I32, a.shape, 1)
    for dist in (1, 2, 4):
        ok = within >= dist
        a_sh = pltpu.roll(a, dist, 1)
        b_sh = pltpu.roll(b, dist, 1)
        b = jnp.where(ok, a * b_sh + b, b)
        a = jnp.where(ok, a * a_sh, a)
    a = a.reshape(tt, a.shape[-1])
    b = b.reshape(a.shape)
    gt = gate_ref[...]
    gelu = 0.5 * gt * (1.0 + jnp.tanh(0.7978845608028654 * (gt + 0.044715 * gt * gt * gt)))
    h = hc_ref[0:1, :]
    for g in range(tt // 8):
        sl = slice(g * 8, (g + 1) * 8)
        hg = b[sl] + a[sl] * h
        o_ref[sl, :] = (hg * gelu[sl]).astype(o_ref.dtype)
        h = hg[7:8, :]
    hc_ref[...] = jnp.broadcast_to(h, hc_ref.shape)


def _rglru_mixer(h, g_mix, w_in, conv_w, conv_b, w_a, b_a, w_x, b_x, lam, batch, seq, tt=256):
    n = batch * seq
    nt = seq // tt
    w = D_MODEL
    proj = _norm_proj(h, g_mix, w_in.astype(BF16))
    par = jnp.concatenate([conv_w.astype(F32), conv_b.reshape(1, w), b_a.reshape(1, w),
                           b_x.reshape(1, w), lam.reshape(1, w)], axis=0).astype(F32)
    return pl.pallas_call(
        functools.partial(_rglru_kernel, tt=tt),
        grid=(batch, nt),
        in_specs=[pl.BlockSpec((tt, w), lambda b, i: (b * nt + i, 0)),
                  pl.BlockSpec((tt, w), lambda b, i: (b * nt + i, 1)),
                  _full((8, w)), _full((RG_BLOCKS, RG_BW, RG_BW)), _full((RG_BLOCKS, RG_BW, RG_BW))],
        out_specs=pl.BlockSpec((tt, w), lambda b, i: (b * nt + i, 0)),
        out_shape=jax.ShapeDtypeStruct((n, w), BF16),
        scratch_shapes=[pltpu.VMEM((8, w), F32), pltpu.VMEM((8, w), F32)],
        compiler_params=_cparams(("parallel", "arbitrary")),
        name="rglru",
    )(proj, proj, par, w_a.astype(BF16), w_x.astype(BF16))


def _outproj_router_kernel(a_ref, h_ref, wo_ref, g_ref, wr_ref, br_ref, tri_ref, lst_ref,
                           hm_ref, xn_ref, gate_ref, pos_ref, cnt_ref):
    hm = h_ref[...] + jnp.dot(a_ref[...], wo_ref[...], preferred_element_type=F32)
    hm_ref[...] = hm
    xn = _rms(hm, g_ref[...])
    x_hi, x_lo = _split2(xn)
    xn_ref[...] = x_hi
    w_hi, w_lo = _split2(wr_ref[...])
    logit = _bdot_nt(w_hi, x_hi) + (_bdot_nt(w_hi, x_lo) + _bdot_nt(w_lo, x_hi)) + br_ref[:, 0:1]
    ne, tm = logit.shape
    eidx = lax.broadcasted_iota(I32, (ne, tm), 0)
    work = logit
    vals, hots = [], []
    for _ in range(TOP_K):
        mx = jnp.max(work, axis=0, keepdims=True)
        pick = jnp.min(jnp.where(work == mx, eidx, ne), axis=0, keepdims=True)
        hot = eidx == pick
        work = jnp.where(hot, -jnp.inf, work)
        vals.append(mx)
        hots.append(hot)
    ex = [jnp.exp(v - vals[0]) for v in vals]
    den = ex[0] + ex[1] + ex[2] + ex[3]
    gate_ref[...] = jnp.concatenate([e / den for e in ex], axis=0)
    tok_hot = (hots[0] | hots[1] | hots[2] | hots[3]).astype(BF16)
    tri = tri_ref[...]
    count = jnp.zeros((ne, 1), F32)
    pieces = []
    for j in range(tm // 128):
        hb = tok_hot[:, j * 128:(j + 1) * 128]
        pieces.append(jnp.dot(hb, tri, preferred_element_type=F32) + count)
        count = count + jnp.sum(hb.astype(F32), axis=1, keepdims=True)
    before = jnp.concatenate(pieces, axis=1)
    units = jnp.floor((count + (STRIP_ALIGN - 1)) * (1.0 / STRIP_ALIGN))
    units_b = jnp.broadcast_to(units, (ne, 128)).astype(BF16)
    start = STRIP_ALIGN * jnp.dot(lst_ref[...], units_b, preferred_element_type=F32)[:, 0:1]
    where = before + start
    pos_ref[...] = jnp.concatenate(
        [jnp.sum(jnp.where(h, where, 0.0), axis=0, keepdims=True) for h in hots], axis=0).astype(I32)
    cnt_ref[...] = jnp.broadcast_to(STRIP_ALIGN * units, cnt_ref.shape)


def _outproj_router(act, h, w_out, g_ffn, w_r, b_r):
    n, d = h.shape
    tm = MOE_TILE
    tri = jnp.asarray(np.triu(np.ones((128, 128), np.float32), 1), BF16)
    lst =jnp.asarray(np.tril(np.ones((N_EXPERTS, N_EXPERTS), np.float32), -1), BF16)
    row = lambda i: (i, 0)
    colb = lambda i: (0, i)
    return pl.pallas_call(
        _outproj_router_kernel,
        grid=(n // tm,),
        in_specs=[pl.BlockSpec((tm, d), row), pl.BlockSpec((tm, d), row), _full((d, d)), _full((1, d)),
                  _full((N_EXPERTS, d)), _full((N_EXPERTS, 128)), _full((128, 128)),
                  _full((N_EXPERTS, N_EXPERTS))],
        out_specs=[pl.BlockSpec((tm, d), row), pl.BlockSpec((tm, d), row),
                   pl.BlockSpec((TOP_K, tm), colb), pl.BlockSpec((TOP_K, tm), colb),
                   pl.BlockSpec((N_EXPERTS, 128), row)],
        out_shape=[jax.ShapeDtypeStruct((n, d), F32), jax.ShapeDtypeStruct((n, d), BF16),
                   jax.ShapeDtypeStruct((TOP_K, n), F32), jax.ShapeDtypeStruct((TOP_K, n), I32),
                   jax.ShapeDtypeStruct((n // tm * N_EXPERTS, 128), F32)],
        compiler_params=_cparams(("parallel",)),
        name="outproj_router",
    )(act, h, w_out.astype(BF16), g_ffn.reshape(1, d), w_r.T.astype(F32),
      jnp.broadcast_to(b_r.astype(F32)[:, None], (N_EXPERTS, 128)), tri, lst)


def _for_strip_chunks(nrows, max_rows, fn):
    size = max_rows
    while size >= STRIP_ALIGN:
        @pl.when((nrows & size) != 0)
        def _(size=size):
            fn(pl.multiple_of(nrows & ~(2 * size - 1), STRIP_ALIGN), size)
        size //= 2


def _for_tile_strips(n8_ref, sbase_ref, gdst_ref, tile, fn):
    def per_expert(e, carry):
        i = tile * N_EXPERTS + e
        so = sbase_ref[i]
        gd = gdst_ref[i]
        _for_strip_chunks(n8_ref[i], MOE_TILE,
                          lambda o, size: fn(pl.multiple_of(so + o, STRIP_ALIGN), pl.multiple_of(gd + o, STRIP_ALIGN), size))
        return carry

    lax.fori_loop(0, N_EXPERTS, per_expert, 0)


def _wait_tile_strips(n8_ref, sbase_ref, tile, wait_rows):
    last = tile * N_EXPERTS + (N_EXPERTS - 1)
    total = sbase_ref[last] + n8_ref[last]
    size = STAGE_ROWS.bit_length() - 1
    size = 1 << size
    while size >= STRIP_ALIGN:
        @pl.when((total & size) != 0)
        def _(size=size):
            wait_rows(size)
        size //= 2


def _dispatch_kernel(n8_ref, sbase_ref, gdst_ref, tot_ref, pst_ref, x_ref, pos_ref, xs_hbm,
                     stage, zblk, rsem, zsem):
    tile = pl.program_id(0)
    slot = tile % 2
    xb = x_ref[...]
    for c in range(STAGE_ROWS // ONEHOT_CHUNK):
        rows = lax.broadcasted_iota(I32, (ONEHOT_CHUNK, MOE_TILE), 0) + c * ONEHOT_CHUNK
        onehot = jnp.zeros((ONEHOT_CHUNK, MOE_TILE), F32)
        for k in range(TOP_K):
            onehot = jnp.where(rows == pos_ref[k:k + 1, :], 1.0, onehot)
        stage[slot, c * ONEHOT_CHUNK:(c + 1) * ONEHOT_CHUNK, :] = jnp.dot(
            onehot.astype(BF16), xb, preferred_element_type=F32)

    def copy(sl, so, gd, size):
        return pltpu.make_async_copy(stage.at[sl, pl.ds(so, size), :], xs_hbm.at[pl.ds(gd, size), :], rsem.at[sl])

    _for_tile_strips(n8_ref, sbase_ref, gdst_ref, tile, lambda so, gd, size: copy(slot, so, gd, size).start())

    @pl.when(tile > 0)
    def _():
        _wait_tile_strips(n8_ref, sbase_ref, tile - 1, lambda size: copy(1 - slot, 0, 0, size).wait())

    @pl.when(tile == 0)
    def _():
        zblk[...] = jnp.zeros_like(zblk)

        def zero(first, size):
            cp = pltpu.make_async_copy(zblk.at[pl.ds(0, size), :], xs_hbm.at[pl.ds(first, size), :], zsem)
            cp.start()
            cp.wait()

        def per_expert(e, carry):
            tot = tot_ref[e]
            first = pst_ref[e] + tot
            npad = ((tot + (MOE_BLK - 1)) // MOE_BLK) * MOE_BLK - tot
            _for_strip_chunks(npad, MOE_BLK // 2,
                              lambda o, size: zero(pl.multiple_of(first + o, STRIP_ALIGN), size))
            return carry

        lax.fori_loop(0, N_EXPERTS, per_expert, 0)

        last = N_EXPERTS - 1
        used = (pst_ref[last] + tot_ref[last] + (MOE_BLK - 1)) // MOE_BLK

        def ztail(b, c):
            zero(pl.multiple_of(b * MOE_BLK, MOE_BLK), MOE_BLK)
            return c

        lax.fori_loop(used, xs_hbm.shape[0] // MOE_BLK, ztail, 0)

    @pl.when(tile == pl.num_programs(0) - 1)
    def _():
        _wait_tile_strips(n8_ref, sbase_ref, tile, lambda size: copy(slot, 0, 0, size).wait())


def _dispatch(xn, pos, n8, sbase, gdst, total, pstart, cap):
    n, d = xn.shape
    tm = MOE_TILE
    grid_spec = pltpu.PrefetchScalarGridSpec(
        num_scalar_prefetch=5,
        grid=(n // tm,),
        in_specs=[pl.BlockSpec((tm, d), lambda i, *_: (i, 0)), pl.BlockSpec((TOP_K, tm), lambda i, *_: (0, i))],
        out_specs=pl.BlockSpec(memory_space=pl.ANY),
        scratch_shapes=[pltpu.VMEM((2, STAGE_ROWS, d), F32), pltpu.VMEM((MOE_BLK, d), F32),
                        pltpu.SemaphoreType.DMA((2,)), pltpu.SemaphoreType.DMA],
    )
    return pl.pallas_call(
        _dispatch_kernel,
        grid_spec=grid_spec,
        out_shape=jax.ShapeDtypeStruct((cap, d), F32),
        compiler_params=_cparams(("arbitrary",)),
        name="moe_dispatch",
    )(n8, sbase, gdst, total, pstart, xn, pos)


def _expert_kernel(be_ref, nu_ref, x_ref, wgu_ref, bgu_ref, wdn_ref, bdn_ref, y_ref, wgu_bf, wdn_bf):
    d = D_MODEL
    b = pl.program_id(0)

    @pl.when((b == 0) | (be_ref[b] != be_ref[jnp.maximum(b - 1, 0)]))
    def _():
        wgu_bf[...] = wgu_ref[...].astype(BF16)
        wdn_bf[...] = wdn_ref[...].astype(BF16)

    @pl.when(b < nu_ref[0])
    def _():
        xb = x_ref[...].astype(BF16)
        acc = None
        for c in range(d // EXPERT_CHUNK):
            lo, hi = c * EXPERT_CHUNK, (c + 1) * EXPERT_CHUNK
            glu = jnp.dot(xb, wgu_bf[:, lo:hi], preferred_element_type=F32) + bgu_ref[:, lo:hi]
            lin = jnp.dot(xb, wgu_bf[:, d + lo:d + hi], preferred_element_type=F32) + bgu_ref[:, d + lo:d + hi]
            glu = jnp.minimum(glu, SWIGLU_LIMIT)
            lin = jnp.clip(lin, -SWIGLU_LIMIT, SWIGLU_LIMIT)
            act = glu * jax.nn.sigmoid(SWIGLU_ALPHA * glu) * (lin + 1.0)
            part = jnp.dot(act.astype(BF16), wdn_bf[lo:hi, :], preferred_element_type=F32)
            acc = part if acc is None else acc + part
        y_ref[...] = acc + bdn_ref[...]

    @pl.when(b >= nu_ref[0])
    def _():
        y_ref[...] = jnp.zeros_like(y_ref)


def _experts(xs, block_expert, n_used, w_gu, b_gu, w_dn, b_dn):
    cap, d = xs.shape
    nb = cap // MOE_BLK
    ne = w_gu.shape[0]

    def xmap(b, be, nu):
        return (jnp.minimum(b, nu[0] - 1), 0)

    def wmap(b, be, nu):
        return (be[b], 0, 0)

    grid_spec = pltpu.PrefetchScalarGridSpec(
        num_scalar_prefetch=2,
        grid=(nb,),
        in_specs=[pl.BlockSpec((MOE_BLK, d), xmap),
                  pl.BlockSpec((None, d, 2 * d), wmap), pl.BlockSpec((None, 1, 2 * d), wmap),
                  pl.BlockSpec((None, d, d), wmap), pl.BlockSpec((None, 1, d), wmap)],
        out_specs=pl.BlockSpec((MOE_BLK, d), lambda b, be, nu: (b, 0)),
        scratch_shapes=[pltpu.VMEM((d, 2 * d), BF16), pltpu.VMEM((d, d), BF16)],
    )
    return pl.pallas_call(
        _expert_kernel,
        grid_spec=grid_spec,
        out_shape=jax.ShapeDtypeStruct((cap, d), F32),
        compiler_params=_cparams(("arbitrary",)),
        name="moe_experts",
    )(block_expert, n_used, xs, w_gu, b_gu.reshape(ne, 1, 2 * d), w_dn, b_dn.reshape(ne, 1, d))


def _combine_ple_kernel(n8_ref, sbase_ref, gdst_ref, y_hbm, hm_ref, gate_ref, pos_ref, p_ref, pw_ref, pn_ref,
                        gn_ref, gw_ref, o_ref, stage, rsem):
    tile = pl.program_id(0)
    slot = tile % 2

    def copy(sl, so, gd, size):
        return pltpu.make_async_copy(y_hbm.at[pl.ds(gd, size), :], stage.at[sl, pl.ds(so, size), :], rsem.at[sl])

    def fetch(t, sl):
        _for_tile_strips(n8_ref, sbase_ref, gdst_ref, t, lambda so, gd, size: copy(sl, so, gd, size).start())

    @pl.when(tile == 0)
    def _():
        stage[...] = jnp.zeros_like(stage)
        fetch(0, 0)

    @pl.when(tile + 1 < pl.num_programs(0))
    def _():
        fetch(tile + 1, 1 - slot)

    ple = _rms(jnp.dot(p_ref[...].astype(BF16), pw_ref[...], preferred_element_type=F32), pn_ref[...])
    _wait_tile_strips(n8_ref, sbase_ref, tile, lambda size: copy(slot, 0, 0, size).wait())
    h2 = hm_ref[...]
    for c in range(STAGE_ROWS // COMBINE_CHUNK):
        lanes = lax.broadcasted_iota(I32, (MOE_TILE, COMBINE_CHUNK), 1) + c * COMBINE_CHUNK
        weights = jnp.zeros((MOE_TILE, COMBINE_CHUNK), F32)
        for k in range(TOP_K):
            weights = jnp.where(lanes == pos_ref[:, k:k + 1], gate_ref[:, k:k + 1], weights)
        staged = stage[slot, c * COMBINE_CHUNK:(c + 1) * COMBINE_CHUNK, :].astype(BF16)
        h2 = h2 + jnp.dot(weights.astype(BF16), staged, preferred_element_type=F32)
    gate = jax.nn.sigmoid(jnp.dot(_rms(h2, gn_ref[...]).astype(BF16), gw_ref[...], preferred_element_type=F32))
    o_ref[...] = h2 + ple * gate


def _combine_ple(y, pos, n8, sbase, gdst, h_mid, gates, p, ple_w, ple_norm, gate_norm, gate_w):
    n, d = h_mid.shape
    tm = MOE_TILE
    row = lambda i, *_: (i, 0)
    full = lambda shape: pl.BlockSpec(shape, lambda i, *_: (0,) * len(shape))
    grid_spec = pltpu.PrefetchScalarGridSpec(
        num_scalar_prefetch=3,
        grid=(n // tm,),
        in_specs=[pl.BlockSpec(memory_space=pl.ANY), pl.BlockSpec((tm, d), row), pl.BlockSpec((tm, TOP_K), row),
                  pl.BlockSpec((tm, TOP_K), row), pl.BlockSpec((tm, PLE_DIM), row), full((PLE_DIM, d)),
                  full((1, d)), full((1, d)), full((d, d))],
        out_specs=pl.BlockSpec((tm, d), row),
        scratch_shapes=[pltpu.VMEM((2, STAGE_ROWS, d), F32), pltpu.SemaphoreType.DMA((2,))],
    )
    return pl.pallas_call(
        _combine_ple_kernel,
        grid_spec=grid_spec,
        out_shape=jax.ShapeDtypeStruct((n, d), F32),
        compiler_params=_cparams(("arbitrary",)),
        name="moe_combine_ple",
    )(n8, sbase, gdst, y, h_mid, gates.T, pos.T, p, ple_w.astype(BF16), ple_norm.reshape(1, d),
      gate_norm.reshape(1, d), gate_w.astype(BF16))


def _moe_ple(act, h, w_out, g_ffn, w_r, b_r, layer, w_gu, b_gu, w_dn, b_dn, p, ple_w, ple_norm, gate_norm, gate_w):
    n, d = h.shape
    ntiles = n // MOE_TILE
    h_mid, xn, gates, pos, cnt = _outproj_router(act, h, w_out, g_ffn, w_r, b_r)
    n8 = cnt[:, 0].astype(I32).reshape(ntiles, N_EXPERTS)
    sbase = jnp.cumsum(n8, axis=1) - n8
    total = jnp.sum(n8, axis=0)
    padded = (total + MOE_BLK - 1) // MOE_BLK * MOE_BLK
    pend = jnp.cumsum(padded)
    pstart = pend - padded
    gdst = pstart[None, :] + jnp.cumsum(n8, axis=0) - n8
    nb = -(-(n * TOP_K + ntiles * N_EXPERTS * (STRIP_ALIGN - 1)) // MOE_BLK) + N_EXPERTS
    cap = nb * MOE_BLK
    block_start = jnp.arange(nb, dtype=I32) * MOE_BLK
    block_expert = jnp.minimum(jnp.sum(pend[None, :] <= block_start[:, None], axis=1), N_EXPERTS - 1).astype(I32)
    n_used = (pend[-1:] // MOE_BLK).astype(I32)
    n8f, sbf, gdf = n8.reshape(-1), sbase.reshape(-1).astype(I32), gdst.reshape(-1).astype(I32)
    xs = _dispatch(xn, pos, n8f, sbf, gdf, total.astype(I32), pstart.astype(I32), cap)
    y = _experts(xs, block_expert + layer * N_EXPERTS, n_used, w_gu, b_gu, w_dn, b_dn)
    return _combine_ple(y, pos, n8f, sbf, gdf, h_mid, gates, p, ple_w, ple_norm, gate_norm, gate_w)


def kernel(x, p, norm_mix, norm_ffn, hg_w_in, hg_w_out, hg_gnorm, hg_lb_param, fox_w_in, fox_f_bias, fox_qnorm, fox_knorm, fox_w_out, rg_w_in, rg_conv_w, rg_conv_b, rg_wa, rg_ba, rg_wx, rg_bx, rg_lambda, rg_w_out, router_w, router_b, moe_w_gu, moe_b_gu, moe_w_dn, moe_b_dn, ple_w, ple_norm, ple_gate_norm, ple_gate_w):
    batch, seq, d = x.shape
    depth = p.shape[0]
    n = batch * seq
    lb_all = jnp.cumsum(jax.nn.softmax(hg_lb_param.astype(F32), axis=0), axis=0)
    lb_all = lb_all - lb_all[0]
    h = x.reshape(n, d).astype(F32)
    ne = depth * N_EXPERTS
    w_gu = moe_w_gu.astype(F32).reshape(ne, d, 2 * d)
    b_gu = moe_b_gu.astype(F32).reshape(ne, 2 * d)
    w_dn = moe_w_dn.astype(F32).reshape(ne, d, d)
    b_dn = moe_b_dn.astype(F32).reshape(ne, d)
    for i in range(depth):
        j = i // 3
        kind = i % 3
        if kind == 0:
            act = _hgrn2_mixer(h, norm_mix[i], hg_w_in[j], None, hg_gnorm[j], lb_all[i], batch, seq)
            w_out = hg_w_out[j]
        elif kind == 1:
            act = _fox_mixer(h, norm_mix[i], fox_w_in[j], fox_f_bias[j], fox_qnorm[j], fox_knorm[j], batch, seq)
            w_out = fox_w_out[j]
        else:
            act = _rglru_mixer(h, norm_mix[i], rg_w_in[j], rg_conv_w[j], rg_conv_b[j], rg_wa[j], rg_ba[j],
                               rg_wx[j], rg_bx[j], rg_lambda[j], batch, seq)
            w_out = rg_w_out[j]
        h = _moe_ple(act, h, w_out, norm_ffn[i], router_w[i], router_b[i], i, w_gu, b_gu, w_dn, b_dn,
                     p[i].reshape(n, PLE_DIM), ple_w[i], ple_norm[i], ple_gate_norm[i], ple_gate_w[i])
    return h.reshape(batch, seq, d)
```
